```python
import math, functools
import jax, jax.numpy as jnp
from jax import lax
import numpy as np

D_MODEL = 1024
BATCH = 16
SEQ = 2048
DEPTH = 1

D_MIX = D_MODEL
ATTN_WIDTH = D_MIX // 2
ATTN_HEADS = 8
ATTN_HEAD_DIM = ATTN_WIDTH // ATTN_HEADS
MLSTM_WIDTH = D_MIX - ATTN_WIDTH
MLSTM_HEADS = 4
MLSTM_HEAD_DIM = MLSTM_WIDTH // MLSTM_HEADS
DILATED_CONFIGS = ((128, 1), (512, 4), (2048, 16))
ATTN_BLOCK = 128
MLSTM_CHUNK = 64
CONV_WIDTH = 4
D_FF = ((8 * D_MODEL // 3) + 127) // 128 * 128
NORM_EPS = 1e-6
D_IN = 3 * ATTN_WIDTH + 4 * MLSTM_WIDTH + 2 * MLSTM_HEADS
SPLIT_POINTS = (ATTN_WIDTH, 2 * ATTN_WIDTH, 3 * ATTN_WIDTH,
                3 * ATTN_WIDTH + MLSTM_WIDTH, 3 * ATTN_WIDTH + 2 * MLSTM_WIDTH,
                3 * ATTN_WIDTH + 3 * MLSTM_WIDTH, 3 * ATTN_WIDTH + 4 * MLSTM_WIDTH,
                3 * ATTN_WIDTH + 4 * MLSTM_WIDTH + MLSTM_HEADS)

kernel_name = "hybrid_dilated_attn_mlstm_macaron"


def rmsnorm(x, w):
    xf = x.astype(jnp.float32)
    y = xf * lax.rsqrt(jnp.mean(xf * xf, axis=-1, keepdims=True) + NORM_EPS)
    return (y * w.astype(jnp.float32)).astype(x.dtype)


def swiglu(x, w_gate, w_up, w_down):
    return (jax.nn.silu(x @ w_gate) * (x @ w_up)) @ w_down


def alibi_slopes(n_heads):
    h = np.arange(1, n_heads + 1, dtype=np.float32)
    return jnp.asarray(2.0 ** (-8.0 * h / n_heads), dtype=jnp.float32)


def split_heads(t, n_heads):
    B, S, _ = t.shape
    return t.reshape(B, S, n_heads, -1).transpose(0, 2, 1, 3)


def merge_heads(t):
    B, H, S, dh = t.shape
    return t.transpose(0, 2, 1, 3).reshape(B, S, H * dh)


def dilated_branch(q, k, v, slopes, window, dilation):
    B, H, S, hd = q.shape
    blk = ATTN_BLOCK
    n_steps = window // dilation
    assert n_steps <= blk
    L = S // dilation
    nb = -(-L // blk)
    Lp = nb * blk

    def residues(t):
        return t.reshape(B, H, L, dilation, hd).transpose(0, 1, 3, 2, 4)

    qb = jnp.pad(residues(q), ((0, 0), (0, 0), (0, 0), (0, Lp - L), (0, 0)))
    qb = qb.reshape(B, H, dilation, nb, blk, hd)

    def key_blocks(t):
        tp = jnp.pad(residues(t), ((0, 0), (0, 0), (0, 0), (blk, Lp - L), (0, 0)))
        tp = tp.reshape(B, H, dilation, nb + 1, blk, hd)
        return jnp.concatenate([tp[:, :, :, :-1], tp[:, :, :, 1:]], axis=4)

    kb, vb = key_blocks(k), key_blocks(v)
    steps = np.arange(blk)[:, None] + blk - np.arange(2 * blk)[None, :]
    band = (steps >= 0) & (steps <= n_steps)
    exists = (np.arange(nb)[:, None, None] > 0) | (np.arange(2 * blk)[None, None, :] >= blk)
    mask = jnp.asarray(band[None] & exists)
    dist = jnp.asarray((steps * dilation).astype(np.float32))
    s = jnp.einsum('bhrnqd,bhrnkd->bhrnqk', qb, kb)
    s = s - slopes.reshape(H, 1, 1, 1, 1) * dist
    s = jnp.where(mask, s, -jnp.inf)
    m = jnp.max(s, axis=-1, keepdims=True)
    p = jnp.exp(s - m)
    denom = jnp.sum(p, axis=-1, keepdims=True)
    o = jnp.einsum('bhrnqk,bhrnkd->bhrnqd', p, vb) / denom
    lse = (m + jnp.log(denom))[..., 0]
    o = o.reshape(B, H, dilation, Lp, hd)[:, :, :, :L].transpose(0, 1, 3, 2, 4).reshape(B, H, S, hd)
    lse = lse.reshape(B, H, dilation, Lp)[:, :, :, :L].transpose(0, 1, 3, 2).reshape(B, H, S)
    return o, lse


def dilated_attention(q, k, v):
    slopes = alibi_slopes(q.shape[1])
    outs, lses = [], []
    for window, dilation in DILATED_CONFIGS:
        o, lse = dilated_branch(q, k, v, slopes, window, dilation)
        outs.append(o)
        lses.append(lse)
    w = jax.nn.softmax(jnp.stack(lses, axis=0), axis=0)
    return jnp.einsum('gbhs,gbhsd->bhsd', w, jnp.stack(outs, axis=0))


def causal_dwconv(x, w, b):
    K = w.shape[0]
    S = x.shape[1]
    xp = jnp.pad(x, ((0, 0), (K - 1, 0), (0, 0)))
    y = b
    for j in range(K):
        y = y + xp[:, j:j + S] * w[j]
    return y


def mlstm_chunkwise(q, k, v, i_pre, f_pre):
    B, H, S, dh = q.shape
    L = MLSTM_CHUNK
    nc = S // L
    logf = jax.nn.log_sigmoid(f_pre)

    def chunks(t):
        t = t.reshape((B, H, nc, L) + t.shape[3:])
        return jnp.moveaxis(t, 2, 0)

    causal = jnp.asarray(np.tril(np.ones((L, L), dtype=bool)))

    def step(carry, inp):
        C, n, m = carry
        qc, kc, vc, ic, lfc = inp
        b = jnp.cumsum(lfc, axis=-1)
        D = b[..., :, None] - b[..., None, :] + ic[..., None, :]
        D = jnp.where(causal, D, -jnp.inf)
        g = b + m[..., None]
        m_row = jnp.maximum(g, jnp.max(D, axis=-1))
        Dw = jnp.exp(D - m_row[..., None])
        gw = jnp.exp(g - m_row)
        sc = jnp.einsum('bhtd,bhsd->bhts', qc, kc) * Dw
        num = gw[..., None] * jnp.einsum('bhtd,bhde->bhte', qc, C) + jnp.einsum('bhts,bhse->bhte', sc, vc)
        den = gw * jnp.einsum('bhtd,bhd->bht', qc, n) + jnp.sum(sc, axis=-1)
        h = num / jnp.maximum(jnp.abs(den), jnp.exp(-m_row))[..., None]
        bL = b[..., -1]
        a = bL[..., None] - b + ic
        m_new = jnp.maximum(bL + m, jnp.max(a, axis=-1))
        decay = jnp.exp(bL + m - m_new)
        w = jnp.exp(a - m_new[..., None])
        C_new = decay[..., None, None] * C + jnp.einsum('bhs,bhsd,bhse->bhde', w, kc, vc)
        n_new = decay[..., None] * n + jnp.einsum('bhs,bhsd->bhd', w, kc)
        return (C_new, n_new, m_new), h

    init = (jnp.zeros((B, H, dh, dh), jnp.float32), jnp.zeros((B, H, dh), jnp.float32),
            jnp.zeros((B, H), jnp.float32))
    _, hs = lax.scan(step, init, (chunks(q), chunks(k), chunks(v), chunks(i_pre), chunks(logf)))
    return jnp.moveaxis(hs, 0, 2).reshape(B, H, S, dh)


def head_rmsnorm(t, w):
    return t * lax.rsqrt(jnp.mean(t * t, axis=-1, keepdims=True) + NORM_EPS) * w


def hybrid_mixer(h, w_in, q_norm_w, k_norm_w, conv_w, conv_b, i_bias, f_bias,
                 attn_out_gain, mlstm_out_gain, w_out):
    dtype = h.dtype
    proj = (h @ w_in).astype(jnp.float32)
    qa, ka, va, qm, km, vm, om, ig, fg = jnp.split(proj, SPLIT_POINTS, axis=-1)

    scale = ATTN_HEAD_DIM ** -0.5
    qa = head_rmsnorm(split_heads(qa, ATTN_HEADS), q_norm_w.astype(jnp.float32)) * scale
    ka = head_rmsnorm(split_heads(ka, ATTN_HEADS), k_norm_w.astype(jnp.float32))
    va = split_heads(va, ATTN_HEADS)
    attn = dilated_attention(qa, ka, va)
    attn = head_rmsnorm(attn, attn_out_gain.astype(jnp.float32).reshape(ATTN_HEADS, 1, ATTN_HEAD_DIM))

    qk = jax.nn.silu(causal_dwconv(jnp.concatenate([qm, km], axis=-1),
                                   conv_w.astype(jnp.float32), conv_b.astype(jnp.float32)))
    qm, km = jnp.split(qk, 2, axis=-1)
    qm = split_heads(qm, MLSTM_HEADS)
    km = split_heads(km, MLSTM_HEADS) * (MLSTM_HEAD_DIM ** -0.5)
    vm = split_heads(vm, MLSTM_HEADS)
    i_pre = (ig + i_bias.astype(jnp.float32)).transpose(0, 2, 1)
    f_pre = (fg + f_bias.astype(jnp.float32)).transpose(0, 2, 1)
    hm = mlstm_chunkwise(qm, km, vm, i_pre, f_pre)
    hm = jax.nn.sigmoid(split_heads(om, MLSTM_HEADS)) * hm
    hm = head_rmsnorm(hm, mlstm_out_gain.astype(jnp.float32).reshape(MLSTM_HEADS, 1, MLSTM_HEAD_DIM))

    y = jnp.concatenate([merge_heads(attn), merge_heads(hm)], axis=-1).astype(dtype)
    return y @ w_out


def setup_inputs(seed: int = 0) -> dict:
    key = jax.random.key(seed)
    ks = jax.random.split(key, 24)
    f32 = jnp.float32

    def normal(k, shape, scale):
        return jax.random.normal(k, shape, f32) * scale

    def gain(k, shape):
        return 1.0 + 0.05 * jax.random.normal(k, shape, f32)

    f_bias = (jnp.linspace(3.0, 6.0, MLSTM_HEADS, dtype=f32)[None, :]
              + 0.1 * jax.random.normal(ks[12], (DEPTH, MLSTM_HEADS), f32))
    return {
        "x": jax.random.normal(ks[0], (BATCH, SEQ, D_MODEL), f32),
        "ffn1_norm_w": gain(ks[1], (DEPTH, D_MODEL)),
        "ffn1_w_gate": normal(ks[2], (DEPTH, D_MODEL, D_FF), D_MODEL ** -0.5),
        "ffn1_w_up": normal(ks[3], (DEPTH, D_MODEL, D_FF), D_MODEL ** -0.5),
        "ffn1_w_down": normal(ks[4], (DEPTH, D_FF, D_MODEL), D_FF ** -0.5),
        "mix_norm_w": gain(ks[5], (DEPTH, D_MODEL)),
        "w_in": normal(ks[6], (DEPTH, D_MODEL, D_IN), D_MODEL ** -0.5),
        "q_norm_w": gain(ks[7], (DEPTH, ATTN_HEAD_DIM)),
        "k_norm_w": gain(ks[8], (DEPTH, ATTN_HEAD_DIM)),
        "conv_w": normal(ks[9], (DEPTH, CONV_WIDTH, 2 * MLSTM_WIDTH), CONV_WIDTH ** -0.5),
        "conv_b": normal(ks[10], (DEPTH, 2 * MLSTM_WIDTH), 0.02),
        "i_bias": normal(ks[11], (DEPTH, MLSTM_HEADS), 0.1),
        "f_bias": f_bias,
        "attn_out_gain": gain(ks[13], (DEPTH, ATTN_WIDTH)),
        "mlstm_out_gain": gain(ks[14], (DEPTH, MLSTM_WIDTH)),
        "w_out": normal(ks[15], (DEPTH, D_MIX, D_MODEL), D_MIX ** -0.5),
        "ffn2_norm_w": gain(ks[16], (DEPTH, D_MODEL)),
        "ffn2_w_gate": normal(ks[17], (DEPTH, D_MODEL, D_FF), D_MODEL ** -0.5),
        "ffn2_w_up": normal(ks[18], (DEPTH, D_MODEL, D_FF), D_MODEL ** -0.5),
        "ffn2_w_down": normal(ks[19], (DEPTH, D_FF, D_MODEL), D_FF ** -0.5),
    }


def reference(x, ffn1_norm_w, ffn1_w_gate, ffn1_w_up, ffn1_w_down, mix_norm_w, w_in,
              q_norm_w, k_norm_w, conv_w, conv_b, i_bias, f_bias, attn_out_gain,
              mlstm_out_gain, w_out, ffn2_norm_w, ffn2_w_gate, ffn2_w_up, ffn2_w_down):
    for l in range(DEPTH):
        x = x + 0.5 * swiglu(rmsnorm(x, ffn1_norm_w[l]), ffn1_w_gate[l], ffn1_w_up[l], ffn1_w_down[l])
        x = x + hybrid_mixer(rmsnorm(x, mix_norm_w[l]), w_in[l], q_norm_w[l], k_norm_w[l],
                             conv_w[l], conv_b[l], i_bias[l], f_bias[l], attn_out_gain[l],
                             mlstm_out_gain[l], w_out[l])
        x = x + 0.5 * swiglu(rmsnorm(x, ffn2_norm_w[l]), ffn2_w_gate[l], ffn2_w_up[l], ffn2_w_down[l])
    return x
```

```python
import functools

import numpy as np
import jax
import jax.numpy as jnp
from jax import lax
from jax.experimental import pallas as pl
from jax.experimental.pallas import tpu as pltpu

F32 = jnp.float32
BF16 = jnp.bfloat16

LANES = 128
NORM_EPS = 1e-6
ATTN_HEADS = 8
ATTN_HEAD_DIM = 64
MLSTM_HEADS = 4
MLSTM_HEAD_DIM = 128
CONV_WIDTH = 4
DILATED_CONFIGS = ((128, 1), (512, 4), (2048, 16))
ATTN_BLOCK = 128
MLSTM_CHUNK = 128
NEG_BIG = -1e30
FFN_TOKENS = 512
VMEM_LIMIT = 56 * 1024 * 1024


def _const_spec(shape):
    nd = len(shape)
    return pl.BlockSpec(shape, lambda *_: (0,) * nd, pipeline_mode=pl.Buffered(1))


def _rmsnorm(x, w):
    ms = jnp.mean(x * x, axis=-1, keepdims=True)
    return x * lax.rsqrt(ms + NORM_EPS) * w


def _dot(a, b):
    return jnp.dot(a, b, preferred_element_type=F32)


def _dot_nt(a, b):
    return lax.dot_general(a, b, (((1,), (1,)), ((), ())), preferred_element_type=F32)


def _split3(x):
    h1 = x.astype(BF16)
    r1 = x - h1.astype(F32)
    h2 = r1.astype(BF16)
    h3 = (r1 - h2.astype(F32)).astype(BF16)
    return h1, h2, h3


def _swiglu_half(x, nw_ref, wg_ref, wu_ref, wd_ref):
    h = _rmsnorm(x, nw_ref[...]).astype(BF16)
    g = _dot(h, wg_ref[...])
    u = _dot(h, wu_ref[...])
    a = (g * jax.nn.sigmoid(g) * u).astype(BF16)
    return x + 0.5 * _dot(a, wd_ref[...])


def _ffn_kernel(x_ref, nw_ref, wg_ref, wu_ref, wd_ref, o_ref):
    o_ref[...] = _swiglu_half(x_ref[...], nw_ref, wg_ref, wu_ref, wd_ref)


def _outproj_ffn_kernel(x_ref, ya_ref, ym_ref, woa_ref, wom_ref, nw_ref, wg_ref, wu_ref, wd_ref, o_ref):
    x = x_ref[...] + _dot(ya_ref[...], woa_ref[...]) + _dot(ym_ref[...], wom_ref[...])
    o_ref[...] = _swiglu_half(x, nw_ref, wg_ref, wu_ref, wd_ref)


def _ffn(x, nw, wg, wu, wd, mix=None):
    n, d = x.shape
    f = wg.shape[1]
    tm = FFN_TOKENS
    tok = lambda w: pl.BlockSpec((tm, w), lambda i: (i, 0))
    w_specs = [_const_spec((1, d)), _const_spec((d, f)), _const_spec((d, f)), _const_spec((f, d))]
    if mix is None:
        body, ins, specs = _ffn_kernel, (x,), [tok(d)]
    else:
        ya, ym, woa, wom = mix
        body, ins = _outproj_ffn_kernel, (x, ya, ym, woa, wom)
        specs = [tok(d), tok(ya.shape[1]), tok(ym.shape[1]), _const_spec(woa.shape), _const_spec(wom.shape)]
    return pl.pallas_call(
        body,
        grid=(n // tm,),
        in_specs=specs + w_specs,
        out_specs=tok(d),
        out_shape=jax.ShapeDtypeStruct((n, d), F32),
        compiler_params=pltpu.CompilerParams(dimension_semantics=("parallel",), vmem_limit_bytes=VMEM_LIMIT),
        name="ffn" if mix is None else "outproj_ffn",
    )(*ins, nw, wg, wu, wd)


def _inproj_kernel(x_ref, nw_ref, w_ref, wgc_ref, wgr_ref, gbc_ref, gbr_ref, qnw_ref, knw_ref, bd_ref,
                   qa_ref, ka_ref, va_ref, qm_ref, km_ref, vm_ref, om_ref, gc_ref, gr_ref):
    h = _rmsnorm(x_ref[...], nw_ref[...]).astype(BF16)
    aw = qa_ref.shape[1]
    mw = qm_ref.shape[1]

    def head_norm(p, w):
        s1, s2, s3 = _split3(p * p)
        ss = _dot(s1, bd_ref[...]) + _dot(s2, bd_ref[...]) + _dot(s3, bd_ref[...])
        return p * lax.rsqrt(ss * (1.0 / ATTN_HEAD_DIM) + NORM_EPS) * w

    qa_ref[...] = head_norm(_dot(h, w_ref[:, 0:aw]), qnw_ref[...]) * (ATTN_HEAD_DIM ** -0.5)
    ka_ref[...] = head_norm(_dot(h, w_ref[:, aw:2 * aw]), knw_ref[...])
    va_ref[...] = _dot(h, w_ref[:, 2 * aw:3 * aw])
    base = 3 * aw
    for k, ref in enumerate((qm_ref, km_ref, vm_ref, om_ref)):
        ref[...] = _dot(h, w_ref[:, base + k * mw: base + (k + 1) * mw])
    ng = gc_ref.shape[1]
    gc_ref[...] = _dot(h, wgc_ref[...])[:, :ng] + gbc_ref[...]
    gr_ref[...] = _dot_nt(wgr_ref[...], h)[:ng, :] + gbr_ref[...]


def _inproj(x, nw, w_main, w_gate, gate_bias, qnw, knw):
    n, d = x.shape
    tm = FFN_TOKENS
    aw = ATTN_HEADS * ATTN_HEAD_DIM
    mw = MLSTM_HEADS * MLSTM_HEAD_DIM
    ng = 2 * MLSTM_HEADS
    wgc = jnp.zeros((d, LANES), BF16).at[:, :ng].set(w_gate)
    wgr = jnp.zeros((16, d), BF16).at[:ng, :].set(w_gate.T)
    gbc = gate_bias.reshape(1, ng)
    gbr = gate_bias.reshape(ng, 1)
    heads = np.arange(aw) // ATTN_HEAD_DIM
    bd = jnp.asarray(heads[:, None] == heads[None, :], BF16)
    qnw_t = jnp.tile(qnw, ATTN_HEADS).reshape(1, aw)
    knw_t = jnp.tile(knw, ATTN_HEADS).reshape(1, aw)
    tok = lambda w: pl.BlockSpec((tm, w), lambda i: (i, 0))
    out_shape = ([jax.ShapeDtypeStruct((n, aw), F32)] * 3 + [jax.ShapeDtypeStruct((n, mw), F32)] * 4
                 + [jax.ShapeDtypeStruct((n, ng), F32), jax.ShapeDtypeStruct((ng, n), F32)])
    out_specs = [tok(aw)] * 3 + [tok(mw)] * 4 + [tok(ng), pl.BlockSpec((ng, tm), lambda i: (0, i))]
    return pl.pallas_call(
        _inproj_kernel,
        grid=(n // tm,),
        in_specs=[tok(d), _const_spec((1, d)), _const_spec(w_main.shape), _const_spec(wgc.shape),
                  _const_spec(wgr.shape), _const_spec(gbc.shape), _const_spec(gbr.shape),
                  _const_spec((1, aw)), _const_spec((1, aw)), _const_spec(bd.shape)],
        out_specs=out_specs,
        out_shape=out_shape,
        compiler_params=pltpu.CompilerParams(dimension_semantics=("parallel",), vmem_limit_bytes=VMEM_LIMIT),
        name="in_proj",
    )(x, nw.reshape(1, d), w_main, wgc, wgr, gbc, gbr, qnw_t, knw_t, bd)


def _attn_kernel(slopes_ref, q_ref, k_ref, v_ref, gain_ref, o_ref, acc_s, m_s, l_s):
    blk = ATTN_BLOCK
    seq = q_ref.shape[1]
    pair = pl.program_id(1)
    lane = lax.broadcasted_iota(jnp.int32, (blk, LANES), 1)
    first = lane < ATTN_HEAD_DIM
    slope = (slopes_ref[2 * pair], slopes_ref[2 * pair + 1])

    def rows(ref, start, dil):
        idx = pl.ds(start, blk) if dil == 1 else pl.ds(start, blk, stride=dil)
        return ref[0, idx, :]

    def put(ref, start, dil, val):
        idx = pl.ds(start, blk) if dil == 1 else pl.ds(start, blk, stride=dil)
        ref[idx, :] = val

    def get(ref, start, dil):
        idx = pl.ds(start, blk) if dil == 1 else pl.ds(start, blk, stride=dil)
        return ref[idx, :]

    for bi, (window, dil) in enumerate(DILATED_CONFIGS):
        n_steps = window // dil
        assert n_steps <= blk
        length = seq // dil
        nb = length // blk
        nkeys = 2 * blk if nb > 1 else blk
        qi = lax.broadcasted_iota(jnp.int32, (blk, nkeys), 0)
        ki = lax.broadcasted_iota(jnp.int32, (blk, nkeys), 1)
        steps = qi + (nkeys - blk) - ki
        band = (steps >= 0) & (steps <= n_steps)
        dist = (steps * dil).astype(F32)
        bias = [jnp.where(band, -slope[h] * dist, NEG_BIG) for h in range(2)]
        prev_cols = (ki < blk).astype(F32)
        last = bi == len(DILATED_CONFIGS) - 1

        def block(j, carry, dil=dil, nb=nb, nkeys=nkeys, bias=bias, prev_cols=prev_cols, bi=bi, last=last):
            r = j // nb
            nn = j % nb
            cur = r + nn * (blk * dil)
            q = rows(q_ref, cur, dil)
            if nkeys > blk:
                prev = r + jnp.maximum(nn - 1, 0) * (blk * dil)
                kk = jnp.concatenate([rows(k_ref, prev, dil), rows(k_ref, cur, dil)], axis=0)
                vv = jnp.concatenate([rows(v_ref, prev, dil), rows(v_ref, cur, dil)], axis=0)
                pen = jnp.where(nn > 0, 0.0, NEG_BIG) * prev_cols
            else:
                kk, vv, pen = rows(k_ref, cur, dil), rows(v_ref, cur, dil), None
            kb = kk.astype(BF16)
            vb = vv.astype(BF16)
            m_h, l_h, a_h = [], [], []
            for h in range(2):
                qh = jnp.where(first if h == 0 else ~first, q, 0.0).astype(BF16)
                s = _dot_nt(qh, kb) + bias[h]
                if pen is not None:
                    s = s + pen
                m = jnp.max(s, axis=-1, keepdims=True)
                p = jnp.exp(s - m)
                m_h.append(m)
                l_h.append(jnp.sum(p, axis=-1, keepdims=True))
                a_h.append(_dot(p.astype(BF16), vb))
            m_new = jnp.where(first, m_h[0], m_h[1])
            l_new = jnp.where(first, l_h[0], l_h[1])
            a_new = jnp.where(first, a_h[0], a_h[1])
            if bi > 0:
                m_old, l_old, a_old = get(m_s, cur, dil), get(l_s, cur, dil), get(acc_s, cur, dil)
                m_tot = jnp.maximum(m_old, m_new)
                w_old = jnp.exp(m_old - m_tot)
                w_new = jnp.exp(m_new - m_tot)
                l_new = l_old * w_old + l_new * w_new
                a_new = a_old * w_old + a_new * w_new
                m_new = m_tot
            if not last:
                put(m_s, cur, dil, m_new)
                put(l_s, cur, dil, l_new)
                put(acc_s, cur, dil, a_new)
            else:
                o = a_new / l_new
                sq = o * o
                ss0 = jnp.sum(jnp.where(first, sq, 0.0), axis=-1, keepdims=True)
                ss1 = jnp.sum(jnp.where(first, 0.0, sq), axis=-1, keepdims=True)
                ms = jnp.where(first, ss0, ss1) * (1.0 / ATTN_HEAD_DIM)
                put(acc_s, cur, dil, o * lax.rsqrt(ms + NORM_EPS) * gain_ref[...])
            return carry

        lax.fori_loop(0, seq // blk, block, 0)

    o_ref[0] = acc_s[...].astype(o_ref.dtype)


def _attention(qa, ka, va, gain, slopes):
    b, seq, aw = qa.shape
    pairs = aw // LANES
    blk = pl.BlockSpec((1, seq, LANES), lambda i, p: (i, 0, p))
    return pl.pallas_call(
        _attn_kernel,
        grid=(b, pairs),
        in_specs=[pl.BlockSpec(memory_space=pltpu.SMEM), blk, blk, blk,
                  pl.BlockSpec((1, LANES), lambda i, p: (0, p))],
        out_specs=blk,
        out_shape=jax.ShapeDtypeStruct((b, seq, aw), BF16),
        scratch_shapes=[pltpu.VMEM((seq, LANES), F32)] * 3,
        compiler_params=pltpu.CompilerParams(dimension_semantics=("parallel", "parallel"),
                                             vmem_limit_bytes=VMEM_LIMIT),
        name="dilated_attention",
    )(slopes, qa, ka, va, gain.reshape(1, aw))


def _mlstm_kernel(q_ref, k_ref, v_ref, og_ref, gc_ref, gr_ref, cwq_ref, cwk_ref, cbq_ref, cbk_ref, gain_ref,
                  o_ref, qpad, kpad):
    chunk = MLSTM_CHUNK
    seq = q_ref.shape[1]
    dh = q_ref.shape[2]
    halo = 8
    qpad[0:halo, :] = jnp.zeros((halo, dh), F32)
    kpad[0:halo, :] = jnp.zeros((halo, dh), F32)
    qpad[halo:, :] = q_ref[0]
    kpad[halo:, :] = k_ref[0]

    ti = lax.broadcasted_iota(jnp.int32, (chunk, chunk), 0)
    si = lax.broadcasted_iota(jnp.int32, (chunk, chunk), 1)
    causal = si <= ti
    tri = causal.astype(BF16)
    tri_t = (ti <= si).astype(BF16)

    def log_sigmoid(x):
        return jnp.minimum(x, 0.0) - jnp.log1p(jnp.exp(-jnp.abs(x)))

    def conv_silu(pad, w_ref, b_ref, base):
        y = b_ref[...]
        for j in range(CONV_WIDTH):
            y = y + pad[pl.ds(base + halo - (CONV_WIDTH - 1) + j, chunk), :] * w_ref[j:j + 1, :]
        return y * jax.nn.sigmoid(y)

    def step(c, carry):
        cmat, nvec, m = carry
        base = pl.multiple_of(c * chunk, chunk)
        q = conv_silu(qpad, cwq_ref, cbq_ref, base)
        k = conv_silu(kpad, cwk_ref, cbk_ref, base) * (dh ** -0.5)
        v = v_ref[0, pl.ds(base, chunk), :]
        qb, kb, vb = q.astype(BF16), k.astype(BF16), v.astype(BF16)

        gcol = gc_ref[0, 0, pl.ds(base, chunk), :]
        grow = gr_ref[0, 0, :, pl.ds(base, chunk)]
        i_col = jnp.broadcast_to(gcol[:, 0:1], (chunk, dh))
        lf_col = jnp.broadcast_to(log_sigmoid(gcol[:, 1:2]), (chunk, dh))
        i_row = grow[0:1, :]
        lf_row = jnp.broadcast_to(log_sigmoid(grow[1:2, :]), (16, chunk))
        b_col = sum(_dot(tri, part) for part in _split3(lf_col))
        b_row = sum(_dot(part, tri_t) for part in _split3(lf_row))[0:1, :]
        b_last = b_col[chunk - 1:chunk, :]

        dmat = jnp.where(causal, b_col - b_row + i_row, NEG_BIG)
        g = b_col + m
        m_row = jnp.maximum(g, jnp.max(dmat, axis=-1, keepdims=True))
        dw = jnp.exp(dmat - m_row)
        gw = jnp.exp(g - m_row)
        sc = _dot_nt(qb, kb) * dw
        num = gw * _dot(qb, cmat.astype(BF16)) + _dot(sc.astype(BF16), vb)
        den = gw * jnp.sum(q * nvec, axis=-1, keepdims=True) + jnp.sum(sc, axis=-1, keepdims=True)
        hidden = num / jnp.maximum(jnp.abs(den), jnp.exp(-m_row))

        gate = jax.nn.sigmoid(og_ref[0, pl.ds(base, chunk), :])
        o_ref[0, pl.ds(base, chunk), :] = _rmsnorm(gate * hidden, gain_ref[...]).astype(o_ref.dtype)

        a_col = b_last - b_col + i_col
        m_new = jnp.maximum(b_last + m, jnp.max(a_col, axis=0, keepdims=True))
        decay = jnp.exp(b_last + m - m_new)
        wk = jnp.exp(a_col - m_new) * k
        cmat = decay * cmat + _dot(wk.T.astype(BF16), vb)
        nvec = decay * nvec + jnp.sum(wk, axis=0, keepdims=True)
        return cmat, nvec, m_new

    init = (jnp.zeros((dh, dh), F32), jnp.zeros((1, dh), F32), jnp.zeros((1, dh), F32))
    lax.fori_loop(0, seq // chunk, step, init)


def _mlstm(qm, km, vm, om, gcol, grow, conv_w, conv_b, gain):
    b, seq, mw = qm.shape
    dh = MLSTM_HEAD_DIM
    heads = mw // dh
    blk = pl.BlockSpec((1, seq, dh), lambda i, h: (i, 0, h))
    return pl.pallas_call(
        _mlstm_kernel,
        grid=(b, heads),
        in_specs=[blk, blk, blk, blk,
                  pl.BlockSpec((1, 1, seq, 2), lambda i, h: (i, h, 0, 0)),
                  pl.BlockSpec((1, 1, 2, seq), lambda i, h: (i, h, 0, 0)),
                  pl.BlockSpec((CONV_WIDTH, dh), lambda i, h: (0, h)),
                  pl.BlockSpec((CONV_WIDTH, dh), lambda i, h: (0, heads + h)),
                  pl.BlockSpec((1, dh), lambda i, h: (0, h)),
                  pl.BlockSpec((1, dh), lambda i, h: (0, heads + h)),
                  pl.BlockSpec((1, dh), lambda i, h: (0, h))],
        out_specs=blk,
        out_shape=jax.ShapeDtypeStruct((b, seq, mw), BF16),
        scratch_shapes=[pltpu.VMEM((seq + 8, dh), F32)] * 2,
        compiler_params=pltpu.CompilerParams(dimension_semantics=("parallel", "parallel"),
                                             vmem_limit_bytes=VMEM_LIMIT),
        name="mlstm",
    )(qm, km, vm, om, gcol, grow, conv_w, conv_w, conv_b.reshape(1, -1), conv_b.reshape(1, -1),
      gain.reshape(1, mw))


def kernel(x, ffn1_norm_w, ffn1_w_gate, ffn1_w_up, ffn1_w_down, mix_norm_w, w_in, q_norm_w, k_norm_w, conv_w,
           conv_b, i_bias, f_bias, attn_out_gain, mlstm_out_gain, w_out, ffn2_norm_w, ffn2_w_gate, ffn2_w_up,
           ffn2_w_down):
    b, seq, d = x.shape
    n = b * seq
    aw = ATTN_HEADS * ATTN_HEAD_DIM
    mw = MLSTM_HEADS * MLSTM_HEAD_DIM
    n_main = 3 * aw + 4 * mw
    hm = MLSTM_HEADS
    slopes = jnp.asarray(2.0 ** (-8.0 * np.arange(1, ATTN_HEADS + 1, dtype=np.float32) / ATTN_HEADS), F32)
    xt = x.reshape(n, d)
    for l in range(ffn1_norm_w.shape[0]):
        xt = _ffn(xt, ffn1_norm_w[l].reshape(1, d), ffn1_w_gate[l].astype(BF16), ffn1_w_up[l].astype(BF16),
                  ffn1_w_down[l].astype(BF16))
        w_l = w_in[l].astype(BF16)
        gate_bias = jnp.concatenate([i_bias[l], f_bias[l]]).astype(F32)
        qa, ka, va, qm, km, vm, om, gc, gr = _inproj(xt, mix_norm_w[l], w_l[:, :n_main], w_l[:, n_main:],
                                                     gate_bias, q_norm_w[l], k_norm_w[l])
        shp = lambda t: t.reshape(b, seq, -1)
        attn = _attention(shp(qa), shp(ka), shp(va), attn_out_gain[l], slopes)
        gcol = gc.reshape(b, seq, 2, hm).transpose(0, 3, 1, 2)
        grow = gr.reshape(2, hm, b, seq).transpose(2, 1, 0, 3)
        mls = _mlstm(shp(qm), shp(km), shp(vm), shp(om), gcol, grow, conv_w[l], conv_b[l], mlstm_out_gain[l])
        w_o = w_out[l].astype(BF16)
        xt = _ffn(xt, ffn2_norm_w[l].reshape(1, d), ffn2_w_gate[l].astype(BF16), ffn2_w_up[l].astype(BF16),
                  ffn2_w_down[l].astype(BF16),
                  mix=(attn.reshape(n, aw), mls.reshape(n, mw), w_o[:aw], w_o[aw:]))
    return xt.reshape(b, seq, d)
```

```python
import functools

import numpy as np
import jax
import jax.numpy as jnp
from jax import lax
from jax.experimental import pallas as pl
from jax.experimental.pallas import tpu as pltpu

F32 = jnp.float32
BF16 = jnp.bfloat16

LANES = 128
NORM_EPS = 1e-6
ATTN_HEADS = 8
ATTN_HEAD_DIM = 64
MLSTM_HEADS = 4
MLSTM_HEAD_DIM = 128
CONV_WIDTH = 4
DILATED_CONFIGS = ((128, 1), (512, 4), (2048, 16))
ATTN_BLOCK = 128
MLSTM_CHUNK = 128
NEG_BIG = -1e30
ATTN_GROUP = 4
LOG2E = 1.4426950408889634
FFN_TOKENS = 512
VMEM_LIMIT = 56 * 1024 * 1024


def _const_spec(shape):
    nd = len(shape)
    return pl.BlockSpec(shape, lambda *_: (0,) * nd, pipeline_mode=pl.Buffered(1))


def _rmsnorm(x, w):
    ms = jnp.mean(x * x, axis=-1, keepdims=True)
    return x * lax.rsqrt(ms + NORM_EPS) * w


def _dot(a, b):
    return jnp.dot(a, b, preferred_element_type=F32)


def _dot_nt(a, b):
    return lax.dot_general(a, b, (((1,), (1,)), ((), ())), preferred_element_type=F32)


def _split3(x):
    h1 = x.astype(BF16)
    r1 = x - h1.astype(F32)
    h2 = r1.astype(BF16)
    h3 = (r1 - h2.astype(F32)).astype(BF16)
    return h1, h2, h3


def _swiglu_half(x, nw_ref, wg_ref, wu_ref, wd_ref):
    h = _rmsnorm(x, nw_ref[...]).astype(BF16)
    g = _dot(h, wg_ref[...])
    u = _dot(h, wu_ref[...])
    a = (g * jax.nn.sigmoid(g) * u).astype(BF16)
    return x + 0.5 * _dot(a, wd_ref[...])


def _ffn_kernel(x_ref, nw_ref, wg_ref, wu_ref, wd_ref, o_ref):
    o_ref[...] = _swiglu_half(x_ref[...], nw_ref, wg_ref, wu_ref, wd_ref)


def _outproj_ffn_kernel(x_ref, ya_ref, ym_ref, woa_ref, wom_ref, nw_ref, wg_ref, wu_ref, wd_ref, o_ref):
    x = x_ref[...] + _dot(ya_ref[...], woa_ref[...]) + _dot(ym_ref[...], wom_ref[...])
    o_ref[...] = _swiglu_half(x, nw_ref, wg_ref, wu_ref, wd_ref)


def _ffn(x, nw, wg, wu, wd, mix=None):
    n, d = x.shape
    f = wg.shape[1]
    tm = FFN_TOKENS
    tok = lambda w: pl.BlockSpec((tm, w), lambda i: (i, 0))
    w_specs = [_const_spec((1, d)), _const_spec((d, f)), _const_spec((d, f)), _const_spec((f, d))]
    if mix is None:
        body, ins, specs = _ffn_kernel, (x,), [tok(d)]
    else:
        ya, ym, woa, wom = mix
        body, ins = _outproj_ffn_kernel, (x, ya, ym, woa, wom)
        specs = [tok(d), tok(ya.shape[1]), tok(ym.shape[1]), _const_spec(woa.shape), _const_spec(wom.shape)]
    return pl.pallas_call(
        body,
        grid=(n // tm,),
        in_specs=specs + w_specs,
        out_specs=tok(d),
        out_shape=jax.ShapeDtypeStruct((n, d), F32),
        compiler_params=pltpu.CompilerParams(dimension_semantics=("parallel",), vmem_limit_bytes=VMEM_LIMIT),
        name="ffn" if mix is None else "outproj_ffn",
    )(*ins, nw, wg, wu, wd)


def _inproj_kernel(x_ref, nw_ref, w_ref, wgc_ref, wgr_ref, gbc_ref, gbr_ref, qnw_ref, knw_ref, bd_ref,
                   qa_ref, ka_ref, va_ref, qm_ref, km_ref, vm_ref, om_ref, gc_ref, gr_ref):
    h = _rmsnorm(x_ref[...], nw_ref[...]).astype(BF16)
    aw = qa_ref.shape[1]
    mw = qm_ref.shape[1]

    def head_norm(p, w):
        s1, s2, s3 = _split3(p * p)
        ss = _dot(s1, bd_ref[...]) + _dot(s2, bd_ref[...]) + _dot(s3, bd_ref[...])
        return p * lax.rsqrt(ss * (1.0 / ATTN_HEAD_DIM) + NORM_EPS) * w

    qa_ref[...] = head_norm(_dot(h, w_ref[:, 0:aw]), qnw_ref[...]) * (ATTN_HEAD_DIM ** -0.5 * LOG2E)
    ka_ref[...] = head_norm(_dot(h, w_ref[:, aw:2 * aw]), knw_ref[...])
    va_ref[...] = _dot(h, w_ref[:, 2 * aw:3 * aw])
    base = 3 * aw
    for k, ref in enumerate((qm_ref, km_ref, vm_ref, om_ref)):
        ref[...] = _dot(h, w_ref[:, base + k * mw: base + (k + 1) * mw])
    ng = gc_ref.shape[1]
    gc_ref[...] = _dot(h, wgc_ref[...])[:, :ng] + gbc_ref[...]
    gr_ref[...] = _dot_nt(wgr_ref[...], h)[:ng, :] + gbr_ref[...]


def _inproj(x, nw, w_main, w_gate, gate_bias, qnw, knw):
    n, d = x.shape
    tm = FFN_TOKENS
    aw = ATTN_HEADS * ATTN_HEAD_DIM
    mw = MLSTM_HEADS * MLSTM_HEAD_DIM
    ng = 2 * MLSTM_HEADS
    wgc = jnp.zeros((d, LANES), BF16).at[:, :ng].set(w_gate)
    wgr = jnp.zeros((16, d), BF16).at[:ng, :].set(w_gate.T)
    gbc = gate_bias.reshape(1, ng)
    gbr = gate_bias.reshape(ng, 1)
    heads = np.arange(aw) // ATTN_HEAD_DIM
    bd = jnp.asarray(heads[:, None] == heads[None, :], BF16)
    qnw_t = jnp.tile(qnw, ATTN_HEADS).reshape(1, aw)
    knw_t = jnp.tile(knw, ATTN_HEADS).reshape(1, aw)
    tok = lambda w: pl.BlockSpec((tm, w), lambda i: (i, 0))
    out_shape = ([jax.ShapeDtypeStruct((n, aw), F32)] * 3 + [jax.ShapeDtypeStruct((n, mw), F32)] * 4
                 + [jax.ShapeDtypeStruct((n, ng), F32), jax.ShapeDtypeStruct((ng, n), F32)])
    out_specs = [tok(aw)] * 3 + [tok(mw)] * 4 + [tok(ng), pl.BlockSpec((ng, tm), lambda i: (0, i))]
    return pl.pallas_call(
        _inproj_kernel,
        grid=(n // tm,),
        in_specs=[tok(d), _const_spec((1, d)), _const_spec(w_main.shape), _const_spec(wgc.shape),
                  _const_spec(wgr.shape), _const_spec(gbc.shape), _const_spec(gbr.shape),
                  _const_spec((1, aw)), _const_spec((1, aw)), _const_spec(bd.shape)],
        out_specs=out_specs,
        out_shape=out_shape,
        compiler_params=pltpu.CompilerParams(dimension_semantics=("parallel",), vmem_limit_bytes=VMEM_LIMIT),
        name="in_proj",
    )(x, nw.reshape(1, d), w_main, wgc, wgr, gbc, gbr, qnw_t, knw_t, bd)


def _attn_kernel(slopes_ref, q_ref, k_ref, v_ref, gain_ref, o_ref, qs, kk, vv, x4f, part):
    blk = ATTN_BLOCK
    seq = q_ref.shape[1]
    nblk = seq // blk
    grp = ATTN_GROUP
    assert [d for _, d in DILATED_CONFIGS] == [1, grp, grp * grp] and nblk == grp * grp
    pair = pl.program_id(1)
    lane = lax.broadcasted_iota(jnp.int32, (blk, LANES), 1)
    first = lane < ATTN_HEAD_DIM
    slope = (slopes_ref[2 * pair], slopes_ref[2 * pair + 1])
    refs = (q_ref, k_ref, v_ref)

    def stage(order, j, q, k, v):
        qs[order, 2 * j * blk:(2 * j + 1) * blk, :] = jnp.where(first, q, 0.0).astype(BF16)
        qs[order, (2 * j + 1) * blk:(2 * j + 2) * blk, :] = jnp.where(first, 0.0, q).astype(BF16)
        kk[order, j * blk:(j + 1) * blk, :] = k.astype(BF16)
        vv[order, j * blk:(j + 1) * blk, :] = v.astype(BF16)

    for j in range(nblk):
        stage(0, j, *(ref[0, j * blk:(j + 1) * blk, :] for ref in refs))
    for j in range(nblk):
        r, c = divmod(j, grp)
        x = [ref[0, pl.ds(c * blk * grp + r, blk, stride=grp), :] for ref in refs]
        for i in range(3):
            x4f[i, j * blk:(j + 1) * blk, :] = x[i]
        stage(1, j, *x)
    for j in range(nblk):
        start = (j % grp) * (seq // grp) + j // grp
        stage(2, j, *(x4f[i, pl.ds(start, blk, stride=grp), :] for i in range(3)))

    def make_bias(dil, n_steps):
        qi = lax.broadcasted_iota(jnp.int32, (blk, 2 * blk), 0)
        ki = lax.broadcasted_iota(jnp.int32, (blk, 2 * blk), 1)
        steps = qi + blk - ki
        band = (steps >= 0) & (steps <= n_steps)
        dist = (steps * dil).astype(F32) * LOG2E
        return jnp.concatenate([jnp.where(band, -slope[h] * dist, NEG_BIG) for h in range(2)], axis=0)

    ones = jnp.ones((2 * blk, LANES), BF16)

    def partial(order, off, kcat, vcat, bias):
        nk = kcat.shape[0]
        s = _dot_nt(qs[order, pl.ds(2 * off, 2 * blk), :], kcat) + bias
        m = jnp.max(s, axis=-1, keepdims=True)
        p = jnp.exp2(s - m).astype(BF16)
        r = _dot(p, jnp.concatenate([vcat, ones[:nk]], axis=1))
        return (jnp.where(first, m[:blk], m[blk:]),
                jnp.where(first, r[:blk, LANES:], r[blk:, LANES:]),
                jnp.where(first, r[:blk, :LANES], r[blk:, :LANES]))

    def merge(a, b):
        m = jnp.maximum(a[0], b[0])
        wa = jnp.exp2(a[0] - m)
        wb = jnp.exp2(b[0] - m)
        return m, a[1] * wa + b[1] * wb, a[2] * wa + b[2] * wb

    def load_part(stage_i, idx):
        return tuple(part[stage_i, i, idx, :] for i in range(3))

    def store_part(stage_i, idx, val):
        for i in range(3):
            part[stage_i, i, idx, :] = val[i]

    (w1, d1), (w4, d4), (w16, d16) = DILATED_CONFIGS
    bias16 = make_bias(d16, w16 // d16)[:, blk:]
    bias4 = make_bias(d4, w4 // d4)
    bias1 = make_bias(d1, w1 // d1)
    prev_cols = (lax.broadcasted_iota(jnp.int32, (2 * blk, 2 * blk), 1) < blk).astype(F32)

    def group16(g, carry):
        for c in range(grp):
            off = pl.multiple_of((g * grp + c) * blk, blk)
            cur = pl.ds(off, blk)
            res = partial(2, off, kk[2, cur, :], vv[2, cur, :], bias16)
            store_part(0, pl.ds(c * (seq // grp) + g, blk, stride=grp), res)
        return carry

    lax.fori_loop(0, grp, group16, 0)

    def group4(g, carry):
        k_prev = v_prev = None
        for c in range(grp):
            off = pl.multiple_of((g * grp + c) * blk, blk)
            cur = pl.ds(off, blk)
            k_cur, v_cur = kk[1, cur, :], vv[1, cur, :]
            if c == 0:
                res = partial(1, off, k_cur, v_cur, bias4[:, blk:])
            else:
                res = partial(1, off, jnp.concatenate([k_prev, k_cur], axis=0),
                              jnp.concatenate([v_prev, v_cur], axis=0), bias4)
            k_prev, v_prev = k_cur, v_cur
            res = merge(res, load_part(0, cur))
            store_part(1, pl.ds(c * blk * grp + g, blk, stride=grp), res)
        return carry

    lax.fori_loop(0, grp, group4, 0)

    def group1(g, carry):
        k_prev = v_prev = None
        for c in range(grp):
            off = pl.multiple_of((g * grp + c) * blk, blk)
            cur = pl.ds(off, blk)
            k_cur, v_cur = kk[0, cur, :], vv[0, cur, :]
            bias = bias1
            if c == 0:
                prev = pl.ds(pl.multiple_of(jnp.maximum(g * grp - 1, 0) * blk, blk), blk)
                k_prev, v_prev = kk[0, prev, :], vv[0, prev, :]
                bias = bias + jnp.where(g > 0, 0.0, NEG_BIG) * prev_cols
            res = partial(0, off, jnp.concatenate([k_prev, k_cur], axis=0),
                          jnp.concatenate([v_prev, v_cur], axis=0), bias)
            k_prev, v_prev = k_cur, v_cur
            _, l_tot, a_tot = merge(res, load_part(1, cur))
            o = a_tot / l_tot
            sq = o * o
            ss0 = jnp.sum(jnp.where(first, sq, 0.0), axis=-1, keepdims=True)
            ss1 = jnp.sum(jnp.where(first, 0.0, sq), axis=-1, keepdims=True)
            ms = jnp.where(first, ss0, ss1) * (1.0 / ATTN_HEAD_DIM)
            o_ref[0, cur, :] = (o * lax.rsqrt(ms + NORM_EPS) * gain_ref[...]).astype(o_ref.dtype)
        return carry

    lax.fori_loop(0, grp, group1, 0)


def _attention(qa, ka, va, gain, slopes):
    b, seq, aw = qa.shape
    pairs = aw // LANES
    nbr = len(DILATED_CONFIGS)
    blk = pl.BlockSpec((1, seq, LANES), lambda i, p: (i, 0, p))
    return pl.pallas_call(
        _attn_kernel,
        grid=(b, pairs),
        in_specs=[pl.BlockSpec(memory_space=pltpu.SMEM), blk, blk, blk,
                  pl.BlockSpec((1, LANES), lambda i, p: (0, p))],
        out_specs=blk,
        out_shape=jax.ShapeDtypeStruct((b, seq, aw), BF16),
        scratch_shapes=[pltpu.VMEM((nbr, 2 * seq, LANES), BF16),
                        pltpu.VMEM((nbr, seq, LANES), BF16),
                        pltpu.VMEM((nbr, seq, LANES), BF16),
                        pltpu.VMEM((3, seq, LANES), F32),
                        pltpu.VMEM((2, 3, seq, LANES), F32)],
        compiler_params=pltpu.CompilerParams(dimension_semantics=("parallel", "parallel"),
                                             vmem_limit_bytes=VMEM_LIMIT),
        name="dilated_attention",
    )(slopes, qa, ka, va, gain.reshape(1, aw))


def _mlstm_kernel(q_ref, k_ref, v_ref, og_ref, gc_ref, gr_ref, cwq_ref, cwk_ref, cbq_ref, cbk_ref, gain_ref,
                  o_ref, qpad, kpad):
    chunk = MLSTM_CHUNK
    seq = q_ref.shape[1]
    dh = q_ref.shape[2]
    halo = 8
    qpad[0:halo, :] = jnp.zeros((halo, dh), F32)
    kpad[0:halo, :] = jnp.zeros((halo, dh), F32)
    qpad[halo:, :] = q_ref[0]
    kpad[halo:, :] = k_ref[0]

    ti = lax.broadcasted_iota(jnp.int32, (chunk, chunk), 0)
    si = lax.broadcasted_iota(jnp.int32, (chunk, chunk), 1)
    causal = si <= ti
    tri = causal.astype(BF16)
    tri_t = (ti <= si).astype(BF16)

    def log_sigmoid(x):
        return jnp.minimum(x, 0.0) - jnp.log1p(jnp.exp(-jnp.abs(x)))

    def conv_silu(pad, w_ref, b_ref, base):
        y = b_ref[...]
        for j in range(CONV_WIDTH):
            y = y + pad[pl.ds(base + halo - (CONV_WIDTH - 1) + j, chunk), :] * w_ref[j:j + 1, :]
        return y * jax.nn.sigmoid(y)

    def step(c, carry):
        cmat, nvec, m = carry
        base = pl.multiple_of(c * chunk, chunk)
        q = conv_silu(qpad, cwq_ref, cbq_ref, base)
        k = conv_silu(kpad, cwk_ref, cbk_ref, base) * (dh ** -0.5)
        v = v_ref[0, pl.ds(base, chunk), :]
        qb, kb, vb = q.astype(BF16), k.astype(BF16), v.astype(BF16)

        gcol = gc_ref[0, 0, pl.ds(base, chunk), :]
        grow = gr_ref[0, 0, :, pl.ds(base, chunk)]
        i_col = jnp.broadcast_to(gcol[:, 0:1], (chunk, dh))
        lf_col = jnp.broadcast_to(log_sigmoid(gcol[:, 1:2]), (chunk, dh))
        i_row = grow[0:1, :]
        lf_row = jnp.broadcast_to(log_sigmoid(grow[1:2, :]), (16, chunk))
        b_col = sum(_dot(tri, part) for part in _split3(lf_col))
        b_row = sum(_dot(part, tri_t) for part in _split3(lf_row))[0:1, :]
        b_last = b_col[chunk - 1:chunk, :]

        dmat = jnp.where(causal, b_col - b_row + i_row, NEG_BIG)
        g = b_col + m
        m_row = jnp.maximum(g, jnp.max(dmat, axis=-1, keepdims=True))
        dw = jnp.exp(dmat - m_row)
        gw = jnp.exp(g - m_row)
        sc = _dot_nt(qb, kb) * dw
        num = gw * _dot(qb, cmat.astype(BF16)) + _dot(sc.astype(BF16), vb)
        den = gw * jnp.sum(q * nvec, axis=-1, keepdims=True) + jnp.sum(sc, axis=-1, keepdims=True)
        hidden = num / jnp.maximum(jnp.abs(den), jnp.exp(-m_row))

        gate = jax.nn.sigmoid(og_ref[0, pl.ds(base, chunk), :])
        o_ref[0, pl.ds(base, chunk), :] = _rmsnorm(gate * hidden, gain_ref[...]).astype(o_ref.dtype)

        a_col = b_last - b_col + i_col
        m_new = jnp.maximum(b_last + m, jnp.max(a_col, axis=0, keepdims=True))
        decay = jnp.exp(b_last + m - m_new)
        wk = jnp.exp(a_col - m_new) * k
        cmat = decay * cmat + _dot(wk.T.astype(BF16), vb)
        nvec = decay * nvec + jnp.sum(wk, axis=0, keepdims=True)
        return cmat, nvec, m_new

    init = (jnp.zeros((dh, dh), F32), jnp.zeros((1, dh), F32), jnp.zeros((1, dh), F32))
    lax.fori_loop(0, seq // chunk, step, init)


def _mlstm(qm, km, vm, om, gcol, grow, conv_w, conv_b, gain):
    b, seq, mw = qm.shape
    dh = MLSTM_HEAD_DIM
    heads = mw // dh
    blk = pl.BlockSpec((1, seq, dh), lambda i, h: (i, 0, h))
    return pl.pallas_call(
        _mlstm_kernel,
        grid=(b, heads),
        in_specs=[blk, blk, blk, blk,
                  pl.BlockSpec((1, 1, seq, 2), lambda i, h: (i, h, 0, 0)),
                  pl.BlockSpec((1, 1, 2, seq), lambda i, h: (i, h, 0, 0)),
                  pl.BlockSpec((CONV_WIDTH, dh), lambda i, h: (0, h)),
                  pl.BlockSpec((CONV_WIDTH, dh), lambda i, h: (0, heads + h)),
                  pl.BlockSpec((1, dh), lambda i, h: (0, h)),
                  pl.BlockSpec((1, dh), lambda i, h: (0, heads + h)),
                  pl.BlockSpec((1, dh), lambda i, h: (0, h))],
        out_specs=blk,
        out_shape=jax.ShapeDtypeStruct((b, seq, mw), BF16),
        scratch_shapes=[pltpu.VMEM((seq + 8, dh), F32)] * 2,
        compiler_params=pltpu.CompilerParams(dimension_semantics=("parallel", "parallel"),
                                             vmem_limit_bytes=VMEM_LIMIT),
        name="mlstm",
    )(qm, km, vm, om, gcol, grow, conv_w, conv_w, conv_b.reshape(1, -1), conv_b.reshape(1, -1),
      gain.reshape(1, mw))


def kernel(x, ffn1_norm_w, ffn1_w_gate, ffn1_w_up, ffn1_w_down, mix_norm_w, w_in, q_norm_w, k_norm_w, conv_w,
           conv_b, i_bias, f_bias, attn_out_gain, mlstm_out_gain, w_out, ffn2_norm_w, ffn2_w_gate, ffn2_w_up,
           ffn2_w_down):
    b, seq, d = x.shape
    n = b * seq
    aw = ATTN_HEADS * ATTN_HEAD_DIM
    mw = MLSTM_HEADS * MLSTM_HEAD_DIM
    n_main = 3 * aw + 4 * mw
    hm = MLSTM_HEADS
    slopes = jnp.asarray(2.0 ** (-8.0 * np.arange(1, ATTN_HEADS + 1, dtype=np.float32) / ATTN_HEADS), F32)
    xt = x.reshape(n, d)
    for l in range(ffn1_norm_w.shape[0]):
        xt = _ffn(xt, ffn1_norm_w[l].reshape(1, d), ffn1_w_gate[l].astype(BF16), ffn1_w_up[l].astype(BF16),
                  ffn1_w_down[l].astype(BF16))
        w_l = w_in[l].astype(BF16)
        gate_bias = jnp.concatenate([i_bias[l], f_bias[l]]).astype(F32)
        qa, ka, va, qm, km, vm, om, gc, gr = _inproj(xt, mix_norm_w[l], w_l[:, :n_main], w_l[:, n_main:],
                                                     gate_bias, q_norm_w[l], k_norm_w[l])
        shp = lambda t: t.reshape(b, seq, -1)
        attn = _attention(shp(qa), shp(ka), shp(va), attn_out_gain[l], slopes)
        gcol = gc.reshape(b, seq, 2, hm).transpose(0, 3, 1, 2)
        grow = gr.reshape(2, hm, b, seq).transpose(2, 1, 0, 3)
        mls = _mlstm(shp(qm), shp(km), shp(vm), shp(om), gcol, grow, conv_w[l], conv_b[l], mlstm_out_gain[l])
        w_o = w_out[l].astype(BF16)
        xt = _ffn(xt, ffn2_norm_w[l].reshape(1, d), ffn2_w_gate[l].astype(BF16), ffn2_w_up[l].astype(BF16),
                  ffn2_w_down[l].astype(BF16),
                  mix=(attn.reshape(n, aw), mls.reshape(n, mw), w_o[:aw], w_o[aw:]))
    return xt.reshape(b, seq, d)
```

```python
import functools

import numpy as np
import jax
import jax.numpy as jnp
from jax import lax
from jax.experimental import pallas as pl
from jax.experimental.pallas import tpu as pltpu

F32 = jnp.float32
BF16 = jnp.bfloat16

LANES = 128
SUBLANES = 8
BF16_ROWS = 16
NORM_EPS = 1e-6
ATTN_HEADS = 8
ATTN_HEAD_DIM = 64
MLSTM_HEADS = 4
MLSTM_HEAD_DIM = 128
CONV_WIDTH = 4
DILATED_CONFIGS = ((128, 1), (512, 4), (2048, 16))
ATTN_BLOCK = 128
MLSTM_CHUNK = 128
NEG_BIG = -1e30
ATTN_GROUP = 4
LOG2E = 1.4426950408889634
FFN_TOKENS = 512
VMEM_LIMIT = 56 * 1024 * 1024


def _const_spec(shape):
    nd = len(shape)
    return pl.BlockSpec(shape, lambda *_: (0,) * nd, pipeline_mode=pl.Buffered(1))


def _rmsnorm(x, w):
    ms = jnp.mean(x * x, axis=-1, keepdims=True)
    return x * lax.rsqrt(ms + NORM_EPS) * w


def _dot(a, b):
    return jnp.dot(a, b, preferred_element_type=F32)


def _dot_nt(a, b):
    return lax.dot_general(a, b, (((1,), (1,)), ((), ())), preferred_element_type=F32)


def _split3(x):
    h1 = x.astype(BF16)
    r1 = x - h1.astype(F32)
    h2 = r1.astype(BF16)
    h3 = (r1 - h2.astype(F32)).astype(BF16)
    return h1, h2, h3


def _swiglu_half(x, nw_ref, wg_ref, wu_ref, wd_ref):
    h = _rmsnorm(x, nw_ref[...]).astype(BF16)
    g = _dot(h, wg_ref[...])
    u = _dot(h, wu_ref[...])
    a = (g * jax.nn.sigmoid(g) * u).astype(BF16)
    return x + 0.5 * _dot(a, wd_ref[...])


def _ffn_kernel(x_ref, nw_ref, wg_ref, wu_ref, wd_ref, o_ref):
    o_ref[...] = _swiglu_half(x_ref[...], nw_ref, wg_ref, wu_ref, wd_ref)


def _outproj_ffn_kernel(x_ref, ya_ref, ym_ref, woa_ref, wom_ref, nw_ref, wg_ref, wu_ref, wd_ref, o_ref):
    x = x_ref[...] + _dot(ya_ref[...], woa_ref[...]) + _dot(ym_ref[...], wom_ref[...])
    o_ref[...] = _swiglu_half(x, nw_ref, wg_ref, wu_ref, wd_ref)


def _ffn(x, nw, wg, wu, wd, mix=None):
    n, d = x.shape
    f = wg.shape[1]
    tm = FFN_TOKENS
    tok = lambda w: pl.BlockSpec((tm, w), lambda i: (i, 0))
    w_specs = [_const_spec((1, d)), _const_spec((d, f)), _const_spec((d, f)), _const_spec((f, d))]
    if mix is None:
        body, ins, specs = _ffn_kernel, (x,), [tok(d)]
    else:
        ya, ym, woa, wom = mix
        body, ins = _outproj_ffn_kernel, (x, ya, ym, woa, wom)
        specs = [tok(d), tok(ya.shape[1]), tok(ym.shape[1]), _const_spec(woa.shape), _const_spec(wom.shape)]
    return pl.pallas_call(
        body,
        grid=(n // tm,),
        in_specs=specs + w_specs,
        out_specs=tok(d),
        out_shape=jax.ShapeDtypeStruct((n, d), F32),
        compiler_params=pltpu.CompilerParams(dimension_semantics=("parallel",), vmem_limit_bytes=VMEM_LIMIT),
        name="ffn" if mix is None else "outproj_ffn",
    )(*ins, nw, wg, wu, wd)


def _inproj_kernel(tiles_per_seq, x_ref, nw_ref, w_ref, wgr_ref, gbr_ref, qnw_ref, knw_ref, cw_ref, cb_ref,
                   qa_ref, ka_ref, va_ref, qm_ref, km_ref, vm_ref, om_ref, gr_ref, cpad):
    tm = x_ref.shape[0]
    aw = qa_ref.shape[1]
    mw = qm_ref.shape[1]
    h = _rmsnorm(x_ref[...], nw_ref[...]).astype(BF16)
    first = lax.broadcasted_iota(jnp.int32, (1, LANES), 1) < ATTN_HEAD_DIM

    def head_norm(dst, p, w_row, scale):
        for c in range(0, aw, LANES):
            ps = p[:, c:c + LANES]
            sq = ps * ps
            ss0 = jnp.sum(jnp.where(first, sq, 0.0), axis=-1, keepdims=True)
            ss1 = jnp.sum(jnp.where(first, 0.0, sq), axis=-1, keepdims=True)
            ms = jnp.where(first, ss0, ss1) * (1.0 / ATTN_HEAD_DIM)
            dst[:, c:c + LANES] = ps * lax.rsqrt(ms + NORM_EPS) * (w_row[:, c:c + LANES] * scale)

    head_norm(qa_ref, _dot(h, w_ref[:, 0:aw]), qnw_ref[...], ATTN_HEAD_DIM ** -0.5 * LOG2E)
    head_norm(ka_ref, _dot(h, w_ref[:, aw:2 * aw]), knw_ref[...], 1.0)
    va_ref[...] = _dot(h, w_ref[:, 2 * aw:3 * aw])

    base = 3 * aw
    hist = CONV_WIDTH - 1

    @pl.when(pl.program_id(0) % tiles_per_seq == 0)
    def _():
        cpad[0:SUBLANES, :] = jnp.zeros((SUBLANES, 2 * mw), F32)

    cpad[SUBLANES:, :] = _dot(h, w_ref[:, base:base + 2 * mw])
    y = cb_ref[...]
    for j in range(CONV_WIDTH):
        y = y + cpad[SUBLANES - hist + j:SUBLANES - hist + j + tm, :] * cw_ref[j:j + 1, :]
    y = y * jax.nn.sigmoid(y)
    qm_ref[...] = y[:, :mw].astype(BF16)
    km_ref[...] = (y[:, mw:] * (MLSTM_HEAD_DIM ** -0.5)).astype(BF16)
    cpad[0:SUBLANES, :] = cpad[tm:tm + SUBLANES, :]

    vm_ref[...] = _dot(h, w_ref[:, base + 2 * mw:base + 3 * mw]).astype(BF16)
    om_ref[...] = jax.nn.sigmoid(_dot(h, w_ref[:, base + 3 * mw:base + 4 * mw])).astype(BF16)
    ng = gr_ref.shape[0]
    gr_ref[...] = _dot_nt(wgr_ref[...], h)[:ng, :] + gbr_ref[...]


def _inproj(x, seq, nw, w_main, w_gate, gate_bias, qnw, knw, conv_w, conv_b):
    n, d = x.shape
    tm = FFN_TOKENS
    aw = ATTN_HEADS * ATTN_HEAD_DIM
    mw = MLSTM_HEADS * MLSTM_HEAD_DIM
    ng = 2 * MLSTM_HEADS
    wgr = jnp.zeros((BF16_ROWS, d), BF16).at[:ng, :].set(w_gate.T)
    gbr = gate_bias.reshape(ng, 1)
    qnw_t = jnp.tile(qnw, ATTN_HEADS).reshape(1, aw)
    knw_t = jnp.tile(knw, ATTN_HEADS).reshape(1, aw)
    tok = lambda w: pl.BlockSpec((tm, w), lambda i: (i, 0))
    out_shape = ([jax.ShapeDtypeStruct((n, aw), F32)] * 3 + [jax.ShapeDtypeStruct((n, mw), BF16)] * 4
                 + [jax.ShapeDtypeStruct((ng, n), F32)])
    out_specs = [tok(aw)] * 3 + [tok(mw)] * 4 + [pl.BlockSpec((ng, tm), lambda i: (0, i))]
    return pl.pallas_call(
        functools.partial(_inproj_kernel, seq // tm),
        grid=(n // tm,),
        in_specs=[tok(d), _const_spec((1, d)), _const_spec(w_main.shape), _const_spec(wgr.shape),
                  _const_spec(gbr.shape), _const_spec((1, aw)), _const_spec((1, aw)),
                  _const_spec(conv_w.shape), _const_spec((1, 2 * mw))],
        out_specs=out_specs,
        out_shape=out_shape,
        scratch_shapes=[pltpu.VMEM((tm + SUBLANES, 2 * mw), F32)],
        compiler_params=pltpu.CompilerParams(dimension_semantics=("arbitrary",), vmem_limit_bytes=VMEM_LIMIT),
        name="in_proj",
    )(x, nw.reshape(1, d), w_main, wgr, gbr, qnw_t, knw_t, conv_w, conv_b.reshape(1, 2 * mw))


def _attn_kernel(slopes_ref, q_ref, k_ref, v_ref, gain_ref, o_ref, qs, kk, vv, x4f, part):
    blk = ATTN_BLOCK
    seq = q_ref.shape[1]
    nblk = seq // blk
    grp = ATTN_GROUP
    assert [d for _, d in DILATED_CONFIGS] == [1, grp, grp * grp] and nblk == grp * grp
    pair = pl.program_id(1)
    lane = lax.broadcasted_iota(jnp.int32, (blk, LANES), 1)
    first = lane < ATTN_HEAD_DIM
    slope = (slopes_ref[2 * pair], slopes_ref[2 * pair + 1])
    refs = (q_ref, k_ref, v_ref)

    def stage(order, j, q, k, v):
        qs[order, 2 * j * blk:(2 * j + 1) * blk, :] = jnp.where(first, q, 0.0).astype(BF16)
        qs[order, (2 * j + 1) * blk:(2 * j + 2) * blk, :] = jnp.where(first, 0.0, q).astype(BF16)
        kk[order, j * blk:(j + 1) * blk, :] = k.astype(BF16)
        vv[order, j * blk:(j + 1) * blk, :] = v.astype(BF16)

    for j in range(nblk):
        stage(0, j, *(ref[0, j * blk:(j + 1) * blk, :] for ref in refs))
    for j in range(nblk):
        r, c = divmod(j, grp)
        x = [ref[0, pl.ds(c * blk * grp + r, blk, stride=grp), :] for ref in refs]
        for i in range(3):
            x4f[i, j * blk:(j + 1) * blk, :] = x[i]
        stage(1, j, *x)
    for j in range(nblk):
        start = (j % grp) * (seq // grp) + j // grp
        stage(2, j, *(x4f[i, pl.ds(start, blk, stride=grp), :] for i in range(3)))

    def make_bias(dil, n_steps):
        qi = lax.broadcasted_iota(jnp.int32, (blk, 2 * blk), 0)
        ki = lax.broadcasted_iota(jnp.int32, (blk, 2 * blk), 1)
        steps = qi + blk - ki
        band = (steps >= 0) & (steps <= n_steps)
        dist = (steps * dil).astype(F32) * LOG2E
        return jnp.concatenate([jnp.where(band, -slope[h] * dist, NEG_BIG) for h in range(2)], axis=0)

    ones = jnp.ones((2 * blk, LANES), BF16)

    def partial(order, off, kcat, vcat, bias):
        nk = kcat.shape[0]
        s = _dot_nt(qs[order, pl.ds(2 * off, 2 * blk), :], kcat) + bias
        m = jnp.max(s, axis=-1, keepdims=True)
        p = jnp.exp2(s - m).astype(BF16)
        r = _dot(p, jnp.concatenate([vcat, ones[:nk]], axis=1))
        return (jnp.where(first, m[:blk], m[blk:]),
                jnp.where(first, r[:blk, LANES:], r[blk:, LANES:]),
                jnp.where(first, r[:blk, :LANES], r[blk:, :LANES]))

    def merge(a, b):
        m = jnp.maximum(a[0], b[0])
        wa = jnp.exp2(a[0] - m)
        wb = jnp.exp2(b[0] - m)
        return m, a[1] * wa + b[1] * wb, a[2] * wa + b[2] * wb

    def load_part(stage_i, idx):
        return tuple(part[stage_i, i, idx, :] for i in range(3))

    def store_part(stage_i, idx, val):
        for i in range(3):
            part[stage_i, i, idx, :] = val[i]

    (w1, d1), (w4, d4), (w16, d16) = DILATED_CONFIGS
    bias16 = make_bias(d16, w16 // d16)[:, blk:]
    bias4 = make_bias(d4, w4 // d4)
    bias1 = make_bias(d1, w1 // d1)
    prev_cols = (lax.broadcasted_iota(jnp.int32, (2 * blk, 2 * blk), 1) < blk).astype(F32)

    def group16(g, carry):
        for c in range(grp):
            off = pl.multiple_of((g * grp + c) * blk, blk)
            cur = pl.ds(off, blk)
            res = partial(2, off, kk[2, cur, :], vv[2, cur, :], bias16)
            store_part(0, pl.ds(c * (seq // grp) + g, blk, stride=grp), res)
        return carry

    lax.fori_loop(0, grp, group16, 0)

    def group4(g, carry):
        k_prev = v_prev = None
        for c in range(grp):
            off = pl.multiple_of((g * grp + c) * blk, blk)
            cur = pl.ds(off, blk)
            k_cur, v_cur = kk[1, cur, :], vv[1, cur, :]
            if c == 0:
                res = partial(1, off, k_cur, v_cur, bias4[:, blk:])
            else:
                res = partial(1, off, jnp.concatenate([k_prev, k_cur], axis=0),
                              jnp.concatenate([v_prev, v_cur], axis=0), bias4)
            k_prev, v_prev = k_cur, v_cur
            res = merge(res, load_part(0, cur))
            store_part(1, pl.ds(c * blk * grp + g, blk, stride=grp), res)
        return carry

    lax.fori_loop(0, grp, group4, 0)

    def group1(g, carry):
        k_prev = v_prev = None
        for c in range(grp):
            off = pl.multiple_of((g * grp + c) * blk, blk)
            cur = pl.ds(off, blk)
            k_cur, v_cur = kk[0, cur, :], vv[0, cur, :]
            bias = bias1
            if c == 0:
                prev = pl.ds(pl.multiple_of(jnp.maximum(g * grp - 1, 0) * blk, blk), blk)
                k_prev, v_prev = kk[0, prev, :], vv[0, prev, :]
                bias = bias + jnp.where(g > 0, 0.0, NEG_BIG) * prev_cols
            res = partial(0, off, jnp.concatenate([k_prev, k_cur], axis=0),
                          jnp.concatenate([v_prev, v_cur], axis=0), bias)
            k_prev, v_prev = k_cur, v_cur
            _, l_tot, a_tot = merge(res, load_part(1, cur))
            o = a_tot / l_tot
            sq = o * o
            ss0 = jnp.sum(jnp.where(first, sq, 0.0), axis=-1, keepdims=True)
            ss1 = jnp.sum(jnp.where(first, 0.0, sq), axis=-1, keepdims=True)
            ms = jnp.where(first, ss0, ss1) * (1.0 / ATTN_HEAD_DIM)
            o_ref[0, cur, :] = (o * lax.rsqrt(ms + NORM_EPS) * gain_ref[...]).astype(o_ref.dtype)
        return carry

    lax.fori_loop(0, grp, group1, 0)


def _attention(qa, ka, va, gain, slopes):
    b, seq, aw = qa.shape
    pairs = aw // LANES
    nbr = len(DILATED_CONFIGS)
    blk = pl.BlockSpec((1, seq, LANES), lambda i, p: (i, 0, p))
    return pl.pallas_call(
        _attn_kernel,
        grid=(b, pairs),
        in_specs=[pl.BlockSpec(memory_space=pltpu.SMEM), blk, blk, blk,
                  pl.BlockSpec((1, LANES), lambda i, p: (0, p))],
        out_specs=blk,
        out_shape=jax.ShapeDtypeStruct((b, seq, aw), BF16),
        scratch_shapes=[pltpu.VMEM((nbr, 2 * seq, LANES), BF16),
                        pltpu.VMEM((nbr, seq, LANES), BF16),
                        pltpu.VMEM((nbr, seq, LANES), BF16),
                        pltpu.VMEM((3, seq, LANES), F32),
                        pltpu.VMEM((2, 3, seq, LANES), F32)],
        compiler_params=pltpu.CompilerParams(dimension_semantics=("parallel", "parallel"),
                                             vmem_limit_bytes=VMEM_LIMIT),
        name="dilated_attention",
    )(slopes, qa, ka, va, gain.reshape(1, aw))


def _mlstm_kernel(q_ref, k_ref, v_ref, og_ref, g_ref, gain_ref, o_ref):
    chunk = MLSTM_CHUNK
    seq, dh = q_ref.shape[1], q_ref.shape[2]
    nc = seq // chunk
    assert chunk == LANES and dh == LANES

    ti = lax.broadcasted_iota(jnp.int32, (chunk, chunk), 0)
    si = lax.broadcasted_iota(jnp.int32, (chunk, chunk), 1)
    causal = si <= ti
    eye = (si == ti).astype(BF16)
    tri_t = (ti <= si).astype(BF16)
    lane = lax.broadcasted_iota(jnp.int32, (nc, chunk), 1)
    row = lax.broadcasted_iota(jnp.int32, (nc, chunk), 0)

    i_pre = g_ref[0, 0, 0]
    f_pre = g_ref[0, 0, 1]
    lf = jnp.minimum(f_pre, 0.0) - jnp.log1p(jnp.exp(-jnp.abs(f_pre)))
    b = sum(_dot(part, tri_t) for part in _split3(lf))
    u = i_pre - b
    cm = u
    shift = 1
    while shift < chunk:
        cm = jnp.where(lane >= shift, jnp.maximum(cm, pltpu.roll(cm, shift, axis=1)), cm)
        shift *= 2
    b_last = jnp.broadcast_to(b[:, chunk - 1:chunk], (nc, chunk))
    u_max = jnp.broadcast_to(cm[:, chunk - 1:chunk], (nc, chunk))
    m = jnp.zeros((1, chunk), F32)
    m_start = jnp.zeros((nc, chunk), F32)
    for c in range(nc - 1):
        m = b_last[c:c + 1, :] + jnp.maximum(m, u_max[c:c + 1, :])
        m_start = jnp.where(row == c + 1, m, m_start)
    m_next = b_last + jnp.maximum(m_start, u_max)
    z = jnp.maximum(m_start, cm)
    decay = jnp.exp(b_last + m_start - m_next)
    cols = [[p.astype(F32) for p in _split3(x)]
            for x in (z, jnp.exp(m_start - z), b + z, jnp.exp(b_last + u - m_next))]

    def column_forms(c):
        out = None
        for part in range(3):
            tiles = []
            for x in cols:
                v = jnp.broadcast_to(x[part][c:c + 1, :], (BF16_ROWS, chunk)).astype(BF16)
                tiles.extend([v] * (chunk // BF16_ROWS))
            term = _dot_nt(eye, jnp.concatenate(tiles, axis=0))
            out = term if out is None else out + term
        return out

    ones = jnp.ones((chunk, dh), BF16)
    state = jnp.zeros((dh, 2 * dh), F32)
    for c in range(nc):
        rows = slice(c * chunk, (c + 1) * chunk)
        qb, kb = q_ref[0, rows, :], k_ref[0, rows, :]
        v_ext = jnp.concatenate([v_ref[0, rows, :], ones], axis=1)
        colf = column_forms(c)
        z_col, gw_col, mrow_col, w_col = (colf[:, j * chunk:(j + 1) * chunk] for j in range(4))
        dw = jnp.where(causal, jnp.exp(u[c:c + 1, :] - z_col), 0.0)
        sc = (_dot_nt(qb, kb) * dw).astype(BF16)
        nd = jnp.concatenate([gw_col, gw_col], axis=1) * _dot(qb, state.astype(BF16)) + _dot(sc, v_ext)
        hidden = nd[:, :dh] / jnp.maximum(jnp.abs(nd[:, dh:]), jnp.exp(-mrow_col))
        gated = og_ref[0, rows, :].astype(F32) * hidden
        o_ref[0, rows, :] = _rmsnorm(gated, gain_ref[...]).astype(o_ref.dtype)
        wk = (w_col * kb.astype(F32)).astype(BF16)
        upd = lax.dot_general(wk, v_ext, (((0,), (0,)), ((), ())), preferred_element_type=F32)
        dec = decay[c:c + 1, :]
        state = jnp.concatenate([dec, dec], axis=1) * state + upd


def _mlstm(qm, km, vm, om, gates, gain):
    b, seq, mw = qm.shape
    dh = MLSTM_HEAD_DIM
    nc = seq // MLSTM_CHUNK
    blk = pl.BlockSpec((1, seq, dh), lambda i, h: (i, 0, h))
    return pl.pallas_call(
        _mlstm_kernel,
        grid=(b, mw // dh),
        in_specs=[blk, blk, blk, blk,
                  pl.BlockSpec((1, 1, 2, nc, MLSTM_CHUNK), lambda i, h: (i, h, 0, 0, 0)),
                  pl.BlockSpec((1, dh), lambda i, h: (0, h))],
        out_specs=blk,
        out_shape=jax.ShapeDtypeStruct((b, seq, mw), BF16),
        compiler_params=pltpu.CompilerParams(dimension_semantics=("parallel", "parallel"),
                                             vmem_limit_bytes=VMEM_LIMIT),
        name="mlstm",
    )(qm, km, vm, om, gates, gain.reshape(1, mw))


def kernel(x, ffn1_norm_w, ffn1_w_gate, ffn1_w_up, ffn1_w_down, mix_norm_w, w_in, q_norm_w, k_norm_w, conv_w,
           conv_b, i_bias, f_bias, attn_out_gain, mlstm_out_gain, w_out, ffn2_norm_w, ffn2_w_gate, ffn2_w_up,
           ffn2_w_down):
    b, seq, d = x.shape
    n = b * seq
    aw = ATTN_HEADS * ATTN_HEAD_DIM
    mw = MLSTM_HEADS * MLSTM_HEAD_DIM
    n_main = 3 * aw + 4 * mw
    hm = MLSTM_HEADS
    slopes = jnp.asarray(2.0 ** (-8.0 * np.arange(1, ATTN_HEADS + 1, dtype=np.float32) / ATTN_HEADS), F32)
    xt = x.reshape(n, d)
    for l in range(ffn1_norm_w.shape[0]):
        xt = _ffn(xt, ffn1_norm_w[l].reshape(1, d), ffn1_w_gate[l].astype(BF16), ffn1_w_up[l].astype(BF16),
                  ffn1_w_down[l].astype(BF16))
        w_l = w_in[l].astype(BF16)
        gate_bias = jnp.concatenate([i_bias[l], f_bias[l]]).astype(F32)
        qa, ka, va, qm, km, vm, om, gr = _inproj(xt, seq, mix_norm_w[l], w_l[:, :n_main], w_l[:, n_main:],
                                                 gate_bias, q_norm_w[l], k_norm_w[l], conv_w[l], conv_b[l])
        shp = lambda t: t.reshape(b, seq, -1)
        attn = _attention(shp(qa), shp(ka), shp(va), attn_out_gain[l], slopes)
        gates = gr.reshape(2, hm, b, seq // MLSTM_CHUNK, MLSTM_CHUNK).transpose(2, 1, 0, 3, 4)
        mls = _mlstm(shp(qm), shp(km), shp(vm), shp(om), gates, mlstm_out_gain[l])
        w_o = w_out[l].astype(BF16)
        xt = _ffn(xt, ffn2_norm_w[l].reshape(1, d), ffn2_w_gate[l].astype(BF16), ffn2_w_up[l].astype(BF16),
                  ffn2_w_down[l].astype(BF16),
                  mix=(attn.reshape(n, aw), mls.reshape(n, mw), w_o[:aw], w_o[aw:]))
    return xt.reshape(b, seq, d)
```

```python
import functools

import numpy as np
import jax
import jax.numpy as jnp
from jax import lax
from jax.experimental import pallas as pl
from jax.experimental.pallas import tpu as pltpu

F32 = jnp.float32
BF16 = jnp.bfloat16

LANES = 128
SUBLANES = 8
BF16_ROWS = 16
NORM_EPS = 1e-6
ATTN_HEADS = 8
ATTN_HEAD_DIM = 64
MLSTM_HEADS = 4
MLSTM_HEAD_DIM = 128
CONV_WIDTH = 4
DILATED_CONFIGS = ((128, 1), (512, 4), (2048, 16))
ATTN_BLOCK = 128
MLSTM_CHUNK = 128
NEG_BIG = -1e30
ATTN_GROUP = 4
LOG2E = 1.4426950408889634
FFN_TOKENS = 512
VMEM_LIMIT = 56 * 1024 * 1024


def _const_spec(shape):
    nd = len(shape)
    return pl.BlockSpec(shape, lambda *_: (0,) * nd, pipeline_mode=pl.Buffered(1))


def _rmsnorm(x, w):
    ms = jnp.mean(x * x, axis=-1, keepdims=True)
    return x * lax.rsqrt(ms + NORM_EPS) * w


def _dot(a, b):
    return jnp.dot(a, b, preferred_element_type=F32)


def _dot_nt(a, b):
    return lax.dot_general(a, b, (((1,), (1,)), ((), ())), preferred_element_type=F32)


def _split3(x):
    h1 = x.astype(BF16)
    r1 = x - h1.astype(F32)
    h2 = r1.astype(BF16)
    h3 = (r1 - h2.astype(F32)).astype(BF16)
    return h1, h2, h3


def _swiglu_half(x, nw_ref, wg_ref, wu_ref, wd_ref):
    h = _rmsnorm(x, nw_ref[...]).astype(BF16)
    g = _dot(h, wg_ref[...])
    u = _dot(h, wu_ref[...])
    a = (g * jax.nn.sigmoid(g) * u).astype(BF16)
    return x + 0.5 * _dot(a, wd_ref[...])


def _ffn_kernel(x_ref, nw_ref, wg_ref, wu_ref, wd_ref, o_ref):
    o_ref[...] = _swiglu_half(x_ref[...], nw_ref, wg_ref, wu_ref, wd_ref)


def _outproj_ffn_kernel(x_ref, ya_ref, ym_ref, woa_ref, wom_ref, nw_ref, wg_ref, wu_ref, wd_ref, o_ref):
    x = x_ref[...] + _dot(ya_ref[...], woa_ref[...]) + _dot(ym_ref[...], wom_ref[...])
    o_ref[...] = _swiglu_half(x, nw_ref, wg_ref, wu_ref, wd_ref)


def _ffn(x, nw, wg, wu, wd, mix=None):
    n, d = x.shape
    f = wg.shape[1]
    tm = FFN_TOKENS
    tok = lambda w: pl.BlockSpec((tm, w), lambda i: (i, 0))
    w_specs = [_const_spec((1, d)), _const_spec((d, f)), _const_spec((d, f)), _const_spec((f, d))]
    if mix is None:
        body, ins, specs = _ffn_kernel, (x,), [tok(d)]
    else:
        ya, ym, woa, wom = mix
        body, ins = _outproj_ffn_kernel, (x, ya, ym, woa, wom)
        specs = [tok(d), tok(ya.shape[1]), tok(ym.shape[1]), _const_spec(woa.shape), _const_spec(wom.shape)]
    return pl.pallas_call(
        body,
        grid=(n // tm,),
        in_specs=specs + w_specs,
        out_specs=tok(d),
        out_shape=jax.ShapeDtypeStruct((n, d), F32),
        compiler_params=pltpu.CompilerParams(dimension_semantics=("parallel",), vmem_limit_bytes=VMEM_LIMIT),
        name="ffn" if mix is None else "outproj_ffn",
    )(*ins, nw, wg, wu, wd)


def _inproj_kernel(tiles_per_seq, x_ref, nw_ref, w_ref, wgr_ref, gbr_ref, qnw_ref, knw_ref, cw_ref, cb_ref,
                   qa_ref, ka_ref, va_ref, qm_ref, km_ref, vm_ref, om_ref, gr_ref, cpad):
    tm = x_ref.shape[0]
    aw = qa_ref.shape[1]
    mw = qm_ref.shape[1]
    h = _rmsnorm(x_ref[...], nw_ref[...]).astype(BF16)
    first = lax.broadcasted_iota(jnp.int32, (1, LANES), 1) < ATTN_HEAD_DIM

    def head_norm(dst, p, w_row, scale):
        for c in range(0, aw, LANES):
            ps = p[:, c:c + LANES]
            sq = ps * ps
            ss0 = jnp.sum(jnp.where(first, sq, 0.0), axis=-1, keepdims=True)
            ss1 = jnp.sum(jnp.where(first, 0.0, sq), axis=-1, keepdims=True)
            ms = jnp.where(first, ss0, ss1) * (1.0 / ATTN_HEAD_DIM)
            dst[:, c:c + LANES] = ps * lax.rsqrt(ms + NORM_EPS) * (w_row[:, c:c + LANES] * scale)

    head_norm(qa_ref, _dot(h, w_ref[:, 0:aw]), qnw_ref[...], ATTN_HEAD_DIM ** -0.5 * LOG2E)
    head_norm(ka_ref, _dot(h, w_ref[:, aw:2 * aw]), knw_ref[...], 1.0)
    va_ref[...] = _dot(h, w_ref[:, 2 * aw:3 * aw])

    base = 3 * aw
    hist = CONV_WIDTH - 1

    @pl.when(pl.program_id(0) % tiles_per_seq == 0)
    def _():
        cpad[0:SUBLANES, :] = jnp.zeros((SUBLANES, 2 * mw), F32)

    cpad[SUBLANES:, :] = _dot(h, w_ref[:, base:base + 2 * mw])
    y = cb_ref[...]
    for j in range(CONV_WIDTH):
        y = y + cpad[SUBLANES - hist + j:SUBLANES - hist + j + tm, :] * cw_ref[j:j + 1, :]
    y = y * jax.nn.sigmoid(y)
    qm_ref[...] = y[:, :mw].astype(BF16)
    km_ref[...] = (y[:, mw:] * (MLSTM_HEAD_DIM ** -0.5)).astype(BF16)
    cpad[0:SUBLANES, :] = cpad[tm:tm + SUBLANES, :]

    vm_ref[...] = _dot(h, w_ref[:, base + 2 * mw:base + 3 * mw]).astype(BF16)
    om_ref[...] = jax.nn.sigmoid(_dot(h, w_ref[:, base + 3 * mw:base + 4 * mw])).astype(BF16)
    ng = gr_ref.shape[0]
    gr_ref[...] = _dot_nt(wgr_ref[...], h)[:ng, :] + gbr_ref[...]


def _inproj(x, seq, nw, w_main, w_gate, gate_bias, qnw, knw, conv_w, conv_b):
    n, d = x.shape
    tm = FFN_TOKENS
    aw = ATTN_HEADS * ATTN_HEAD_DIM
    mw = MLSTM_HEADS * MLSTM_HEAD_DIM
    ng = 2 * MLSTM_HEADS
    wgr = jnp.zeros((BF16_ROWS, d), BF16).at[:ng, :].set(w_gate.T)
    gbr = gate_bias.reshape(ng, 1)
    qnw_t = jnp.tile(qnw, ATTN_HEADS).reshape(1, aw)
    knw_t = jnp.tile(knw, ATTN_HEADS).reshape(1, aw)
    tok = lambda w: pl.BlockSpec((tm, w), lambda i: (i, 0))
    out_shape = ([jax.ShapeDtypeStruct((n, aw), F32)] * 3 + [jax.ShapeDtypeStruct((n, mw), BF16)] * 4
                 + [jax.ShapeDtypeStruct((ng, n), F32)])
    out_specs = [tok(aw)] * 3 + [tok(mw)] * 4 + [pl.BlockSpec((ng, tm), lambda i: (0, i))]
    return pl.pallas_call(
        functools.partial(_inproj_kernel, seq // tm),
        grid=(n // tm,),
        in_specs=[tok(d), _const_spec((1, d)), _const_spec(w_main.shape), _const_spec(wgr.shape),
                  _const_spec(gbr.shape), _const_spec((1, aw)), _const_spec((1, aw)),
                  _const_spec(conv_w.shape), _const_spec((1, 2 * mw))],
        out_specs=out_specs,
        out_shape=out_shape,
        scratch_shapes=[pltpu.VMEM((tm + SUBLANES, 2 * mw), F32)],
        compiler_params=pltpu.CompilerParams(dimension_semantics=("arbitrary",), vmem_limit_bytes=VMEM_LIMIT),
        name="in_proj",
    )(x, nw.reshape(1, d), w_main, wgr, gbr, qnw_t, knw_t, conv_w, conv_b.reshape(1, 2 * mw))


def _attn_kernel(slopes_ref, q_ref, k_ref, v_ref, gain_ref, o_ref, qs, kk, vv, x4f, part):
    blk = ATTN_BLOCK
    seq = q_ref.shape[1]
    nblk = seq // blk
    grp = ATTN_GROUP
    assert [d for _, d in DILATED_CONFIGS] == [1, grp, grp * grp] and nblk == grp * grp
    pair = pl.program_id(1)
    lane = lax.broadcasted_iota(jnp.int32, (blk, LANES), 1)
    first = lane < ATTN_HEAD_DIM
    slope = (slopes_ref[2 * pair], slopes_ref[2 * pair + 1])
    refs = (q_ref, k_ref, v_ref)

    def stage(order, j, q, k, v):
        qs[order, 2 * j * blk:(2 * j + 1) * blk, :] = jnp.where(first, q, 0.0).astype(BF16)
        qs[order, (2 * j + 1) * blk:(2 * j + 2) * blk, :] = jnp.where(first, 0.0, q).astype(BF16)
        kk[order, j * blk:(j + 1) * blk, :] = k.astype(BF16)
        vv[order, j * blk:(j + 1) * blk, :] = v.astype(BF16)

    for j in range(nblk):
        stage(0, j, *(ref[0, j * blk:(j + 1) * blk, :] for ref in refs))
    for j in range(nblk):
        r, c = divmod(j, grp)
        x = [ref[0, pl.ds(c * blk * grp + r, blk, stride=grp), :] for ref in refs]
        for i in range(3):
            x4f[i, j * blk:(j + 1) * blk, :] = x[i]
        stage(1, j, *x)
    for j in range(nblk):
        start = (j % grp) * (seq // grp) + j // grp
        stage(2, j, *(x4f[i, pl.ds(start, blk, stride=grp), :] for i in range(3)))

    def make_bias(dil, n_steps):
        qi = lax.broadcasted_iota(jnp.int32, (blk, 2 * blk), 0)
        ki = lax.broadcasted_iota(jnp.int32, (blk, 2 * blk), 1)
        steps = qi + blk - ki
        band = (steps >= 0) & (steps <= n_steps)
        dist = (steps * dil).astype(F32) * LOG2E
        return jnp.concatenate([jnp.where(band, -slope[h] * dist, NEG_BIG) for h in range(2)], axis=0)

    ones = jnp.ones((2 * blk, LANES), BF16)

    def partial(order, off, kcat, vcat, bias):
        nk = kcat.shape[0]
        s = _dot_nt(qs[order, pl.ds(2 * off, 2 * blk), :], kcat) + bias
        m = jnp.max(s, axis=-1, keepdims=True)
        p = jnp.exp2(s - m).astype(BF16)
        r = _dot(p, jnp.concatenate([vcat, ones[:nk]], axis=1))
        return (jnp.where(first, m[:blk], m[blk:]),
                jnp.where(first, r[:blk, LANES:], r[blk:, LANES:]),
                jnp.where(first, r[:blk, :LANES], r[blk:, :LANES]))

    def merge(a, b):
        m = jnp.maximum(a[0], b[0])
        wa = jnp.exp2(a[0] - m)
        wb = jnp.exp2(b[0] - m)
        return m, a[1] * wa + b[1] * wb, a[2] * wa + b[2] * wb

    def load_part(stage_i, idx):
        return tuple(part[stage_i, i, idx, :] for i in range(3))

    def store_part(stage_i, idx, val):
        for i in range(3):
            part[stage_i, i, idx, :] = val[i]

    (w1, d1), (w4, d4), (w16, d16) = DILATED_CONFIGS
    bias16 = make_bias(d16, w16 // d16)[:, blk:]
    bias4 = make_bias(d4, w4 // d4)
    bias1 = make_bias(d1, w1 // d1)

    def group16(g, carry):
        for c in range(grp):
            off = (g * grp + c) * blk
            cur = pl.ds(off, blk)
            res = partial(2, off, kk[2, cur, :], vv[2, cur, :], bias16)
            store_part(0, pl.ds(c * (seq // grp) + g, blk, stride=grp), res)
        return carry

    for g in range(grp):
        group16(g, 0)

    def group4(g, carry):
        k_prev = v_prev = None
        for c in range(grp):
            off = (g * grp + c) * blk
            cur = pl.ds(off, blk)
            k_cur, v_cur = kk[1, cur, :], vv[1, cur, :]
            if c == 0:
                res = partial(1, off, k_cur, v_cur, bias4[:, blk:])
            else:
                res = partial(1, off, jnp.concatenate([k_prev, k_cur], axis=0),
                              jnp.concatenate([v_prev, v_cur], axis=0), bias4)
            k_prev, v_prev = k_cur, v_cur
            res = merge(res, load_part(0, cur))
            store_part(1, pl.ds(c * blk * grp + g, blk, stride=grp), res)
        return carry

    for g in range(grp):
        group4(g, 0)

    def group1(g, carry):
        k_prev = v_prev = None
        for c in range(grp):
            off = (g * grp + c) * blk
            cur = pl.ds(off, blk)
            k_cur, v_cur = kk[0, cur, :], vv[0, cur, :]
            if g == 0 and c == 0:
                res = partial(0, off, k_cur, v_cur, bias1[:, blk:])
            else:
                if c == 0:
                    prev = pl.ds(off - blk, blk)
                    k_prev, v_prev = kk[0, prev, :], vv[0, prev, :]
                res = partial(0, off, jnp.concatenate([k_prev, k_cur], axis=0),
                              jnp.concatenate([v_prev, v_cur], axis=0), bias1)
            k_prev, v_prev = k_cur, v_cur
            _, l_tot, a_tot = merge(res, load_part(1, cur))
            o = a_tot / l_tot
            sq = o * o
            ss0 = jnp.sum(jnp.where(first, sq, 0.0), axis=-1, keepdims=True)
            ss1 = jnp.sum(jnp.where(first, 0.0, sq), axis=-1, keepdims=True)
            ms = jnp.where(first, ss0, ss1) * (1.0 / ATTN_HEAD_DIM)
            o_ref[0, cur, :] = (o * lax.rsqrt(ms + NORM_EPS) * gain_ref[...]).astype(o_ref.dtype)
        return carry

    for g in range(grp):
        group1(g, 0)


def _attention(qa, ka, va, gain, slopes):
    b, seq, aw = qa.shape
    pairs = aw // LANES
    nbr = len(DILATED_CONFIGS)
    blk = pl.BlockSpec((1, seq, LANES), lambda i, p: (i, 0, p))
    return pl.pallas_call(
        _attn_kernel,
        grid=(b, pairs),
        in_specs=[pl.BlockSpec(memory_space=pltpu.SMEM), blk, blk, blk,
                  pl.BlockSpec((1, LANES), lambda i, p: (0, p))],
        out_specs=blk,
        out_shape=jax.ShapeDtypeStruct((b, seq, aw), BF16),
        scratch_shapes=[pltpu.VMEM((nbr, 2 * seq, LANES), BF16),
                        pltpu.VMEM((nbr, seq, LANES), BF16),
                        pltpu.VMEM((nbr, seq, LANES), BF16),
                        pltpu.VMEM((3, seq, LANES), F32),
                        pltpu.VMEM((2, 3, seq, LANES), F32)],
        compiler_params=pltpu.CompilerParams(dimension_semantics=("parallel", "parallel"),
                                             vmem_limit_bytes=VMEM_LIMIT),
        name="dilated_attention",
    )(slopes, qa, ka, va, gain.reshape(1, aw))


def _mlstm_kernel(q_ref, k_ref, v_ref, og_ref, g_ref, gain_ref, o_ref):
    chunk = MLSTM_CHUNK
    seq, dh = q_ref.shape[1], q_ref.shape[2]
    nc = seq // chunk
    assert chunk == LANES and dh == LANES

    ti = lax.broadcasted_iota(jnp.int32, (chunk, chunk), 0)
    si = lax.broadcasted_iota(jnp.int32, (chunk, chunk), 1)
    causal = si <= ti
    eye = (si == ti).astype(BF16)
    tri_t = (ti <= si).astype(BF16)
    lane = lax.broadcasted_iota(jnp.int32, (nc, chunk), 1)
    row = lax.broadcasted_iota(jnp.int32, (nc, chunk), 0)

    i_pre = g_ref[0, 0, 0]
    f_pre = g_ref[0, 0, 1]
    lf = jnp.minimum(f_pre, 0.0) - jnp.log1p(jnp.exp(-jnp.abs(f_pre)))
    b = sum(_dot(part, tri_t) for part in _split3(lf))
    u = i_pre - b
    cm = u
    shift = 1
    while shift < chunk:
        cm = jnp.where(lane >= shift, jnp.maximum(cm, pltpu.roll(cm, shift, axis=1)), cm)
        shift *= 2
    b_last = jnp.broadcast_to(b[:, chunk - 1:chunk], (nc, chunk))
    u_max = jnp.broadcast_to(cm[:, chunk - 1:chunk], (nc, chunk))
    m = jnp.zeros((1, chunk), F32)
    m_start = jnp.zeros((nc, chunk), F32)
    for c in range(nc - 1):
        m = b_last[c:c + 1, :] + jnp.maximum(m, u_max[c:c + 1, :])
        m_start = jnp.where(row == c + 1, m, m_start)
    m_next = b_last + jnp.maximum(m_start, u_max)
    z = jnp.maximum(m_start, cm)
    decay = jnp.exp(b_last + m_start - m_next)
    cols = [[p.astype(F32) for p in _split3(x)]
            for x in (z, jnp.exp(m_start - z), b + z, jnp.exp(b_last + u - m_next))]

    def column_forms(c):
        out = None
        for part in range(3):
            tiles = []
            for x in cols:
                v = jnp.broadcast_to(x[part][c:c + 1, :], (BF16_ROWS, chunk)).astype(BF16)
                tiles.extend([v] * (chunk // BF16_ROWS))
            term = _dot_nt(eye, jnp.concatenate(tiles, axis=0))
            out = term if out is None else out + term
        return out

    ones = jnp.ones((chunk, dh), BF16)
    state = jnp.zeros((dh, 2 * dh), F32)
    for c in range(nc):
        rows = slice(c * chunk, (c + 1) * chunk)
        qb, kb = q_ref[0, rows, :], k_ref[0, rows, :]
        v_ext = jnp.concatenate([v_ref[0, rows, :], ones], axis=1)
        colf = column_forms(c)
        z_col, gw_col, mrow_col, w_col = (colf[:, j * chunk:(j + 1) * chunk] for j in range(4))
        dw = jnp.where(causal, jnp.exp(u[c:c + 1, :] - z_col), 0.0)
        sc = (_dot_nt(qb, kb) * dw).astype(BF16)
        nd = jnp.concatenate([gw_col, gw_col], axis=1) * _dot(qb, state.astype(BF16)) + _dot(sc, v_ext)
        hidden = nd[:, :dh] / jnp.maximum(jnp.abs(nd[:, dh:]), jnp.exp(-mrow_col))
        gated = og_ref[0, rows, :].astype(F32) * hidden
        o_ref[0, rows, :] = _rmsnorm(gated, gain_ref[...]).astype(o_ref.dtype)
        wk = (w_col * kb.astype(F32)).astype(BF16)
        upd = lax.dot_general(wk, v_ext, (((0,), (0,)), ((), ())), preferred_element_type=F32)
        dec = decay[c:c + 1, :]
        state = jnp.concatenate([dec, dec], axis=1) * state + upd


def _mlstm(qm, km, vm, om, gates, gain):
    b, seq, mw = qm.shape
    dh = MLSTM_HEAD_DIM
    nc = seq // MLSTM_CHUNK
    blk = pl.BlockSpec((1, seq, dh), lambda i, h: (i, 0, h))
    return pl.pallas_call(
        _mlstm_kernel,
        grid=(b, mw // dh),
        in_specs=[blk, blk, blk, blk,
                  pl.BlockSpec((1, 1, 2, nc, MLSTM_CHUNK), lambda i, h: (i, h, 0, 0, 0)),
                  pl.BlockSpec((1, dh), lambda i, h: (0, h))],
        out_specs=blk,
        out_shape=jax.ShapeDtypeStruct((b, seq, mw), BF16),
        compiler_params=pltpu.CompilerParams(dimension_semantics=("parallel", "parallel"),
                                             vmem_limit_bytes=VMEM_LIMIT),
        name="mlstm",
    )(qm, km, vm, om, gates, gain.reshape(1, mw))


def kernel(x, ffn1_norm_w, ffn1_w_gate, ffn1_w_up, ffn1_w_down, mix_norm_w, w_in, q_norm_w, k_norm_w, conv_w,
           conv_b, i_bias, f_bias, attn_out_gain, mlstm_out_gain, w_out, ffn2_norm_w, ffn2_w_gate, ffn2_w_up,
           ffn2_w_down):
    b, seq, d = x.shape
    n = b * seq
    aw = ATTN_HEADS * ATTN_HEAD_DIM
    mw = MLSTM_HEADS * MLSTM_HEAD_DIM
    n_main = 3 * aw + 4 * mw
    hm = MLSTM_HEADS
    slopes = jnp.asarray(2.0 ** (-8.0 * np.arange(1, ATTN_HEADS + 1, dtype=np.float32) / ATTN_HEADS), F32)
    xt = x.reshape(n, d)
    for l in range(ffn1_norm_w.shape[0]):
        xt = _ffn(xt, ffn1_norm_w[l].reshape(1, d), ffn1_w_gate[l].astype(BF16), ffn1_w_up[l].astype(BF16),
                  ffn1_w_down[l].astype(BF16))
        w_l = w_in[l].astype(BF16)
        gate_bias = jnp.concatenate([i_bias[l], f_bias[l]]).astype(F32)
        qa, ka, va, qm, km, vm, om, gr = _inproj(xt, seq, mix_norm_w[l], w_l[:, :n_main], w_l[:, n_main:],
                                                 gate_bias, q_norm_w[l], k_norm_w[l], conv_w[l], conv_b[l])
        shp = lambda t: t.reshape(b, seq, -1)
        attn = _attention(shp(qa), shp(ka), shp(va), attn_out_gain[l], slopes)
        gates = gr.reshape(2, hm, b, seq // MLSTM_CHUNK, MLSTM_CHUNK).transpose(2, 1, 0, 3, 4)
        mls = _mlstm(shp(qm), shp(km), shp(vm), shp(om), gates, mlstm_out_gain[l])
        w_o = w_out[l].astype(BF16)
        xt = _ffn(xt, ffn2_norm_w[l].reshape(1, d), ffn2_w_gate[l].astype(BF16), ffn2_w_up[l].astype(BF16),
                  ffn2_w_down[l].astype(BF16),
                  mix=(attn.reshape(n, aw), mls.reshape(n, mw), w_o[:aw], w_o[aw:]))
    return xt.reshape(b, seq, d)
```

```python
import functools

import numpy as np
import jax
import jax.numpy as jnp
from jax import lax
from jax.experimental import pallas as pl
from jax.experimental.pallas import tpu as pltpu

F32 = jnp.float32
BF16 = jnp.bfloat16

LANES = 128
SUBLANES = 8
BF16_ROWS = 16
NORM_EPS = 1e-6
ATTN_HEADS = 8
ATTN_HEAD_DIM = 64
MLSTM_HEADS = 4
MLSTM_HEAD_DIM = 128
CONV_WIDTH = 4
DILATED_CONFIGS = ((128, 1), (512, 4), (2048, 16))
ATTN_BLOCK = 128
MLSTM_CHUNK = 128
NEG_BIG = -1e30
ATTN_GROUP = 4
LOG2E = 1.4426950408889634
FFN_TOKENS = 512
VMEM_LIMIT = 56 * 1024 * 1024


def _const_spec(shape):
    nd = len(shape)
    return pl.BlockSpec(shape, lambda *_: (0,) * nd, pipeline_mode=pl.Buffered(1))


def _rmsnorm(x, w):
    ms = jnp.mean(x * x, axis=-1, keepdims=True)
    return x * lax.rsqrt(ms + NORM_EPS) * w


def _dot(a, b):
    return jnp.dot(a, b, preferred_element_type=F32)


def _dot_nt(a, b):
    return lax.dot_general(a, b, (((1,), (1,)), ((), ())), preferred_element_type=F32)


def _split3(x):
    h1 = x.astype(BF16)
    r1 = x - h1.astype(F32)
    h2 = r1.astype(BF16)
    h3 = (r1 - h2.astype(F32)).astype(BF16)
    return h1, h2, h3


def _swiglu_half(x, nw_ref, wg_ref, wu_ref, wd_ref):
    h = _rmsnorm(x, nw_ref[...]).astype(BF16)
    g = _dot(h, wg_ref[...])
    u = _dot(h, wu_ref[...])
    a = (g * jax.nn.sigmoid(g) * u).astype(BF16)
    return x + 0.5 * _dot(a, wd_ref[...])


def _ffn_kernel(x_ref, nw_ref, wg_ref, wu_ref, wd_ref, o_ref):
    o_ref[...] = _swiglu_half(x_ref[...], nw_ref, wg_ref, wu_ref, wd_ref)


def _outproj_ffn_kernel(x_ref, ya_ref, ym_ref, woa_ref, wom_ref, nw_ref, wg_ref, wu_ref, wd_ref, o_ref):
    x = x_ref[...] + _dot(ya_ref[...], woa_ref[...]) + _dot(ym_ref[...], wom_ref[...])
    o_ref[...] = _swiglu_half(x, nw_ref, wg_ref, wu_ref, wd_ref)


def _ffn(x, nw, wg, wu, wd, mix=None):
    n, d = x.shape
    f = wg.shape[1]
    tm = FFN_TOKENS
    tok = lambda w: pl.BlockSpec((tm, w), lambda i: (i, 0))
    w_specs = [_const_spec((1, d)), _const_spec((d, f)), _const_spec((d, f)), _const_spec((f, d))]
    if mix is None:
        body, ins, specs = _ffn_kernel, (x,), [tok(d)]
    else:
        ya, ym, woa, wom = mix
        body, ins = _outproj_ffn_kernel, (x, ya, ym, woa, wom)
        specs = [tok(d), tok(ya.shape[1]), tok(ym.shape[1]), _const_spec(woa.shape), _const_spec(wom.shape)]
    return pl.pallas_call(
        body,
        grid=(n // tm,),
        in_specs=specs + w_specs,
        out_specs=tok(d),
        out_shape=jax.ShapeDtypeStruct((n, d), F32),
        compiler_params=pltpu.CompilerParams(dimension_semantics=("parallel",), vmem_limit_bytes=VMEM_LIMIT),
        name="ffn" if mix is None else "outproj_ffn",
    )(*ins, nw, wg, wu, wd)


def _inproj_kernel(tiles_per_seq, x_ref, nw_ref, w_ref, wgr_ref, gbr_ref, qnw_ref, knw_ref, cw_ref, cb_ref,
                   qa_ref, ka_ref, va_ref, qm_ref, km_ref, vm_ref, om_ref, gr_ref, cpad):
    tm = x_ref.shape[0]
    aw = qa_ref.shape[1]
    mw = qm_ref.shape[1]
    h = _rmsnorm(x_ref[...], nw_ref[...]).astype(BF16)
    first = lax.broadcasted_iota(jnp.int32, (1, LANES), 1) < ATTN_HEAD_DIM

    def head_norm(dst, p, w_row, scale):
        for c in range(0, aw, LANES):
            ps = p[:, c:c + LANES]
            sq = ps * ps
            ss0 = jnp.sum(jnp.where(first, sq, 0.0), axis=-1, keepdims=True)
            ss1 = jnp.sum(jnp.where(first, 0.0, sq), axis=-1, keepdims=True)
            ms = jnp.where(first, ss0, ss1) * (1.0 / ATTN_HEAD_DIM)
            dst[:, c:c + LANES] = ps * lax.rsqrt(ms + NORM_EPS) * (w_row[:, c:c + LANES] * scale)

    head_norm(qa_ref, _dot(h, w_ref[:, 0:aw]), qnw_ref[...], ATTN_HEAD_DIM ** -0.5 * LOG2E)
    head_norm(ka_ref, _dot(h, w_ref[:, aw:2 * aw]), knw_ref[...], 1.0)
    va_ref[...] = _dot(h, w_ref[:, 2 * aw:3 * aw])

    base = 3 * aw
    hist = CONV_WIDTH - 1

    @pl.when(pl.program_id(0) % tiles_per_seq == 0)
    def _():
        cpad[0:SUBLANES, :] = jnp.zeros((SUBLANES, 2 * mw), F32)

    cpad[SUBLANES:, :] = _dot(h, w_ref[:, base:base + 2 * mw])
    y = cb_ref[...]
    for j in range(CONV_WIDTH):
        y = y + cpad[SUBLANES - hist + j:SUBLANES - hist + j + tm, :] * cw_ref[j:j + 1, :]
    y = y * jax.nn.sigmoid(y)
    qm_ref[...] = y[:, :mw].astype(BF16)
    km_ref[...] = (y[:, mw:] * (MLSTM_HEAD_DIM ** -0.5)).astype(BF16)
    cpad[0:SUBLANES, :] = cpad[tm:tm + SUBLANES, :]

    vm_ref[...] = _dot(h, w_ref[:, base + 2 * mw:base + 3 * mw]).astype(BF16)
    om_ref[...] = jax.nn.sigmoid(_dot(h, w_ref[:, base + 3 * mw:base + 4 * mw])).astype(BF16)
    ng = gr_ref.shape[0]
    gr_ref[...] = _dot_nt(wgr_ref[...], h)[:ng, :] + gbr_ref[...]


def _inproj(x, seq, nw, w_main, w_gate, gate_bias, qnw, knw, conv_w, conv_b):
    n, d = x.shape
    tm = FFN_TOKENS
    aw = ATTN_HEADS * ATTN_HEAD_DIM
    mw = MLSTM_HEADS * MLSTM_HEAD_DIM
    ng = 2 * MLSTM_HEADS
    wgr = jnp.zeros((BF16_ROWS, d), BF16).at[:ng, :].set(w_gate.T)
    gbr = gate_bias.reshape(ng, 1)
    qnw_t = jnp.tile(qnw, ATTN_HEADS).reshape(1, aw)
    knw_t = jnp.tile(knw, ATTN_HEADS).reshape(1, aw)
    tok = lambda w: pl.BlockSpec((tm, w), lambda i: (i, 0))
    out_shape = ([jax.ShapeDtypeStruct((n, aw), F32)] * 3 + [jax.ShapeDtypeStruct((n, mw), BF16)] * 4
                 + [jax.ShapeDtypeStruct((ng, n), F32)])
    out_specs = [tok(aw)] * 3 + [tok(mw)] * 4 + [pl.BlockSpec((ng, tm), lambda i: (0, i))]
    return pl.pallas_call(
        functools.partial(_inproj_kernel, seq // tm),
        grid=(n // tm,),
        in_specs=[tok(d), _const_spec((1, d)), _const_spec(w_main.shape), _const_spec(wgr.shape),
                  _const_spec(gbr.shape), _const_spec((1, aw)), _const_spec((1, aw)),
                  _const_spec(conv_w.shape), _const_spec((1, 2 * mw))],
        out_specs=out_specs,
        out_shape=out_shape,
        scratch_shapes=[pltpu.VMEM((tm + SUBLANES, 2 * mw), F32)],
        compiler_params=pltpu.CompilerParams(dimension_semantics=("arbitrary",), vmem_limit_bytes=VMEM_LIMIT),
        name="in_proj",
    )(x, nw.reshape(1, d), w_main, wgr, gbr, qnw_t, knw_t, conv_w, conv_b.reshape(1, 2 * mw))


def _attn_kernel(slopes_ref, q_ref, k_ref, v_ref, gain_ref, o_ref, qs, kk, vv, x4f, part):
    blk = ATTN_BLOCK
    seq = q_ref.shape[1]
    nblk = seq // blk
    grp = ATTN_GROUP
    assert [d for _, d in DILATED_CONFIGS] == [1, grp, grp * grp] and nblk == grp * grp
    pair = pl.program_id(1)
    lane = lax.broadcasted_iota(jnp.int32, (blk, LANES), 1)
    first = lane < ATTN_HEAD_DIM
    slope = (slopes_ref[2 * pair], slopes_ref[2 * pair + 1])
    refs = (q_ref, k_ref, v_ref)

    def stage(order, j, q, k, v):
        qs[order, 2 * j * blk:(2 * j + 1) * blk, :] = jnp.where(first, q, 0.0).astype(BF16)
        qs[order, (2 * j + 1) * blk:(2 * j + 2) * blk, :] = jnp.where(first, 0.0, q).astype(BF16)
        kk[order, j * blk:(j + 1) * blk, :] = k.astype(BF16)
        vv[order, j * blk:(j + 1) * blk, :] = v.astype(BF16)

    for j in range(nblk):
        stage(0, j, *(ref[0, j * blk:(j + 1) * blk, :] for ref in refs))
    for j in range(nblk):
        r, c = divmod(j, grp)
        x = [ref[0, pl.ds(c * blk * grp + r, blk, stride=grp), :] for ref in refs]
        for i in range(3):
            x4f[i, j * blk:(j + 1) * blk, :] = x[i]
        stage(1, j, *x)
    for j in range(nblk):
        start = (j % grp) * (seq // grp) + j // grp
        stage(2, j, *(x4f[i, pl.ds(start, blk, stride=grp), :] for i in range(3)))

    def make_bias(dil, n_steps):
        qi = lax.broadcasted_iota(jnp.int32, (blk, 2 * blk), 0)
        ki = lax.broadcasted_iota(jnp.int32, (blk, 2 * blk), 1)
        steps = qi + blk - ki
        band = (steps >= 0) & (steps <= n_steps)
        dist = (steps * dil).astype(F32) * LOG2E
        return jnp.concatenate([jnp.where(band, -slope[h] * dist, NEG_BIG) for h in range(2)], axis=0)

    ones = jnp.ones((2 * blk, LANES), BF16)

    def partial(order, off, kcat, vcat, bias):
        nk = kcat.shape[0]
        s = _dot_nt(qs[order, pl.ds(2 * off, 2 * blk), :], kcat) + bias
        m = jnp.max(s, axis=-1, keepdims=True)
        p = jnp.exp2(s - m).astype(BF16)
        r = _dot(p, jnp.concatenate([vcat, ones[:nk]], axis=1))
        return (jnp.where(first, m[:blk], m[blk:]),
                jnp.where(first, r[:blk, LANES:], r[blk:, LANES:]),
                jnp.where(first, r[:blk, :LANES], r[blk:, :LANES]))

    def merge(a, b):
        m = jnp.maximum(a[0], b[0])
        wa = jnp.exp2(a[0] - m)
        wb = jnp.exp2(b[0] - m)
        return m, a[1] * wa + b[1] * wb, a[2] * wa + b[2] * wb

    def load_part(stage_i, idx):
        return tuple(part[stage_i, i, idx, :] for i in range(3))

    def store_part(stage_i, idx, val):
        for i in range(3):
            part[stage_i, i, idx, :] = val[i]

    (w1, d1), (w4, d4), (w16, d16) = DILATED_CONFIGS
    bias16 = make_bias(d16, w16 // d16)[:, blk:]
    bias4 = make_bias(d4, w4 // d4)
    bias1 = make_bias(d1, w1 // d1)

    def group16(g, carry):
        for c in range(grp):
            off = (g * grp + c) * blk
            cur = pl.ds(off, blk)
            res = partial(2, off, kk[2, cur, :], vv[2, cur, :], bias16)
            store_part(0, pl.ds(c * (seq // grp) + g, blk, stride=grp), res)
        return carry

    for g in range(grp):
        group16(g, 0)

    def group4(g, carry):
        k_prev = v_prev = None
        for c in range(grp):
            off = (g * grp + c) * blk
            cur = pl.ds(off, blk)
            k_cur, v_cur = kk[1, cur, :], vv[1, cur, :]
            if c == 0:
                res = partial(1, off, k_cur, v_cur, bias4[:, blk:])
            else:
                res = partial(1, off, jnp.concatenate([k_prev, k_cur], axis=0),
                              jnp.concatenate([v_prev, v_cur], axis=0), bias4)
            k_prev, v_prev = k_cur, v_cur
            res = merge(res, load_part(0, cur))
            store_part(1, pl.ds(c * blk * grp + g, blk, stride=grp), res)
        return carry

    for g in range(grp):
        group4(g, 0)

    def group1(g, carry):
        k_prev = v_prev = None
        for c in range(grp):
            off = (g * grp + c) * blk
            cur = pl.ds(off, blk)
            k_cur, v_cur = kk[0, cur, :], vv[0, cur, :]
            if g == 0 and c == 0:
                res = partial(0, off, k_cur, v_cur, bias1[:, blk:])
            else:
                if c == 0:
                    prev = pl.ds(off - blk, blk)
                    k_prev, v_prev = kk[0, prev, :], vv[0, prev, :]
                res = partial(0, off, jnp.concatenate([k_prev, k_cur], axis=0),
                              jnp.concatenate([v_prev, v_cur], axis=0), bias1)
            k_prev, v_prev = k_cur, v_cur
            _, l_tot, a_tot = merge(res, load_part(1, cur))
            o = a_tot / l_tot
            sq = o * o
            ss0 = jnp.sum(jnp.where(first, sq, 0.0), axis=-1, keepdims=True)
            ss1 = jnp.sum(jnp.where(first, 0.0, sq), axis=-1, keepdims=True)
            ms = jnp.where(first, ss0, ss1) * (1.0 / ATTN_HEAD_DIM)
            o_ref[0, cur, :] = (o * lax.rsqrt(ms + NORM_EPS) * gain_ref[...]).astype(o_ref.dtype)
        return carry

    for g in range(grp):
        group1(g, 0)


def _attention(qa, ka, va, gain, slopes):
    b, seq, aw = qa.shape
    pairs = aw // LANES
    nbr = len(DILATED_CONFIGS)
    blk = pl.BlockSpec((1, seq, LANES), lambda i, p: (i, 0, p))
    return pl.pallas_call(
        _attn_kernel,
        grid=(b, pairs),
        in_specs=[pl.BlockSpec(memory_space=pltpu.SMEM), blk, blk, blk,
                  pl.BlockSpec((1, LANES), lambda i, p: (0, p))],
        out_specs=blk,
        out_shape=jax.ShapeDtypeStruct((b, seq, aw), BF16),
        scratch_shapes=[pltpu.VMEM((nbr, 2 * seq, LANES), BF16),
                        pltpu.VMEM((nbr, seq, LANES), BF16),
                        pltpu.VMEM((nbr, seq, LANES), BF16),
                        pltpu.VMEM((3, seq, LANES), F32),
                        pltpu.VMEM((2, 3, seq, LANES), F32)],
        compiler_params=pltpu.CompilerParams(dimension_semantics=("parallel", "parallel"),
                                             vmem_limit_bytes=VMEM_LIMIT),
        name="dilated_attention",
    )(slopes, qa, ka, va, gain.reshape(1, aw))


def _split2(x):
    h1 = x.astype(BF16).astype(F32)
    return h1, (x - h1).astype(BF16).astype(F32)


def _mlstm_kernel(q_ref, k_ref, v_ref, og_ref, g_ref, gain_ref, o_ref, rowq, colq, lhs_s, guard_s, upd_s):
    chunk = MLSTM_CHUNK
    seq, dh = q_ref.shape[1], q_ref.shape[2]
    nc = seq // chunk
    assert chunk == LANES and dh == LANES

    ti = lax.broadcasted_iota(jnp.int32, (chunk, chunk), 0)
    si = lax.broadcasted_iota(jnp.int32, (chunk, chunk), 1)
    causal = si <= ti
    eye = (si == ti).astype(BF16)
    eye2 = jnp.concatenate([eye, eye], axis=1)
    tri_t = (ti <= si).astype(BF16)
    lane = lax.broadcasted_iota(jnp.int32, (nc, chunk), 1)
    row = lax.broadcasted_iota(jnp.int32, (nc, chunk), 0)

    i_pre = g_ref[0, 0, 0] * LOG2E
    f_pre = g_ref[0, 0, 1]
    lf = (jnp.minimum(f_pre, 0.0) - jnp.log1p(jnp.exp(-jnp.abs(f_pre)))) * LOG2E
    b = sum(_dot(part, tri_t) for part in _split3(lf))
    u = i_pre - b
    cm = u
    span = 1
    while span < chunk:
        nxt = min(span * 8, chunk)
        terms = [jnp.where(lane >= sh, pltpu.roll(cm, sh, axis=1), cm) for sh in range(span, nxt, span)]
        cm = functools.reduce(jnp.maximum, terms, cm)
        span = nxt
    b_last = jnp.broadcast_to(b[:, chunk - 1:chunk], (nc, chunk))
    u_max = jnp.broadcast_to(cm[:, chunk - 1:chunk], (nc, chunk))
    m = jnp.zeros((1, chunk), F32)
    m_start = jnp.zeros((nc, chunk), F32)
    for c in range(nc - 1):
        m = b_last[c:c + 1, :] + jnp.maximum(m, u_max[c:c + 1, :])
        m_start = jnp.where(row == c + 1, m, m_start)
    m_next = b_last + jnp.maximum(m_start, u_max)
    z1, z2 = _split2(jnp.maximum(m_start, cm))
    z = z1 + z2
    r1, r2 = _split2(b + z)
    rowq[0] = u
    rowq[1] = m_start
    rowq[2] = jnp.exp2(b_last + u - m_next)
    rowq[3] = jnp.exp2(b_last + m_start - m_next)
    colq[0] = jnp.concatenate([z1, z2], axis=1)
    colq[1] = jnp.concatenate([r1, r2], axis=1)

    ones = jnp.ones((chunk, dh), BF16)

    def independent(c, carry):
        rows = pl.ds(pl.multiple_of(c * chunk, chunk), chunk)
        qb, kb = q_ref[0, rows, :], k_ref[0, rows, :]
        v_ext = jnp.concatenate([v_ref[0, rows, :], ones], axis=1)
        tiles = []
        for j in range(2):
            v = jnp.broadcast_to(colq[j, pl.ds(c, 1), :], (BF16_ROWS, 2 * chunk)).astype(BF16)
            tiles.extend([v] * (chunk // BF16_ROWS))
        colf = _dot_nt(eye2, jnp.concatenate(tiles, axis=0))
        z_col, mrow_col = colf[:, :chunk], colf[:, chunk:]
        dw = jnp.where(causal, jnp.exp2(rowq[0, pl.ds(c, 1), :] - z_col), 0.0)
        gw = jnp.exp2(rowq[1, pl.ds(c, 1), :] - z_col)
        sc = _dot_nt(qb, kb) * dw
        lhs_s[rows, :] = jnp.concatenate([(gw * qb.astype(F32)).astype(BF16), sc.astype(BF16)], axis=1)
        guard_s[rows, :] = jnp.exp2(-mrow_col)
        wk_t = (kb.astype(F32).T * rowq[2, pl.ds(c, 1), :]).astype(BF16)
        upd_s[rows, :] = _dot(wk_t, v_ext)
        return carry

    lax.fori_loop(0, nc, independent, 0, unroll=8)

    def recurrent(c, state):
        rows = pl.ds(pl.multiple_of(c * chunk, chunk), chunk)
        v_ext = jnp.concatenate([v_ref[0, rows, :], ones], axis=1)
        nd = _dot(lhs_s[rows, :], jnp.concatenate([state.astype(BF16), v_ext], axis=0))
        hidden = nd[:, :dh] / jnp.maximum(jnp.abs(nd[:, dh:]), guard_s[rows, :])
        gated = og_ref[0, rows, :].astype(F32) * hidden
        o_ref[0, rows, :] = _rmsnorm(gated, gain_ref[...]).astype(o_ref.dtype)
        dec = rowq[3, pl.ds(c, 1), :]
        return jnp.concatenate([dec, dec], axis=1) * state + upd_s[rows, :]

    lax.fori_loop(0, nc, recurrent, jnp.zeros((dh, 2 * dh), F32), unroll=8)


def _mlstm(qm, km, vm, om, gates, gain):
    b, seq, mw = qm.shape
    dh = MLSTM_HEAD_DIM
    nc = seq // MLSTM_CHUNK
    blk = pl.BlockSpec((1, seq, dh), lambda i, h: (i, 0, h))
    return pl.pallas_call(
        _mlstm_kernel,
        grid=(b, mw // dh),
        in_specs=[blk, blk, blk, blk,
                  pl.BlockSpec((1, 1, 2, nc, MLSTM_CHUNK), lambda i, h: (i, h, 0, 0, 0)),
                  pl.BlockSpec((1, dh), lambda i, h: (0, h))],
        out_specs=blk,
        out_shape=jax.ShapeDtypeStruct((b, seq, mw), BF16),
        scratch_shapes=[pltpu.VMEM((4, nc, MLSTM_CHUNK), F32),
                        pltpu.VMEM((2, nc, 2 * MLSTM_CHUNK), F32),
                        pltpu.VMEM((seq, 2 * dh), BF16),
                        pltpu.VMEM((seq, dh), F32),
                        pltpu.VMEM((seq, 2 * dh), F32)],
        compiler_params=pltpu.CompilerParams(dimension_semantics=("parallel", "parallel"),
                                             vmem_limit_bytes=VMEM_LIMIT),
        name="mlstm",
    )(qm, km, vm, om, gates, gain.reshape(1, mw))


def kernel(x, ffn1_norm_w, ffn1_w_gate, ffn1_w_up, ffn1_w_down, mix_norm_w, w_in, q_norm_w, k_norm_w, conv_w,
           conv_b, i_bias, f_bias, attn_out_gain, mlstm_out_gain, w_out, ffn2_norm_w, ffn2_w_gate, ffn2_w_up,
           ffn2_w_down):
    b, seq, d = x.shape
    n = b * seq
    aw = ATTN_HEADS * ATTN_HEAD_DIM
    mw = MLSTM_HEADS * MLSTM_HEAD_DIM
    n_main = 3 * aw + 4 * mw
    hm = MLSTM_HEADS
    slopes = jnp.asarray(2.0 ** (-8.0 * np.arange(1, ATTN_HEADS + 1, dtype=np.float32) / ATTN_HEADS), F32)
    xt = x.reshape(n, d)
    for l in range(ffn1_norm_w.shape[0]):
        xt = _ffn(xt, ffn1_norm_w[l].reshape(1, d), ffn1_w_gate[l].astype(BF16), ffn1_w_up[l].astype(BF16),
                  ffn1_w_down[l].astype(BF16))
        w_l = w_in[l].astype(BF16)
        gate_bias = jnp.concatenate([i_bias[l], f_bias[l]]).astype(F32)
        qa, ka, va, qm, km, vm, om, gr = _inproj(xt, seq, mix_norm_w[l], w_l[:, :n_main], w_l[:, n_main:],
                                                 gate_bias, q_norm_w[l], k_norm_w[l], conv_w[l], conv_b[l])
        shp = lambda t: t.reshape(b, seq, -1)
        attn = _attention(shp(qa), shp(ka), shp(va), attn_out_gain[l], slopes)
        gates = gr.reshape(2, hm, b, seq // MLSTM_CHUNK, MLSTM_CHUNK).transpose(2, 1, 0, 3, 4)
        mls = _mlstm(shp(qm), shp(km), shp(vm), shp(om), gates, mlstm_out_gain[l])
        w_o = w_out[l].astype(BF16)
        xt = _ffn(xt, ffn2_norm_w[l].reshape(1, d), ffn2_w_gate[l].astype(BF16), ffn2_w_up[l].astype(BF16),
                  ffn2_w_down[l].astype(BF16),
                  mix=(attn.reshape(n, aw), mls.reshape(n, mw), w_o[:aw], w_o[aw:]))
    return xt.reshape(b, seq, d)
```

```python
import functools

import numpy as np
import jax
import jax.numpy as jnp
from jax import lax
from jax.experimental import pallas as pl
from jax.experimental.pallas import tpu as pltpu

F32 = jnp.float32
BF16 = jnp.bfloat16

LANES = 128
SUBLANES = 8
BF16_ROWS = 16
NORM_EPS = 1e-6
ATTN_HEADS = 8
ATTN_HEAD_DIM = 64
MLSTM_HEADS = 4
MLSTM_HEAD_DIM = 128
CONV_WIDTH = 4
DILATED_CONFIGS = ((128, 1), (512, 4), (2048, 16))
ATTN_BLOCK = 128
MLSTM_CHUNK = 128
NEG_BIG = -1e30
ATTN_GROUP = 4
LOG2E = 1.4426950408889634
FFN_TOKENS = 512
INPROJ_TOKENS = 512
CONV_ROWS = 64
VMEM_LIMIT = 56 * 1024 * 1024


def _const_spec(shape):
    nd = len(shape)
    return pl.BlockSpec(shape, lambda *_: (0,) * nd, pipeline_mode=pl.Buffered(1))


def _rmsnorm(x, w):
    ms = jnp.mean(x * x, axis=-1, keepdims=True)
    return x * lax.rsqrt(ms + NORM_EPS) * w


def _dot(a, b):
    return jnp.dot(a, b, preferred_element_type=F32)


def _dot_nt(a, b):
    return lax.dot_general(a, b, (((1,), (1,)), ((), ())), preferred_element_type=F32)


def _split3(x):
    h1 = x.astype(BF16)
    r1 = x - h1.astype(F32)
    h2 = r1.astype(BF16)
    h3 = (r1 - h2.astype(F32)).astype(BF16)
    return h1, h2, h3


def _swiglu_half(x, nw_ref, wg_ref, wu_ref, wd_ref):
    h = _rmsnorm(x, nw_ref[...]).astype(BF16)
    g = _dot(h, wg_ref[...])
    u = _dot(h, wu_ref[...])
    a = (g * jax.nn.sigmoid(g) * u).astype(BF16)
    return x + 0.5 * _dot(a, wd_ref[...])


def _ffn_kernel(x_ref, nw_ref, wg_ref, wu_ref, wd_ref, o_ref):
    o_ref[...] = _swiglu_half(x_ref[...], nw_ref, wg_ref, wu_ref, wd_ref)


def _outproj_ffn_kernel(x_ref, ya_ref, ym_ref, woa_ref, wom_ref, nw_ref, wg_ref, wu_ref, wd_ref, o_ref):
    x = x_ref[...] + _dot(ya_ref[...], woa_ref[...]) + _dot(ym_ref[...], wom_ref[...])
    o_ref[...] = _swiglu_half(x, nw_ref, wg_ref, wu_ref, wd_ref)


def _ffn(x, nw, wg, wu, wd, mix=None):
    n, d = x.shape
    f = wg.shape[1]
    tm = FFN_TOKENS
    tok = lambda w: pl.BlockSpec((tm, w), lambda i: (i, 0))
    w_specs = [_const_spec((1, d)), _const_spec((d, f)), _const_spec((d, f)), _const_spec((f, d))]
    if mix is None:
        body, ins, specs = _ffn_kernel, (x,), [tok(d)]
    else:
        ya, ym, woa, wom = mix
        body, ins = _outproj_ffn_kernel, (x, ya, ym, woa, wom)
        specs = [tok(d), tok(ya.shape[1]), tok(ym.shape[1]), _const_spec(woa.shape), _const_spec(wom.shape)]
    return pl.pallas_call(
        body,
        grid=(n // tm,),
        in_specs=specs + w_specs,
        out_specs=tok(d),
        out_shape=jax.ShapeDtypeStruct((n, d), F32),
        compiler_params=pltpu.CompilerParams(dimension_semantics=("parallel",), vmem_limit_bytes=VMEM_LIMIT),
        name="ffn" if mix is None else "outproj_ffn",
    )(*ins, nw, wg, wu, wd)


def _inproj_kernel(tiles_per_seq, x_ref, nw_ref, w_ref, wgr_ref, gbr_ref, qnw_ref, knw_ref, cw_ref, cb_ref,
                   qa_ref, ka_ref, va_ref, qm_ref, km_ref, vm_ref, om_ref, gr_ref, cpad):
    tm = x_ref.shape[0]
    aw = qa_ref.shape[1]
    mw = qm_ref.shape[1]
    h = _rmsnorm(x_ref[...], nw_ref[...]).astype(BF16)
    first = lax.broadcasted_iota(jnp.int32, (1, LANES), 1) < ATTN_HEAD_DIM

    def head_norm(dst, p, w_row, scale):
        for c in range(0, aw, LANES):
            ps = p[:, c:c + LANES]
            sq = ps * ps
            ss0 = jnp.sum(jnp.where(first, sq, 0.0), axis=-1, keepdims=True)
            ss1 = jnp.sum(jnp.where(first, 0.0, sq), axis=-1, keepdims=True)
            ms = jnp.where(first, ss0, ss1) * (1.0 / ATTN_HEAD_DIM)
            dst[:, c:c + LANES] = ps * lax.rsqrt(ms + NORM_EPS) * (w_row[:, c:c + LANES] * scale)

    head_norm(qa_ref, _dot(h, w_ref[:, 0:aw]), qnw_ref[...], ATTN_HEAD_DIM ** -0.5 * LOG2E)
    head_norm(ka_ref, _dot(h, w_ref[:, aw:2 * aw]), knw_ref[...], 1.0)
    va_ref[...] = _dot(h, w_ref[:, 2 * aw:3 * aw])

    base = 3 * aw
    hist = CONV_WIDTH - 1

    nslab = 2 * mw // LANES

    @pl.when(pl.program_id(0) % tiles_per_seq == 0)
    def _():
        cpad[:, 0:SUBLANES, :] = jnp.zeros((nslab, SUBLANES, LANES), F32)

    raw = _dot(h, w_ref[:, base:base + 2 * mw])
    row0 = jnp.minimum(pl.program_id(0), 0) + (SUBLANES - hist)
    for s in range(nslab):
        lanes = slice(s * LANES, (s + 1) * LANES)
        cpad[s, SUBLANES:, :] = raw[:, lanes]
        for r in range(0, tm, CONV_ROWS):
            y = cb_ref[:, lanes]
            for j in range(CONV_WIDTH):
                y = y + cpad[s, pl.ds(row0 + r + j, CONV_ROWS), :] * cw_ref[j:j + 1, lanes]
            y = y * jax.nn.sigmoid(y)
            if s * LANES < mw:
                qm_ref[r:r + CONV_ROWS, lanes] = y.astype(BF16)
            else:
                y = y * (MLSTM_HEAD_DIM ** -0.5)
                km_ref[r:r + CONV_ROWS, s * LANES - mw:(s + 1) * LANES - mw] = y.astype(BF16)
        cpad[s, 0:SUBLANES, :] = cpad[s, tm:tm + SUBLANES, :]

    vm_ref[...] = _dot(h, w_ref[:, base + 2 * mw:base + 3 * mw]).astype(BF16)
    om_ref[...] = jax.nn.sigmoid(_dot(h, w_ref[:, base + 3 * mw:base + 4 * mw])).astype(BF16)
    ng = gr_ref.shape[0]
    gr_ref[...] = _dot_nt(wgr_ref[...], h)[:ng, :] + gbr_ref[...]


def _inproj(x, seq, nw, w_main, w_gate, gate_bias, qnw, knw, conv_w, conv_b):
    n, d = x.shape
    tm = INPROJ_TOKENS
    aw = ATTN_HEADS * ATTN_HEAD_DIM
    mw = MLSTM_HEADS * MLSTM_HEAD_DIM
    ng = 2 * MLSTM_HEADS
    wgr = jnp.zeros((BF16_ROWS, d), BF16).at[:ng, :].set(w_gate.T)
    gbr = gate_bias.reshape(ng, 1)
    qnw_t = jnp.tile(qnw, ATTN_HEADS).reshape(1, aw)
    knw_t = jnp.tile(knw, ATTN_HEADS).reshape(1, aw)
    tok = lambda w: pl.BlockSpec((tm, w), lambda i: (i, 0))
    out_shape = ([jax.ShapeDtypeStruct((n, aw), F32)] * 3 + [jax.ShapeDtypeStruct((n, mw), BF16)] * 4
                 + [jax.ShapeDtypeStruct((ng, n), F32)])
    out_specs = [tok(aw)] * 3 + [tok(mw)] * 4 + [pl.BlockSpec((ng, tm), lambda i: (0, i))]
    return pl.pallas_call(
        functools.partial(_inproj_kernel, seq // tm),
        grid=(n // tm,),
        in_specs=[tok(d), _const_spec((1, d)), _const_spec(w_main.shape), _const_spec(wgr.shape),
                  _const_spec(gbr.shape), _const_spec((1, aw)), _const_spec((1, aw)),
                  _const_spec(conv_w.shape), _const_spec((1, 2 * mw))],
        out_specs=out_specs,
        out_shape=out_shape,
        scratch_shapes=[pltpu.VMEM((2 * mw // LANES, tm + SUBLANES, LANES), F32)],
        compiler_params=pltpu.CompilerParams(dimension_semantics=("arbitrary",), vmem_limit_bytes=VMEM_LIMIT),
        name="in_proj",
    )(x, nw.reshape(1, d), w_main, wgr, gbr, qnw_t, knw_t, conv_w, conv_b.reshape(1, 2 * mw))


def _attn_kernel(slopes_ref, q_ref, k_ref, v_ref, gain_ref, o_ref, qs, kk, vv, x4f, part):
    blk = ATTN_BLOCK
    seq = q_ref.shape[1]
    nblk = seq // blk
    grp = ATTN_GROUP
    assert [d for _, d in DILATED_CONFIGS] == [1, grp, grp * grp] and nblk == grp * grp
    pair = pl.program_id(1)
    lane = lax.broadcasted_iota(jnp.int32, (blk, LANES), 1)
    first = lane < ATTN_HEAD_DIM
    slope = (slopes_ref[2 * pair], slopes_ref[2 * pair + 1])
    refs = (q_ref, k_ref, v_ref)

    def stage(order, j, q, k, v):
        qs[order, 2 * j * blk:(2 * j + 1) * blk, :] = jnp.where(first, q, 0.0).astype(BF16)
        qs[order, (2 * j + 1) * blk:(2 * j + 2) * blk, :] = jnp.where(first, 0.0, q).astype(BF16)
        kk[order, j * blk:(j + 1) * blk, :] = k.astype(BF16)
        vv[order, j * blk:(j + 1) * blk, :] = v.astype(BF16)

    for j in range(nblk):
        stage(0, j, *(ref[0, j * blk:(j + 1) * blk, :] for ref in refs))
    for j in range(nblk):
        r, c = divmod(j, grp)
        x = [ref[0, pl.ds(c * blk * grp + r, blk, stride=grp), :] for ref in refs]
        for i in range(3):
            x4f[i, j * blk:(j + 1) * blk, :] = x[i]
        stage(1, j, *x)
    for j in range(nblk):
        start = (j % grp) * (seq // grp) + j // grp
        stage(2, j, *(x4f[i, pl.ds(start, blk, stride=grp), :] for i in range(3)))

    def make_bias(dil, n_steps):
        qi = lax.broadcasted_iota(jnp.int32, (blk, 2 * blk), 0)
        ki = lax.broadcasted_iota(jnp.int32, (blk, 2 * blk), 1)
        steps = qi + blk - ki
        band = (steps >= 0) & (steps <= n_steps)
        dist = (steps * dil).astype(F32) * LOG2E
        return jnp.concatenate([jnp.where(band, -slope[h] * dist, NEG_BIG) for h in range(2)], axis=0)

    ones = jnp.ones((2 * blk, LANES), BF16)

    def partial(order, off, kcat, vcat, bias):
        nk = kcat.shape[0]
        s = _dot_nt(qs[order, pl.ds(2 * off, 2 * blk), :], kcat) + bias
        m = jnp.max(s, axis=-1, keepdims=True)
        p = jnp.exp2(s - m).astype(BF16)
        r = _dot(p, jnp.concatenate([vcat, ones[:nk]], axis=1))
        return (jnp.where(first, m[:blk], m[blk:]),
                jnp.where(first, r[:blk, LANES:], r[blk:, LANES:]),
                jnp.where(first, r[:blk, :LANES], r[blk:, :LANES]))

    def merge(a, b):
        m = jnp.maximum(a[0], b[0])
        wa = jnp.exp2(a[0] - m)
        wb = jnp.exp2(b[0] - m)
        return m, a[1] * wa + b[1] * wb, a[2] * wa + b[2] * wb

    def load_part(stage_i, idx):
        return tuple(part[stage_i, i, idx, :] for i in range(3))

    def store_part(stage_i, idx, val):
        for i in range(3):
            part[stage_i, i, idx, :] = val[i]

    (w1, d1), (w4, d4), (w16, d16) = DILATED_CONFIGS
    bias16 = make_bias(d16, w16 // d16)[:, blk:]
    bias4 = make_bias(d4, w4 // d4)
    bias1 = make_bias(d1, w1 // d1)

    def group16(g, carry):
        for c in range(grp):
            off = (g * grp + c) * blk
            cur = pl.ds(off, blk)
            res = partial(2, off, kk[2, cur, :], vv[2, cur, :], bias16)
            store_part(0, pl.ds(c * (seq // grp) + g, blk, stride=grp), res)
        return carry

    for g in range(grp):
        group16(g, 0)

    def group4(g, carry):
        k_prev = v_prev = None
        for c in range(grp):
            off = (g * grp + c) * blk
            cur = pl.ds(off, blk)
            k_cur, v_cur = kk[1, cur, :], vv[1, cur, :]
            if c == 0:
                res = partial(1, off, k_cur, v_cur, bias4[:, blk:])
            else:
                res = partial(1, off, jnp.concatenate([k_prev, k_cur], axis=0),
                              jnp.concatenate([v_prev, v_cur], axis=0), bias4)
            k_prev, v_prev = k_cur, v_cur
            res = merge(res, load_part(0, cur))
            store_part(1, pl.ds(c * blk * grp + g, blk, stride=grp), res)
        return carry

    for g in range(grp):
        group4(g, 0)

    def group1(g, carry):
        k_prev = v_prev = None
        for c in range(grp):
            off = (g * grp + c) * blk
            cur = pl.ds(off, blk)
            k_cur, v_cur = kk[0, cur, :], vv[0, cur, :]
            if g == 0 and c == 0:
                res = partial(0, off, k_cur, v_cur, bias1[:, blk:])
            else:
                if c == 0:
                    prev = pl.ds(off - blk, blk)
                    k_prev, v_prev = kk[0, prev, :], vv[0, prev, :]
                res = partial(0, off, jnp.concatenate([k_prev, k_cur], axis=0),
                              jnp.concatenate([v_prev, v_cur], axis=0), bias1)
            k_prev, v_prev = k_cur, v_cur
            _, l_tot, a_tot = merge(res, load_part(1, cur))
            o = a_tot / l_tot
            sq = o * o
            ss0 = jnp.sum(jnp.where(first, sq, 0.0), axis=-1, keepdims=True)
            ss1 = jnp.sum(jnp.where(first, 0.0, sq), axis=-1, keepdims=True)
            ms = jnp.where(first, ss0, ss1) * (1.0 / ATTN_HEAD_DIM)
            o_ref[0, cur, :] = (o * lax.rsqrt(ms + NORM_EPS) * gain_ref[...]).astype(o_ref.dtype)
        return carry

    for g in range(grp):
        group1(g, 0)


def _attention(qa, ka, va, gain, slopes):
    b, seq, aw = qa.shape
    pairs = aw // LANES
    nbr = len(DILATED_CONFIGS)
    blk = pl.BlockSpec((1, seq, LANES), lambda i, p: (i, 0, p))
    return pl.pallas_call(
        _attn_kernel,
        grid=(b, pairs),
        in_specs=[pl.BlockSpec(memory_space=pltpu.SMEM), blk, blk, blk,
                  pl.BlockSpec((1, LANES), lambda i, p: (0, p))],
        out_specs=blk,
        out_shape=jax.ShapeDtypeStruct((b, seq, aw), BF16),
        scratch_shapes=[pltpu.VMEM((nbr, 2 * seq, LANES), BF16),
                        pltpu.VMEM((nbr, seq, LANES), BF16),
                        pltpu.VMEM((nbr, seq, LANES), BF16),
                        pltpu.VMEM((3, seq, LANES), F32),
                        pltpu.VMEM((2, 3, seq, LANES), F32)],
        compiler_params=pltpu.CompilerParams(dimension_semantics=("parallel", "parallel"),
                                             vmem_limit_bytes=VMEM_LIMIT),
        name="dilated_attention",
    )(slopes, qa, ka, va, gain.reshape(1, aw))


def _split2(x):
    h1 = x.astype(BF16).astype(F32)
    return h1, (x - h1).astype(BF16).astype(F32)


def _mlstm_kernel(q_ref, k_ref, v_ref, og_ref, g_ref, gain_ref, o_ref, rowq, colq, lhs_s, guard_s, upd_s):
    chunk = MLSTM_CHUNK
    seq, dh = q_ref.shape[1], q_ref.shape[2]
    nc = seq // chunk
    assert chunk == LANES and dh == LANES

    ti = lax.broadcasted_iota(jnp.int32, (chunk, chunk), 0)
    si = lax.broadcasted_iota(jnp.int32, (chunk, chunk), 1)
    causal = si <= ti
    eye = (si == ti).astype(BF16)
    eye2 = jnp.concatenate([eye, eye], axis=1)
    tri_t = (ti <= si).astype(BF16)
    lane = lax.broadcasted_iota(jnp.int32, (nc, chunk), 1)
    row = lax.broadcasted_iota(jnp.int32, (nc, chunk), 0)

    i_pre = g_ref[0, 0, 0] * LOG2E
    f_pre = g_ref[0, 0, 1]
    lf = (jnp.minimum(f_pre, 0.0) - jnp.log1p(jnp.exp(-jnp.abs(f_pre)))) * LOG2E
    b = sum(_dot(part, tri_t) for part in _split3(lf))
    u = i_pre - b
    cm = u
    span = 1
    while span < chunk:
        nxt = min(span * 8, chunk)
        terms = [jnp.where(lane >= sh, pltpu.roll(cm, sh, axis=1), cm) for sh in range(span, nxt, span)]
        cm = functools.reduce(jnp.maximum, terms, cm)
        span = nxt
    b_last = jnp.broadcast_to(b[:, chunk - 1:chunk], (nc, chunk))
    u_max = jnp.broadcast_to(cm[:, chunk - 1:chunk], (nc, chunk))
    m = jnp.zeros((1, chunk), F32)
    m_start = jnp.zeros((nc, chunk), F32)
    for c in range(nc - 1):
        m = b_last[c:c + 1, :] + jnp.maximum(m, u_max[c:c + 1, :])
        m_start = jnp.where(row == c + 1, m, m_start)
    m_next = b_last + jnp.maximum(m_start, u_max)
    z1, z2 = _split2(jnp.maximum(m_start, cm))
    z = z1 + z2
    r1, r2 = _split2(b + z)
    rowq[0] = u
    rowq[1] = m_start
    rowq[2] = jnp.exp2(b_last + u - m_next)
    rowq[3] = jnp.exp2(b_last + m_start - m_next)
    colq[0] = jnp.concatenate([z1, z2], axis=1)
    colq[1] = jnp.concatenate([r1, r2], axis=1)

    ones = jnp.ones((chunk, dh), BF16)

    def independent(c, carry):
        rows = pl.ds(pl.multiple_of(c * chunk, chunk), chunk)
        qb, kb = q_ref[0, rows, :], k_ref[0, rows, :]
        v_ext = jnp.concatenate([v_ref[0, rows, :], ones], axis=1)
        tiles = []
        for j in range(2):
            v = jnp.broadcast_to(colq[j, pl.ds(c, 1), :], (BF16_ROWS, 2 * chunk)).astype(BF16)
            tiles.extend([v] * (chunk // BF16_ROWS))
        colf = _dot_nt(eye2, jnp.concatenate(tiles, axis=0))
        z_col, mrow_col = colf[:, :chunk], colf[:, chunk:]
        dw = jnp.where(causal, jnp.exp2(rowq[0, pl.ds(c, 1), :] - z_col), 0.0)
        gw = jnp.exp2(rowq[1, pl.ds(c, 1), :] - z_col)
        sc = _dot_nt(qb, kb) * dw
        lhs_s[rows, :] = jnp.concatenate([(gw * qb.astype(F32)).astype(BF16), sc.astype(BF16)], axis=1)
        guard_s[rows, :] = jnp.exp2(-mrow_col)
        wk_t = (kb.astype(F32).T * rowq[2, pl.ds(c, 1), :]).astype(BF16)
        upd_s[rows, :] = _dot(wk_t, v_ext)
        return carry

    lax.fori_loop(0, nc, independent, 0, unroll=8)

    def recurrent(c, state):
        rows = pl.ds(pl.multiple_of(c * chunk, chunk), chunk)
        v_ext = jnp.concatenate([v_ref[0, rows, :], ones], axis=1)
        nd = _dot(lhs_s[rows, :], jnp.concatenate([state.astype(BF16), v_ext], axis=0))
        hidden = nd[:, :dh] / jnp.maximum(jnp.abs(nd[:, dh:]), guard_s[rows, :])
        gated = og_ref[0, rows, :].astype(F32) * hidden
        o_ref[0, rows, :] = _rmsnorm(gated, gain_ref[...]).astype(o_ref.dtype)
        dec = rowq[3, pl.ds(c, 1), :]
        return jnp.concatenate([dec, dec], axis=1) * state + upd_s[rows, :]

    lax.fori_loop(0, nc, recurrent, jnp.zeros((dh, 2 * dh), F32), unroll=8)


def _mlstm(qm, km, vm, om, gates, gain):
    b, seq, mw = qm.shape
    dh = MLSTM_HEAD_DIM
    nc = seq // MLSTM_CHUNK
    blk = pl.BlockSpec((1, seq, dh), lambda i, h: (i, 0, h))
    return pl.pallas_call(
        _mlstm_kernel,
        grid=(b, mw // dh),
        in_specs=[blk, blk, blk, blk,
                  pl.BlockSpec((1, 1, 2, nc, MLSTM_CHUNK), lambda i, h: (i, h, 0, 0, 0)),
                  pl.BlockSpec((1, dh), lambda i, h: (0, h))],
        out_specs=blk,
        out_shape=jax.ShapeDtypeStruct((b, seq, mw), BF16),
        scratch_shapes=[pltpu.VMEM((4, nc, MLSTM_CHUNK), F32),
                        pltpu.VMEM((2, nc, 2 * MLSTM_CHUNK), F32),
                        pltpu.VMEM((seq, 2 * dh), BF16),
                        pltpu.VMEM((seq, dh), F32),
                        pltpu.VMEM((seq, 2 * dh), F32)],
        compiler_params=pltpu.CompilerParams(dimension_semantics=("parallel", "parallel"),
                                             vmem_limit_bytes=VMEM_LIMIT),
        name="mlstm",
    )(qm, km, vm, om, gates, gain.reshape(1, mw))


def kernel(x, ffn1_norm_w, ffn1_w_gate, ffn1_w_up, ffn1_w_down, mix_norm_w, w_in, q_norm_w, k_norm_w, conv_w,
           conv_b, i_bias, f_bias, attn_out_gain, mlstm_out_gain, w_out, ffn2_norm_w, ffn2_w_gate, ffn2_w_up,
           ffn2_w_down):
    b, seq, d = x.shape
    n = b * seq
    aw = ATTN_HEADS * ATTN_HEAD_DIM
    mw = MLSTM_HEADS * MLSTM_HEAD_DIM
    n_main = 3 * aw + 4 * mw
    hm = MLSTM_HEADS
    slopes = jnp.asarray(2.0 ** (-8.0 * np.arange(1, ATTN_HEADS + 1, dtype=np.float32) / ATTN_HEADS), F32)
    xt = x.reshape(n, d)
    for l in range(ffn1_norm_w.shape[0]):
        xt = _ffn(xt, ffn1_norm_w[l].reshape(1, d), ffn1_w_gate[l].astype(BF16), ffn1_w_up[l].astype(BF16),
                  ffn1_w_down[l].astype(BF16))
        w_l = w_in[l].astype(BF16)
        gate_bias = jnp.concatenate([i_bias[l], f_bias[l]]).astype(F32)
        qa, ka, va, qm, km, vm, om, gr = _inproj(xt, seq, mix_norm_w[l], w_l[:, :n_main], w_l[:, n_main:],
                                                 gate_bias, q_norm_w[l], k_norm_w[l], conv_w[l], conv_b[l])
        shp = lambda t: t.reshape(b, seq, -1)
        attn = _attention(shp(qa), shp(ka), shp(va), attn_out_gain[l], slopes)
        gates = gr.reshape(2, hm, b, seq // MLSTM_CHUNK, MLSTM_CHUNK).transpose(2, 1, 0, 3, 4)
        mls = _mlstm(shp(qm), shp(km), shp(vm), shp(om), gates, mlstm_out_gain[l])
        w_o = w_out[l].astype(BF16)
        xt = _ffn(xt, ffn2_norm_w[l].reshape(1, d), ffn2_w_gate[l].astype(BF16), ffn2_w_up[l].astype(BF16),
                  ffn2_w_down[l].astype(BF16),
                  mix=(attn.reshape(n, aw), mls.reshape(n, mw), w_o[:aw], w_o[aw:]))
    return xt.reshape(b, seq, d)
```

```python
import functools

import numpy as np
import jax
import jax.numpy as jnp
from jax import lax
from jax.experimental import pallas as pl
from jax.experimental.pallas import tpu as pltpu

F32 = jnp.float32
BF16 = jnp.bfloat16

LANES = 128
SUBLANES = 8
BF16_ROWS = 16
NORM_EPS = 1e-6
ATTN_HEADS = 8
ATTN_HEAD_DIM = 64
MLSTM_HEADS = 4
MLSTM_HEAD_DIM = 128
CONV_WIDTH = 4
DILATED_CONFIGS = ((128, 1), (512, 4), (2048, 16))
ATTN_BLOCK = 128
MLSTM_CHUNK = 128
NEG_BIG = -1e30
ATTN_GROUP = 4
LOG2E = 1.4426950408889634
FFN_TOKENS = 512
INPROJ_TOKENS = 512
CONV_ROWS = 64
VMEM_LIMIT = 56 * 1024 * 1024


def _const_spec(shape):
    nd = len(shape)
    return pl.BlockSpec(shape, lambda *_: (0,) * nd, pipeline_mode=pl.Buffered(1))


def _rmsnorm(x, w):
    ms = jnp.mean(x * x, axis=-1, keepdims=True)
    return x * lax.rsqrt(ms + NORM_EPS) * w


def _dot(a, b):
    return jnp.dot(a, b, preferred_element_type=F32)


def _dot_nt(a, b):
    return lax.dot_general(a, b, (((1,), (1,)), ((), ())), preferred_element_type=F32)


def _split3(x):
    h1 = x.astype(BF16)
    r1 = x - h1.astype(F32)
    h2 = r1.astype(BF16)
    h3 = (r1 - h2.astype(F32)).astype(BF16)
    return h1, h2, h3


def _swiglu_half(x, nw_ref, wg_ref, wu_ref, wd_ref):
    h = _rmsnorm(x, nw_ref[...]).astype(BF16)
    g = _dot(h, wg_ref[...])
    u = _dot(h, wu_ref[...])
    a = (g * jax.nn.sigmoid(g) * u).astype(BF16)
    return x + 0.5 * _dot(a, wd_ref[...])


def _ffn_kernel(x_ref, nw_ref, wg_ref, wu_ref, wd_ref, o_ref):
    o_ref[...] = _swiglu_half(x_ref[...], nw_ref, wg_ref, wu_ref, wd_ref)


def _outproj_ffn_kernel(x_ref, ya_ref, ym_ref, woa_ref, wom_ref, nw_ref, wg_ref, wu_ref, wd_ref, o_ref):
    x = x_ref[...] + _dot(ya_ref[...], woa_ref[...]) + _dot(ym_ref[...], wom_ref[...])
    o_ref[...] = _swiglu_half(x, nw_ref, wg_ref, wu_ref, wd_ref)


def _ffn(x, nw, wg, wu, wd, mix=None):
    n, d = x.shape
    f = wg.shape[1]
    tm = FFN_TOKENS
    tok = lambda w: pl.BlockSpec((tm, w), lambda i: (i, 0))
    w_specs = [_const_spec((1, d)), _const_spec((d, f)), _const_spec((d, f)), _const_spec((f, d))]
    if mix is None:
        body, ins, specs = _ffn_kernel, (x,), [tok(d)]
    else:
        ya, ym, woa, wom = mix
        body, ins = _outproj_ffn_kernel, (x, ya, ym, woa, wom)
        specs = [tok(d), tok(ya.shape[1]), tok(ym.shape[1]), _const_spec(woa.shape), _const_spec(wom.shape)]
    return pl.pallas_call(
        body,
        grid=(n // tm,),
        in_specs=specs + w_specs,
        out_specs=tok(d),
        out_shape=jax.ShapeDtypeStruct((n, d), F32),
        compiler_params=pltpu.CompilerParams(dimension_semantics=("parallel",), vmem_limit_bytes=VMEM_LIMIT),
        name="ffn" if mix is None else "outproj_ffn",
    )(*ins, nw, wg, wu, wd)


def _inproj_kernel(tiles_per_seq, x_ref, nw_ref, w_ref, wgr_ref, gbr_ref, qnw_ref, knw_ref, cw_ref, cb_ref,
                   qa_ref, ka_ref, va_ref, qm_ref, km_ref, vm_ref, om_ref, gr_ref, *cpad):
    tm = x_ref.shape[0]
    aw = qa_ref.shape[0] * LANES
    mw = qm_ref.shape[1]
    h = _rmsnorm(x_ref[...], nw_ref[...]).astype(BF16)
    first = lax.broadcasted_iota(jnp.int32, (1, LANES), 1) < ATTN_HEAD_DIM

    base = 3 * aw
    hist = CONV_WIDTH - 1
    nslab = len(cpad)

    @pl.when(pl.program_id(0) % tiles_per_seq == 0)
    def _():
        for buf in cpad:
            buf[0:SUBLANES, :] = jnp.zeros((SUBLANES, LANES), F32)

    def raw_pair(s):
        raw = _dot(h, w_ref[:, base + s * LANES:base + (s + 2) * LANES])
        cpad[s][SUBLANES:, :] = raw[:, :LANES]
        cpad[s + 1][SUBLANES:, :] = raw[:, LANES:]

    row0 = jnp.minimum(pl.program_id(0), 0) + (SUBLANES - hist)

    def conv_slab(s):
        lanes = slice(s * LANES, (s + 1) * LANES)
        for r in range(0, tm, CONV_ROWS):
            y = cb_ref[:, lanes]
            for j in range(CONV_WIDTH):
                y = y + cpad[s][pl.ds(row0 + r + j, CONV_ROWS), :] * cw_ref[j:j + 1, lanes]
            y = y * jax.nn.sigmoid(y)
            if s * LANES < mw:
                qm_ref[r:r + CONV_ROWS, lanes] = y.astype(BF16)
            else:
                y = y * (MLSTM_HEAD_DIM ** -0.5)
                km_ref[r:r + CONV_ROWS, s * LANES - mw:(s + 1) * LANES - mw] = y.astype(BF16)
        cpad[s][0:SUBLANES, :] = cpad[s][tm:tm + SUBLANES, :]

    def head_norm(dst, p, w_row, scale):
        for s in range(aw // LANES):
            lanes = slice(s * LANES, (s + 1) * LANES)
            ps = p[:, lanes]
            sq = ps * ps
            ss0 = jnp.sum(jnp.where(first, sq, 0.0), axis=-1, keepdims=True)
            ss1 = jnp.sum(jnp.where(first, 0.0, sq), axis=-1, keepdims=True)
            ms = jnp.where(first, ss0, ss1) * (1.0 / ATTN_HEAD_DIM)
            dst[s] = (ps * lax.rsqrt(ms + NORM_EPS) * (w_row[:, lanes] * scale)).astype(BF16)

    head_norm(qa_ref, _dot(h, w_ref[:, 0:aw]), qnw_ref[...], ATTN_HEAD_DIM ** -0.5 * LOG2E)
    head_norm(ka_ref, _dot(h, w_ref[:, aw:2 * aw]), knw_ref[...], 1.0)
    va = _dot(h, w_ref[:, 2 * aw:3 * aw])
    for s in range(aw // LANES):
        va_ref[s] = va[:, s * LANES:(s + 1) * LANES].astype(BF16)
    for s in range(0, nslab, 2):
        raw_pair(s)
    for s in range(nslab):
        conv_slab(s)
    vm_ref[...] = _dot(h, w_ref[:, base + 2 * mw:base + 3 * mw]).astype(BF16)
    om_ref[...] = jax.nn.sigmoid(_dot(h, w_ref[:, base + 3 * mw:base + 4 * mw])).astype(BF16)
    ng = gr_ref.shape[0]
    gr_ref[...] = _dot_nt(wgr_ref[...], h)[:ng, :] + gbr_ref[...]


def _inproj(x, seq, nw, w_main, w_gate, gate_bias, qnw, knw, conv_w, conv_b):
    n, d = x.shape
    tm = INPROJ_TOKENS
    aw = ATTN_HEADS * ATTN_HEAD_DIM
    mw = MLSTM_HEADS * MLSTM_HEAD_DIM
    ng = 2 * MLSTM_HEADS
    wgr = jnp.zeros((BF16_ROWS, d), BF16).at[:ng, :].set(w_gate.T)
    gbr = gate_bias.reshape(ng, 1)
    qnw_t = jnp.tile(qnw, ATTN_HEADS).reshape(1, aw)
    knw_t = jnp.tile(knw, ATTN_HEADS).reshape(1, aw)
    tok = lambda w: pl.BlockSpec((tm, w), lambda i: (i, 0))
    pairs = aw // LANES
    out_shape = ([jax.ShapeDtypeStruct((pairs, n, LANES), BF16)] * 3 + [jax.ShapeDtypeStruct((n, mw), BF16)] * 4
                 + [jax.ShapeDtypeStruct((ng, n), F32)])
    out_specs = ([pl.BlockSpec((pairs, tm, LANES), lambda i: (0, i, 0))] * 3 + [tok(mw)] * 4
                 + [pl.BlockSpec((ng, tm), lambda i: (0, i))])
    return pl.pallas_call(
        functools.partial(_inproj_kernel, seq // tm),
        grid=(n // tm,),
        in_specs=[tok(d), _const_spec((1, d)), _const_spec(w_main.shape), _const_spec(wgr.shape),
                  _const_spec(gbr.shape), _const_spec((1, aw)), _const_spec((1, aw)),
                  _const_spec(conv_w.shape), _const_spec((1, 2 * mw))],
        out_specs=out_specs,
        out_shape=out_shape,
        scratch_shapes=[pltpu.VMEM((tm + SUBLANES, LANES), F32)] * (2 * mw // LANES),
        compiler_params=pltpu.CompilerParams(dimension_semantics=("arbitrary",), vmem_limit_bytes=VMEM_LIMIT),
        name="in_proj",
    )(x, nw.reshape(1, d), w_main, wgr, gbr, qnw_t, knw_t, conv_w, conv_b.reshape(1, 2 * mw))


def _attn_kernel(slopes_ref, q1, k1, v1, q4, k4, v4, q16, k16, v16, gain_ref, o_ref, part):
    blk = ATTN_BLOCK
    seq = q1.shape[2]
    nblk = seq // blk
    grp = ATTN_GROUP
    assert [d for _, d in DILATED_CONFIGS] == [1, grp, grp * grp] and nblk == grp * grp
    pair = pl.program_id(1)
    lane = lax.broadcasted_iota(jnp.int32, (blk, LANES), 1)
    first = lane < ATTN_HEAD_DIM
    head_mask = (first.astype(BF16), (~first).astype(BF16))
    slope = (slopes_ref[2 * pair], slopes_ref[2 * pair + 1])
    operands = ((q1, k1, v1), (q4, k4, v4), (q16, k16, v16))

    def block(order, which, j):
        ref = operands[order][which]
        if order == 0:
            return ref[0, 0, j * blk:(j + 1) * blk, :]
        if order == 1:
            r, c = divmod(j, grp)
            return ref[0, 0, c * blk:(c + 1) * blk, r * LANES:(r + 1) * LANES]
        return ref[0, 0, :, j * LANES:(j + 1) * LANES]

    def make_bias(dil, n_steps):
        qi = lax.broadcasted_iota(jnp.int32, (blk, 2 * blk), 0)
        ki = lax.broadcasted_iota(jnp.int32, (blk, 2 * blk), 1)
        steps = qi + blk - ki
        band = (steps >= 0) & (steps <= n_steps)
        dist = (steps * dil).astype(F32) * LOG2E
        return jnp.concatenate([jnp.where(band, -slope[h] * dist, NEG_BIG) for h in range(2)], axis=0)

    ones = jnp.ones((2 * blk, LANES), BF16)

    def partial(order, j, kcat, vcat, bias):
        nk = kcat.shape[0]
        q = block(order, 0, j)
        qs = jnp.concatenate([q * head_mask[0], q * head_mask[1]], axis=0)
        s = _dot_nt(qs, kcat) + bias
        m = jnp.max(s, axis=-1, keepdims=True)
        p = jnp.exp2(s - m).astype(BF16)
        r = _dot(p, jnp.concatenate([vcat, ones[:nk]], axis=1))
        return (jnp.where(first, m[:blk], m[blk:]),
                jnp.where(first, r[:blk, LANES:], r[blk:, LANES:]),
                jnp.where(first, r[:blk, :LANES], r[blk:, :LANES]))

    def merge(a, b):
        m = jnp.maximum(a[0], b[0])
        wa = jnp.exp2(a[0] - m)
        wb = jnp.exp2(b[0] - m)
        return m, a[1] * wa + b[1] * wb, a[2] * wa + b[2] * wb

    def load_part(stage_i, idx):
        return tuple(part[stage_i, i, idx, :] for i in range(3))

    def store_part(stage_i, idx, val):
        for i in range(3):
            part[stage_i, i, idx, :] = val[i]

    (w1, d1), (w4, d4), (w16, d16) = DILATED_CONFIGS
    bias16 = make_bias(d16, w16 // d16)[:, blk:]
    bias4 = make_bias(d4, w4 // d4)
    bias1 = make_bias(d1, w1 // d1)

    for j in range(nblk):
        res = partial(2, j, block(2, 1, j), block(2, 2, j), bias16)
        g, c = divmod(j, grp)
        store_part(0, pl.ds(c * (seq // grp) + g, blk, stride=grp), res)

    for order, bias, n_first in ((1, bias4, grp), (0, bias1, nblk)):
        k_prev = v_prev = None
        for j in range(nblk):
            g, c = divmod(j, grp)
            cur = pl.ds(j * blk, blk)
            k_cur, v_cur = block(order, 1, j), block(order, 2, j)
            if j % n_first == 0:
                res = partial(order, j, k_cur, v_cur, bias[:, blk:])
            else:
                res = partial(order, j, jnp.concatenate([k_prev, k_cur], axis=0),
                              jnp.concatenate([v_prev, v_cur], axis=0), bias)
            k_prev, v_prev = k_cur, v_cur
            if order == 1:
                res = merge(res, load_part(0, cur))
                store_part(1, pl.ds(c * blk * grp + g, blk, stride=grp), res)
            else:
                _, l_tot, a_tot = merge(res, load_part(1, cur))
                o = a_tot / l_tot
                sq = o * o
                ss0 = jnp.sum(jnp.where(first, sq, 0.0), axis=-1, keepdims=True)
                ss1 = jnp.sum(jnp.where(first, 0.0, sq), axis=-1, keepdims=True)
                ms = jnp.where(first, ss0, ss1) * (1.0 / ATTN_HEAD_DIM)
                o_ref[0, cur, :] = (o * lax.rsqrt(ms + NORM_EPS) * gain_ref[...]).astype(o_ref.dtype)


def _attention(qa, ka, va, b, seq, gain, slopes):
    pairs = qa.shape[0]
    aw = pairs * LANES
    in_specs = [pl.BlockSpec(memory_space=pltpu.SMEM)]
    args = [slopes]
    for _, dil in DILATED_CONFIGS:
        shape = (pairs, b, seq // dil, dil * LANES)
        spec = pl.BlockSpec((1, 1) + shape[2:], lambda i, p: (p, i, 0, 0))
        for t in (qa, ka, va):
            in_specs.append(spec)
            args.append(t.reshape(shape))
    in_specs.append(pl.BlockSpec((1, LANES), lambda i, p: (0, p)))
    return pl.pallas_call(
        _attn_kernel,
        grid=(b, pairs),
        in_specs=in_specs,
        out_specs=pl.BlockSpec((1, seq, LANES), lambda i, p: (i, 0, p)),
        out_shape=jax.ShapeDtypeStruct((b, seq, aw), BF16),
        scratch_shapes=[pltpu.VMEM((2, 3, seq, LANES), F32)],
        compiler_params=pltpu.CompilerParams(dimension_semantics=("parallel", "parallel"),
                                             vmem_limit_bytes=VMEM_LIMIT),
        name="dilated_attention",
    )(*args, gain.reshape(1, aw))


def _split2(x):
    h1 = x.astype(BF16).astype(F32)
    return h1, (x - h1).astype(BF16).astype(F32)


def _mlstm_kernel(q_ref, k_ref, v_ref, og_ref, g_ref, gain_ref, o_ref, rowq, colq, lhs_s, guard_s, upd_s):
    chunk = MLSTM_CHUNK
    seq, dh = q_ref.shape[1], q_ref.shape[2]
    nc = seq // chunk
    assert chunk == LANES and dh == LANES

    ti = lax.broadcasted_iota(jnp.int32, (chunk, chunk), 0)
    si = lax.broadcasted_iota(jnp.int32, (chunk, chunk), 1)
    causal = si <= ti
    eye = (si == ti).astype(BF16)
    eye2 = jnp.concatenate([eye, eye], axis=1)
    tri_t = (ti <= si).astype(BF16)
    lane = lax.broadcasted_iota(jnp.int32, (nc, chunk), 1)
    row = lax.broadcasted_iota(jnp.int32, (nc, chunk), 0)

    i_pre = g_ref[0, 0, 0] * LOG2E
    f_pre = g_ref[0, 0, 1]
    lf = (jnp.minimum(f_pre, 0.0) - jnp.log1p(jnp.exp(-jnp.abs(f_pre)))) * LOG2E
    b = sum(_dot(part, tri_t) for part in _split3(lf))
    u = i_pre - b
    cm = u
    span = 1
    while span < chunk:
        nxt = min(span * 8, chunk)
        terms = [jnp.where(lane >= sh, pltpu.roll(cm, sh, axis=1), cm) for sh in range(span, nxt, span)]
        cm = functools.reduce(jnp.maximum, terms, cm)
        span = nxt
    b_last = jnp.broadcast_to(b[:, chunk - 1:chunk], (nc, chunk))
    u_max = jnp.broadcast_to(cm[:, chunk - 1:chunk], (nc, chunk))
    m = jnp.zeros((1, chunk), F32)
    m_start = jnp.zeros((nc, chunk), F32)
    for c in range(nc - 1):
        m = b_last[c:c + 1, :] + jnp.maximum(m, u_max[c:c + 1, :])
        m_start = jnp.where(row == c + 1, m, m_start)
    m_next = b_last + jnp.maximum(m_start, u_max)
    z1, z2 = _split2(jnp.maximum(m_start, cm))
    z = z1 + z2
    r1, r2 = _split2(b + z)
    rowq[0] = u
    rowq[1] = m_start
    rowq[2] = jnp.exp2(b_last + u - m_next)
    rowq[3] = jnp.exp2(b_last + m_start - m_next)
    colq[0] = jnp.concatenate([z1, z2], axis=1)
    colq[1] = jnp.concatenate([r1, r2], axis=1)

    ones = jnp.ones((chunk, dh), BF16)

    def independent(c, carry):
        rows = pl.ds(pl.multiple_of(c * chunk, chunk), chunk)
        qb, kb = q_ref[0, rows, :], k_ref[0, rows, :]
        v_ext = jnp.concatenate([v_ref[0, rows, :], ones], axis=1)
        tiles = []
        for j in range(2):
            v = jnp.broadcast_to(colq[j, pl.ds(c, 1), :], (BF16_ROWS, 2 * chunk)).astype(BF16)
            tiles.extend([v] * (chunk // BF16_ROWS))
        colf = _dot_nt(eye2, jnp.concatenate(tiles, axis=0))
        z_col, mrow_col = colf[:, :chunk], colf[:, chunk:]
        dw = jnp.where(causal, jnp.exp2(rowq[0, pl.ds(c, 1), :] - z_col), 0.0)
        gw = jnp.exp2(rowq[1, pl.ds(c, 1), :] - z_col)
        sc = _dot_nt(qb, kb) * dw
        lhs_s[rows, :] = jnp.concatenate([(gw * qb.astype(F32)).astype(BF16), sc.astype(BF16)], axis=1)
        guard_s[rows, :] = jnp.exp2(-mrow_col)
        wk_t = (kb.astype(F32).T * rowq[2, pl.ds(c, 1), :]).astype(BF16)
        upd_s[rows, :] = _dot(wk_t, v_ext)
        return carry

    lax.fori_loop(0, nc, independent, 0, unroll=8)

    def recurrent(c, state):
        rows = pl.ds(pl.multiple_of(c * chunk, chunk), chunk)
        v_ext = jnp.concatenate([v_ref[0, rows, :], ones], axis=1)
        nd = _dot(lhs_s[rows, :], jnp.concatenate([state.astype(BF16), v_ext], axis=0))
        hidden = nd[:, :dh] / jnp.maximum(jnp.abs(nd[:, dh:]), guard_s[rows, :])
        gated = og_ref[0, rows, :].astype(F32) * hidden
        o_ref[0, rows, :] = _rmsnorm(gated, gain_ref[...]).astype(o_ref.dtype)
        dec = rowq[3, pl.ds(c, 1), :]
        return jnp.concatenate([dec, dec], axis=1) * state + upd_s[rows, :]

    lax.fori_loop(0, nc, recurrent, jnp.zeros((dh, 2 * dh), F32), unroll=8)


def _mlstm(qm, km, vm, om, gates, gain):
    b, seq, mw = qm.shape
    dh = MLSTM_HEAD_DIM
    nc = seq // MLSTM_CHUNK
    blk = pl.BlockSpec((1, seq, dh), lambda i, h: (i, 0, h))
    return pl.pallas_call(
        _mlstm_kernel,
        grid=(b, mw // dh),
        in_specs=[blk, blk, blk, blk,
                  pl.BlockSpec((1, 1, 2, nc, MLSTM_CHUNK), lambda i, h: (i, h, 0, 0, 0)),
                  pl.BlockSpec((1, dh), lambda i, h: (0, h))],
        out_specs=blk,
        out_shape=jax.ShapeDtypeStruct((b, seq, mw), BF16),
        scratch_shapes=[pltpu.VMEM((4, nc, MLSTM_CHUNK), F32),
                        pltpu.VMEM((2, nc, 2 * MLSTM_CHUNK), F32),
                        pltpu.VMEM((seq, 2 * dh), BF16),
                        pltpu.VMEM((seq, dh), F32),
                        pltpu.VMEM((seq, 2 * dh), F32)],
        compiler_params=pltpu.CompilerParams(dimension_semantics=("parallel", "parallel"),
                                             vmem_limit_bytes=VMEM_LIMIT),
        name="mlstm",
    )(qm, km, vm, om, gates, gain.reshape(1, mw))


def kernel(x, ffn1_norm_w, ffn1_w_gate, ffn1_w_up, ffn1_w_down, mix_norm_w, w_in, q_norm_w, k_norm_w, conv_w,
           conv_b, i_bias, f_bias, attn_out_gain, mlstm_out_gain, w_out, ffn2_norm_w, ffn2_w_gate, ffn2_w_up,
           ffn2_w_down):
    b, seq, d = x.shape
    n = b * seq
    aw = ATTN_HEADS * ATTN_HEAD_DIM
    mw = MLSTM_HEADS * MLSTM_HEAD_DIM
    n_main = 3 * aw + 4 * mw
    hm = MLSTM_HEADS
    slopes = jnp.asarray(2.0 ** (-8.0 * np.arange(1, ATTN_HEADS + 1, dtype=np.float32) / ATTN_HEADS), F32)
    xt = x.reshape(n, d)
    for l in range(ffn1_norm_w.shape[0]):
        xt = _ffn(xt, ffn1_norm_w[l].reshape(1, d), ffn1_w_gate[l].astype(BF16), ffn1_w_up[l].astype(BF16),
                  ffn1_w_down[l].astype(BF16))
        w_l = w_in[l].astype(BF16)
        gate_bias = jnp.concatenate([i_bias[l], f_bias[l]]).astype(F32)
        qa, ka, va, qm, km, vm, om, gr = _inproj(xt, seq, mix_norm_w[l], w_l[:, :n_main], w_l[:, n_main:],
                                                 gate_bias, q_norm_w[l], k_norm_w[l], conv_w[l], conv_b[l])
        shp = lambda t: t.reshape(b, seq, -1)
        attn = _attention(qa, ka, va, b, seq, attn_out_gain[l], slopes)
        gates = gr.reshape(2, hm, b, seq // MLSTM_CHUNK, MLSTM_CHUNK).transpose(2, 1, 0, 3, 4)
        mls = _mlstm(shp(qm), shp(km), shp(vm), shp(om), gates, mlstm_out_gain[l])
        w_o = w_out[l].astype(BF16)
        xt = _ffn(xt, ffn2_norm_w[l].reshape(1, d), ffn2_w_gate[l].astype(BF16), ffn2_w_up[l].astype(BF16),
                  ffn2_w_down[l].astype(BF16),
                  mix=(attn.reshape(n, aw), mls.reshape(n, mw), w_o[:aw], w_o[aw:]))
    return xt.reshape(b, seq, d)
```

```python
import functools

import numpy as np
import jax
import jax.numpy as jnp
from jax import lax
from jax.experimental import pallas as pl
from jax.experimental.pallas import tpu as pltpu

F32 = jnp.float32
BF16 = jnp.bfloat16

LANES = 128
SUBLANES = 8
BF16_ROWS = 16
NORM_EPS = 1e-6
ATTN_HEADS = 8
ATTN_HEAD_DIM = 64
MLSTM_HEADS = 4
MLSTM_HEAD_DIM = 128
CONV_WIDTH = 4
DILATED_CONFIGS = ((128, 1), (512, 4), (2048, 16))
ATTN_BLOCK = 128
MLSTM_CHUNK = 128
NEG_BIG = -1e30
ATTN_GROUP = 4
LOG2E = 1.4426950408889634
FFN_TOKENS = 512
INPROJ_TOKENS = 512
CONV_ROWS = 64
VMEM_LIMIT = 56 * 1024 * 1024


def _const_spec(shape):
    nd = len(shape)
    return pl.BlockSpec(shape, lambda *_: (0,) * nd, pipeline_mode=pl.Buffered(1))


def _rmsnorm(x, w):
    ms = jnp.mean(x * x, axis=-1, keepdims=True)
    return x * lax.rsqrt(ms + NORM_EPS) * w


def _dot(a, b):
    return jnp.dot(a, b, preferred_element_type=F32)


def _dot_nt(a, b):
    return lax.dot_general(a, b, (((1,), (1,)), ((), ())), preferred_element_type=F32)


def _split3(x):
    h1 = x.astype(BF16)
    r1 = x - h1.astype(F32)
    h2 = r1.astype(BF16)
    h3 = (r1 - h2.astype(F32)).astype(BF16)
    return h1, h2, h3


def _swiglu_half(x, nw_ref, wg_ref, wu_ref, wd_ref):
    h = _rmsnorm(x, nw_ref[...]).astype(BF16)
    g = _dot(h, wg_ref[...])
    u = _dot(h, wu_ref[...])
    a = (g * jax.nn.sigmoid(g) * u).astype(BF16)
    return x + 0.5 * _dot(a, wd_ref[...])


def _ffn_kernel(x_ref, nw_ref, wg_ref, wu_ref, wd_ref, o_ref):
    o_ref[...] = _swiglu_half(x_ref[...], nw_ref, wg_ref, wu_ref, wd_ref)


def _outproj_ffn_kernel(x_ref, ya_ref, ym_ref, wo_ref, nw_ref, wg_ref, wu_ref, wd_ref, o_ref):
    aw = ya_ref.shape[1]
    x = x_ref[...] + _dot(ya_ref[...], wo_ref[0:aw, :]) + _dot(ym_ref[...], wo_ref[aw:, :])
    o_ref[...] = _swiglu_half(x, nw_ref, wg_ref, wu_ref, wd_ref)


def _ffn(x, nw, wg, wu, wd, mix=None):
    n, d = x.shape
    f = wg.shape[1]
    tm = FFN_TOKENS
    tok = lambda w: pl.BlockSpec((tm, w), lambda i: (i, 0))
    w_specs = [_const_spec((1, d)), _const_spec((d, f)), _const_spec((d, f)), _const_spec((f, d))]
    if mix is None:
        body, ins, specs = _ffn_kernel, (x,), [tok(d)]
    else:
        ya, ym, wo = mix
        body, ins = _outproj_ffn_kernel, (x, ya, ym, wo)
        specs = [tok(d), tok(ya.shape[1]), tok(ym.shape[1]), _const_spec(wo.shape)]
    return pl.pallas_call(
        body,
        grid=(n // tm,),
        in_specs=specs + w_specs,
        out_specs=tok(d),
        out_shape=jax.ShapeDtypeStruct((n, d), F32),
        compiler_params=pltpu.CompilerParams(dimension_semantics=("parallel",), vmem_limit_bytes=VMEM_LIMIT),
        name="ffn" if mix is None else "outproj_ffn",
    )(*ins, nw, wg, wu, wd)


def _inproj_kernel(tiles_per_seq, x_ref, nw_ref, w_ref, wgr_ref, gbr_ref, qnw_ref, knw_ref, cw_ref, cb_ref,
                   qa_ref, ka_ref, va_ref, qm_ref, km_ref, vm_ref, om_ref, gr_ref, cpad):
    tm = x_ref.shape[0]
    aw = qa_ref.shape[1]
    mw = qm_ref.shape[1]
    h = _rmsnorm(x_ref[...], nw_ref[...]).astype(BF16)
    first = lax.broadcasted_iota(jnp.int32, (1, LANES), 1) < ATTN_HEAD_DIM

    def head_norm(dst, p, w_row, scale):
        for c in range(0, aw, LANES):
            ps = p[:, c:c + LANES]
            sq = ps * ps
            ss0 = jnp.sum(jnp.where(first, sq, 0.0), axis=-1, keepdims=True)
            ss1 = jnp.sum(jnp.where(first, 0.0, sq), axis=-1, keepdims=True)
            ms = jnp.where(first, ss0, ss1) * (1.0 / ATTN_HEAD_DIM)
            dst[:, c:c + LANES] = ps * lax.rsqrt(ms + NORM_EPS) * (w_row[:, c:c + LANES] * scale)

    head_norm(qa_ref, _dot(h, w_ref[:, 0:aw]), qnw_ref[...], ATTN_HEAD_DIM ** -0.5 * LOG2E)
    head_norm(ka_ref, _dot(h, w_ref[:, aw:2 * aw]), knw_ref[...], 1.0)
    va_ref[...] = _dot(h, w_ref[:, 2 * aw:3 * aw])

    base = 3 * aw
    hist = CONV_WIDTH - 1

    nslab = 2 * mw // LANES

    @pl.when(pl.program_id(0) % tiles_per_seq == 0)
    def _():
        cpad[:, 0:SUBLANES, :] = jnp.zeros((nslab, SUBLANES, LANES), F32)

    raw = _dot(h, w_ref[:, base:base + 2 * mw])
    row0 = jnp.minimum(pl.program_id(0), 0) + (SUBLANES - hist)
    for s in range(nslab):
        lanes = slice(s * LANES, (s + 1) * LANES)
        cpad[s, SUBLANES:, :] = raw[:, lanes]
        for r in range(0, tm, CONV_ROWS):
            y = cb_ref[:, lanes]
            for j in range(CONV_WIDTH):
                y = y + cpad[s, pl.ds(row0 + r + j, CONV_ROWS), :] * cw_ref[j:j + 1, lanes]
            y = y * jax.nn.sigmoid(y)
            if s * LANES < mw:
                qm_ref[r:r + CONV_ROWS, lanes] = y.astype(BF16)
            else:
                y = y * (MLSTM_HEAD_DIM ** -0.5)
                km_ref[r:r + CONV_ROWS, s * LANES - mw:(s + 1) * LANES - mw] = y.astype(BF16)
        cpad[s, 0:SUBLANES, :] = cpad[s, tm:tm + SUBLANES, :]

    vm_ref[...] = _dot(h, w_ref[:, base + 2 * mw:base + 3 * mw]).astype(BF16)
    om_ref[...] = jax.nn.sigmoid(_dot(h, w_ref[:, base + 3 * mw:base + 4 * mw])).astype(BF16)
    ng = gr_ref.shape[0]
    gr_ref[...] = _dot_nt(wgr_ref[...], h)[:ng, :] + gbr_ref[...]


def _inproj(x, seq, nw, w_main, w_gate, gate_bias, qnw, knw, conv_w, conv_b):
    n, d = x.shape
    tm = INPROJ_TOKENS
    aw = ATTN_HEADS * ATTN_HEAD_DIM
    mw = MLSTM_HEADS * MLSTM_HEAD_DIM
    ng = 2 * MLSTM_HEADS
    wgr = jnp.zeros((BF16_ROWS, d), BF16).at[:ng, :].set(w_gate.T)
    gbr = gate_bias.reshape(ng, 1)
    qnw_t = jnp.tile(qnw, ATTN_HEADS).reshape(1, aw)
    knw_t = jnp.tile(knw, ATTN_HEADS).reshape(1, aw)
    tok = lambda w: pl.BlockSpec((tm, w), lambda i: (i, 0))
    out_shape = ([jax.ShapeDtypeStruct((n, aw), F32)] * 3 + [jax.ShapeDtypeStruct((n, mw), BF16)] * 4
                 + [jax.ShapeDtypeStruct((ng, n), F32)])
    out_specs = [tok(aw)] * 3 + [tok(mw)] * 4 + [pl.BlockSpec((ng, tm), lambda i: (0, i))]
    return pl.pallas_call(
        functools.partial(_inproj_kernel, seq // tm),
        grid=(n // tm,),
        in_specs=[tok(d), _const_spec((1, d)), _const_spec(w_main.shape), _const_spec(wgr.shape),
                  _const_spec(gbr.shape), _const_spec((1, aw)), _const_spec((1, aw)),
                  _const_spec(conv_w.shape), _const_spec((1, 2 * mw))],
        out_specs=out_specs,
        out_shape=out_shape,
        scratch_shapes=[pltpu.VMEM((2 * mw // LANES, tm + SUBLANES, LANES), F32)],
        compiler_params=pltpu.CompilerParams(dimension_semantics=("arbitrary",), vmem_limit_bytes=VMEM_LIMIT),
        name="in_proj",
    )(x, nw.reshape(1, d), w_main, wgr, gbr, qnw_t, knw_t, conv_w, conv_b.reshape(1, 2 * mw))


def _attn_kernel(slopes_ref, q_ref, k_ref, v_ref, gain_ref, o_ref, qs, kk, vv, x4f, part):
    blk = ATTN_BLOCK
    seq = q_ref.shape[1]
    nblk = seq // blk
    grp = ATTN_GROUP
    assert [d for _, d in DILATED_CONFIGS] == [1, grp, grp * grp] and nblk == grp * grp
    pair = pl.program_id(1)
    lane = lax.broadcasted_iota(jnp.int32, (blk, LANES), 1)
    first = lane < ATTN_HEAD_DIM
    slope = (slopes_ref[2 * pair], slopes_ref[2 * pair + 1])
    refs = (q_ref, k_ref, v_ref)

    def stage(order, j, q, k, v):
        qs[order, 2 * j * blk:(2 * j + 1) * blk, :] = jnp.where(first, q, 0.0).astype(BF16)
        qs[order, (2 * j + 1) * blk:(2 * j + 2) * blk, :] = jnp.where(first, 0.0, q).astype(BF16)
        kk[order, j * blk:(j + 1) * blk, :] = k.astype(BF16)
        vv[order, j * blk:(j + 1) * blk, :] = v.astype(BF16)

    for j in range(nblk):
        stage(0, j, *(ref[0, j * blk:(j + 1) * blk, :] for ref in refs))
    for j in range(nblk):
        r, c = divmod(j, grp)
        x = [ref[0, pl.ds(c * blk * grp + r, blk, stride=grp), :] for ref in refs]
        for i in range(3):
            x4f[i, j * blk:(j + 1) * blk, :] = x[i]
        stage(1, j, *x)
    for j in range(nblk):
        start = (j % grp) * (seq // grp) + j // grp
        stage(2, j, *(x4f[i, pl.ds(start, blk, stride=grp), :] for i in range(3)))

    def make_bias(dil, n_steps):
        qi = lax.broadcasted_iota(jnp.int32, (blk, 2 * blk), 0)
        ki = lax.broadcasted_iota(jnp.int32, (blk, 2 * blk), 1)
        steps = qi + blk - ki
        band = (steps >= 0) & (steps <= n_steps)
        dist = (steps * dil).astype(F32) * LOG2E
        return jnp.concatenate([jnp.where(band, -slope[h] * dist, NEG_BIG) for h in range(2)], axis=0)

    ones = jnp.ones((2 * blk, LANES), BF16)

    def partial(order, off, kcat, vcat, bias):
        nk = kcat.shape[0]
        s = _dot_nt(qs[order, pl.ds(2 * off, 2 * blk), :], kcat) + bias
        m = jnp.max(s, axis=-1, keepdims=True)
        p = jnp.exp2(s - m).astype(BF16)
        r = _dot(p, jnp.concatenate([vcat, ones[:nk]], axis=1))
        return (jnp.where(first, m[:blk], m[blk:]),
                jnp.where(first, r[:blk, LANES:], r[blk:, LANES:]),
                jnp.where(first, r[:blk, :LANES], r[blk:, :LANES]))

    def merge(a, b):
        m = jnp.maximum(a[0], b[0])
        wa = jnp.exp2(a[0] - m)
        wb = jnp.exp2(b[0] - m)
        return m, a[1] * wa + b[1] * wb, a[2] * wa + b[2] * wb

    def load_part(stage_i, idx):
        return tuple(part[stage_i, i, idx, :] for i in range(3))

    def store_part(stage_i, idx, val):
        for i in range(3):
            part[stage_i, i, idx, :] = val[i]

    (w1, d1), (w4, d4), (w16, d16) = DILATED_CONFIGS
    bias16 = make_bias(d16, w16 // d16)[:, blk:]
    bias4 = make_bias(d4, w4 // d4)
    bias1 = make_bias(d1, w1 // d1)

    def group16(g, carry):
        for c in range(grp):
            off = (g * grp + c) * blk
            cur = pl.ds(off, blk)
            res = partial(2, off, kk[2, cur, :], vv[2, cur, :], bias16)
            store_part(0, pl.ds(c * (seq // grp) + g, blk, stride=grp), res)
        return carry

    for g in range(grp):
        group16(g, 0)

    def group4(g, carry):
        k_prev = v_prev = None
        for c in range(grp):
            off = (g * grp + c) * blk
            cur = pl.ds(off, blk)
            k_cur, v_cur = kk[1, cur, :], vv[1, cur, :]
            if c == 0:
                res = partial(1, off, k_cur, v_cur, bias4[:, blk:])
            else:
                res = partial(1, off, jnp.concatenate([k_prev, k_cur], axis=0),
                              jnp.concatenate([v_prev, v_cur], axis=0), bias4)
            k_prev, v_prev = k_cur, v_cur
            res = merge(res, load_part(0, cur))
            store_part(1, pl.ds(c * blk * grp + g, blk, stride=grp), res)
        return carry

    for g in range(grp):
        group4(g, 0)

    def group1(g, carry):
        k_prev = v_prev = None
        for c in range(grp):
            off = (g * grp + c) * blk
            cur = pl.ds(off, blk)
            k_cur, v_cur = kk[0, cur, :], vv[0, cur, :]
            if g == 0 and c == 0:
                res = partial(0, off, k_cur, v_cur, bias1[:, blk:])
            else:
                if c == 0:
                    prev = pl.ds(off - blk, blk)
                    k_prev, v_prev = kk[0, prev, :], vv[0, prev, :]
                res = partial(0, off, jnp.concatenate([k_prev, k_cur], axis=0),
                              jnp.concatenate([v_prev, v_cur], axis=0), bias1)
            k_prev, v_prev = k_cur, v_cur
            _, l_tot, a_tot = merge(res, load_part(1, cur))
            o = a_tot / l_tot
            sq = o * o
            ss0 = jnp.sum(jnp.where(first, sq, 0.0), axis=-1, keepdims=True)
            ss1 = jnp.sum(jnp.where(first, 0.0, sq), axis=-1, keepdims=True)
            ms = jnp.where(first, ss0, ss1) * (1.0 / ATTN_HEAD_DIM)
            o_ref[0, cur, :] = (o * lax.rsqrt(ms + NORM_EPS) * gain_ref[...]).astype(o_ref.dtype)
        return carry

    for g in range(grp):
        group1(g, 0)


def _attention(qa, ka, va, gain, slopes):
    b, seq, aw = qa.shape
    pairs = aw // LANES
    nbr = len(DILATED_CONFIGS)
    blk = pl.BlockSpec((1, seq, LANES), lambda i, p: (i, 0, p))
    return pl.pallas_call(
        _attn_kernel,
        grid=(b, pairs),
        in_specs=[pl.BlockSpec(memory_space=pltpu.SMEM), blk, blk, blk,
                  pl.BlockSpec((1, LANES), lambda i, p: (0, p))],
        out_specs=blk,
        out_shape=jax.ShapeDtypeStruct((b, seq, aw), BF16),
        scratch_shapes=[pltpu.VMEM((nbr, 2 * seq, LANES), BF16),
                        pltpu.VMEM((nbr, seq, LANES), BF16),
                        pltpu.VMEM((nbr, seq, LANES), BF16),
                        pltpu.VMEM((3, seq, LANES), F32),
                        pltpu.VMEM((2, 3, seq, LANES), F32)],
        compiler_params=pltpu.CompilerParams(dimension_semantics=("parallel", "parallel"),
                                             vmem_limit_bytes=VMEM_LIMIT),
        name="dilated_attention",
    )(slopes, qa, ka, va, gain.reshape(1, aw))


def _split2(x):
    h1 = x.astype(BF16).astype(F32)
    return h1, (x - h1).astype(BF16).astype(F32)


def _mlstm_kernel(q_ref, k_ref, v_ref, og_ref, g_ref, gain_ref, o_ref, rowq, colq, lhs_s, guard_s, upd_s):
    chunk = MLSTM_CHUNK
    dh = MLSTM_HEAD_DIM
    seq = q_ref.shape[1]
    heads = q_ref.shape[2] // dh
    nc = seq // chunk
    nr = nc * heads
    assert chunk == LANES and dh == LANES

    ti = lax.broadcasted_iota(jnp.int32, (chunk, chunk), 0)
    si = lax.broadcasted_iota(jnp.int32, (chunk, chunk), 1)
    causal = si <= ti
    eye = (si == ti).astype(BF16)
    eye2 = jnp.concatenate([eye, eye], axis=1)
    tri_t = (ti <= si).astype(BF16)
    lane = lax.broadcasted_iota(jnp.int32, (nr, chunk), 1)
    row = lax.broadcasted_iota(jnp.int32, (nr, chunk), 0)

    i_pre = g_ref[0, 0] * LOG2E
    f_pre = g_ref[0, 1]
    lf = (jnp.minimum(f_pre, 0.0) - jnp.log1p(jnp.exp(-jnp.abs(f_pre)))) * LOG2E
    b = sum(_dot(part, tri_t) for part in _split3(lf))
    u = i_pre - b
    cm = u
    span = 1
    while span < chunk:
        nxt = min(span * 8, chunk)
        terms = [jnp.where(lane >= sh, pltpu.roll(cm, sh, axis=1), cm) for sh in range(span, nxt, span)]
        cm = functools.reduce(jnp.maximum, terms, cm)
        span = nxt
    b_last = jnp.broadcast_to(b[:, chunk - 1:chunk], (nr, chunk))
    u_max = jnp.broadcast_to(cm[:, chunk - 1:chunk], (nr, chunk))
    m = jnp.zeros((heads, chunk), F32)
    m_start = jnp.zeros((nr, chunk), F32)
    for c in range(nc - 1):
        rows_c = slice(c * heads, (c + 1) * heads)
        m = b_last[rows_c, :] + jnp.maximum(m, u_max[rows_c, :])
        m_start = jnp.where(row // heads == c + 1, jnp.concatenate([m] * nc, axis=0), m_start)
    m_next = b_last + jnp.maximum(m_start, u_max)
    z1, z2 = _split2(jnp.maximum(m_start, cm))
    z = z1 + z2
    r1, r2 = _split2(b + z)
    rowq[0] = u
    rowq[1] = m_start
    rowq[2] = jnp.exp2(b_last + u - m_next)
    rowq[3] = jnp.exp2(b_last + m_start - m_next)
    colq[0] = jnp.concatenate([z1, z2], axis=1)
    colq[1] = jnp.concatenate([r1, r2], axis=1)

    ones = jnp.ones((chunk, dh), BF16)

    for h in range(heads):
        cols = slice(h * dh, (h + 1) * dh)

        def independent(c, carry, h=h, cols=cols):
            rows = pl.ds(pl.multiple_of(c * chunk, chunk), chunk)
            gate_row = pl.ds(c * heads + h, 1)
            qb, kb = q_ref[0, rows, cols], k_ref[0, rows, cols]
            v_ext = jnp.concatenate([v_ref[0, rows, cols], ones], axis=1)
            tiles = []
            for j in range(2):
                v = jnp.broadcast_to(colq[j, gate_row, :], (BF16_ROWS, 2 * chunk)).astype(BF16)
                tiles.extend([v] * (chunk // BF16_ROWS))
            colf = _dot_nt(eye2, jnp.concatenate(tiles, axis=0))
            z_col, mrow_col = colf[:, :chunk], colf[:, chunk:]
            dw = jnp.where(causal, jnp.exp2(rowq[0, gate_row, :] - z_col), 0.0)
            gw = jnp.exp2(rowq[1, gate_row, :] - z_col)
            sc = _dot_nt(qb, kb) * dw
            lhs_s[rows, :] = jnp.concatenate([(gw * qb.astype(F32)).astype(BF16), sc.astype(BF16)], axis=1)
            guard_s[rows, :] = jnp.exp2(-mrow_col)
            wk_t = (kb.astype(F32).T * rowq[2, gate_row, :]).astype(BF16)
            upd_s[rows, :] = _dot(wk_t, v_ext)
            return carry

        lax.fori_loop(0, nc, independent, 0, unroll=16)

        def recurrent(c, state, h=h, cols=cols):
            rows = pl.ds(pl.multiple_of(c * chunk, chunk), chunk)
            v_ext = jnp.concatenate([v_ref[0, rows, cols], ones], axis=1)
            nd = _dot(lhs_s[rows, :], jnp.concatenate([state.astype(BF16), v_ext], axis=0))
            hidden = nd[:, :dh] / jnp.maximum(jnp.abs(nd[:, dh:]), guard_s[rows, :])
            gated = og_ref[0, rows, cols].astype(F32) * hidden
            o_ref[0, rows, cols] = _rmsnorm(gated, gain_ref[:, cols]).astype(o_ref.dtype)
            dec = rowq[3, pl.ds(c * heads + h, 1), :]
            return jnp.concatenate([dec, dec], axis=1) * state + upd_s[rows, :]

        lax.fori_loop(0, nc, recurrent, jnp.zeros((dh, 2 * dh), F32), unroll=16)


def _mlstm(qm, km, vm, om, gates, gain):
    b, seq, mw = qm.shape
    dh = MLSTM_HEAD_DIM
    nr = gates.shape[2]
    blk = pl.BlockSpec((1, seq, mw), lambda i: (i, 0, 0))
    return pl.pallas_call(
        _mlstm_kernel,
        grid=(b,),
        in_specs=[blk, blk, blk, blk,
                  pl.BlockSpec((1, 2, nr, MLSTM_CHUNK), lambda i: (i, 0, 0, 0)),
                  _const_spec((1, mw))],
        out_specs=blk,
        out_shape=jax.ShapeDtypeStruct((b, seq, mw), BF16),
        scratch_shapes=[pltpu.VMEM((4, nr, MLSTM_CHUNK), F32),
                        pltpu.VMEM((2, nr, 2 * MLSTM_CHUNK), F32),
                        pltpu.VMEM((seq, 2 * dh), BF16),
                        pltpu.VMEM((seq, dh), F32),
                        pltpu.VMEM((seq, 2 * dh), F32)],
        compiler_params=pltpu.CompilerParams(dimension_semantics=("parallel",), vmem_limit_bytes=VMEM_LIMIT),
        name="mlstm",
    )(qm, km, vm, om, gates, gain.reshape(1, mw))


def kernel(x, ffn1_norm_w, ffn1_w_gate, ffn1_w_up, ffn1_w_down, mix_norm_w, w_in, q_norm_w, k_norm_w, conv_w,
           conv_b, i_bias, f_bias, attn_out_gain, mlstm_out_gain, w_out, ffn2_norm_w, ffn2_w_gate, ffn2_w_up,
           ffn2_w_down):
    b, seq, d = x.shape
    n = b * seq
    aw = ATTN_HEADS * ATTN_HEAD_DIM
    mw = MLSTM_HEADS * MLSTM_HEAD_DIM
    n_main = 3 * aw + 4 * mw
    hm = MLSTM_HEADS
    slopes = jnp.asarray(2.0 ** (-8.0 * np.arange(1, ATTN_HEADS + 1, dtype=np.float32) / ATTN_HEADS), F32)
    xt = x.reshape(n, d)
    for l in range(ffn1_norm_w.shape[0]):
        xt = _ffn(xt, ffn1_norm_w[l].reshape(1, d), ffn1_w_gate[l].astype(BF16), ffn1_w_up[l].astype(BF16),
                  ffn1_w_down[l].astype(BF16))
        gate_bias = jnp.concatenate([i_bias[l], f_bias[l]]).astype(F32)
        qa, ka, va, qm, km, vm, om, gr = _inproj(xt, seq, mix_norm_w[l], w_in[l].astype(BF16),
                                                 w_in[l][:, n_main:].astype(BF16),
                                                 gate_bias, q_norm_w[l], k_norm_w[l], conv_w[l], conv_b[l])
        shp = lambda t: t.reshape(b, seq, -1)
        attn = _attention(shp(qa), shp(ka), shp(va), attn_out_gain[l], slopes)
        gates = gr.reshape(2, hm, b, seq // MLSTM_CHUNK, MLSTM_CHUNK).transpose(2, 0, 3, 1, 4)
        gates = gates.reshape(b, 2, (seq // MLSTM_CHUNK) * hm, MLSTM_CHUNK)
        mls = _mlstm(shp(qm), shp(km), shp(vm), shp(om), gates, mlstm_out_gain[l])
        xt = _ffn(xt, ffn2_norm_w[l].reshape(1, d), ffn2_w_gate[l].astype(BF16), ffn2_w_up[l].astype(BF16),
                  ffn2_w_down[l].astype(BF16),
                  mix=(attn.reshape(n, aw), mls.reshape(n, mw), w_out[l].astype(BF16)))
    return xt.reshape(b, seq, d)
```

```python
import functools

import numpy as np
import jax
import jax.numpy as jnp
from jax import lax
from jax.experimental import pallas as pl
from jax.experimental.pallas import tpu as pltpu

F32 = jnp.float32
BF16 = jnp.bfloat16

LANES = 128
SUBLANES = 8
BF16_ROWS = 16
NORM_EPS = 1e-6
ATTN_HEADS = 8
ATTN_HEAD_DIM = 64
MLSTM_HEADS = 4
MLSTM_HEAD_DIM = 128
CONV_WIDTH = 4
DILATED_CONFIGS = ((128, 1), (512, 4), (2048, 16))
ATTN_BLOCK = 128
MLSTM_CHUNK = 128
NEG_BIG = -1e30
ATTN_GROUP = 4
LOG2E = 1.4426950408889634
FFN_TOKENS = 512
INPROJ_TOKENS = 512
CONV_ROWS = 64
CAST_STEPS = 16
VMEM_LIMIT = 56 * 1024 * 1024


def _const_spec(shape):
    nd = len(shape)
    return pl.BlockSpec(shape, lambda *_: (0,) * nd, pipeline_mode=pl.Buffered(1))


def _rmsnorm(x, w):
    ms = jnp.mean(x * x, axis=-1, keepdims=True)
    return x * lax.rsqrt(ms + NORM_EPS) * w


def _dot(a, b):
    return jnp.dot(a, b, preferred_element_type=F32)


def _dot_nt(a, b):
    return lax.dot_general(a, b, (((1,), (1,)), ((), ())), preferred_element_type=F32)


def _split3(x):
    h1 = x.astype(BF16)
    r1 = x - h1.astype(F32)
    h2 = r1.astype(BF16)
    h3 = (r1 - h2.astype(F32)).astype(BF16)
    return h1, h2, h3


def _cast_kernel(*refs):
    half = len(refs) // 2
    for src, dst in zip(refs[:half], refs[half:]):
        dst[...] = src[...].astype(dst.dtype)


def _to_bf16(*ws):
    specs = [pl.BlockSpec((w.shape[0] // CAST_STEPS, w.shape[1]), lambda i: (i, 0)) for w in ws]
    return pl.pallas_call(
        _cast_kernel,
        grid=(CAST_STEPS,),
        in_specs=specs,
        out_specs=specs,
        out_shape=[jax.ShapeDtypeStruct(w.shape, BF16) for w in ws],
        compiler_params=pltpu.CompilerParams(dimension_semantics=("parallel",), vmem_limit_bytes=VMEM_LIMIT),
        name="cast_weights",
    )(*ws)


def _swiglu_half(x, nw_ref, wg_ref, wu_ref, wd_ref):
    h = _rmsnorm(x, nw_ref[...]).astype(BF16)
    g = _dot(h, wg_ref[...])
    u = _dot(h, wu_ref[...])
    a = (g * jax.nn.sigmoid(g) * u).astype(BF16)
    return x + 0.5 * _dot(a, wd_ref[...])


def _ffn_kernel(x_ref, nw_ref, wg_ref, wu_ref, wd_ref, o_ref):
    o_ref[...] = _swiglu_half(x_ref[...], nw_ref, wg_ref, wu_ref, wd_ref)


def _outproj_ffn_kernel(x_ref, ya_ref, ym_ref, wo_ref, nw_ref, wg_ref, wu_ref, wd_ref, o_ref):
    aw = ya_ref.shape[1]
    x = x_ref[...] + _dot(ya_ref[...], wo_ref[0:aw, :]) + _dot(ym_ref[...], wo_ref[aw:, :])
    o_ref[...] = _swiglu_half(x, nw_ref, wg_ref, wu_ref, wd_ref)


def _ffn(x, nw, wg, wu, wd, mix=None):
    n, d = x.shape
    f = wg.shape[1]
    tm = FFN_TOKENS
    tok = lambda w: pl.BlockSpec((tm, w), lambda i: (i, 0))
    w_specs = [_const_spec((1, d)), _const_spec((d, f)), _const_spec((d, f)), _const_spec((f, d))]
    if mix is None:
        body, ins, specs = _ffn_kernel, (x,), [tok(d)]
    else:
        ya, ym, wo = mix
        body, ins = _outproj_ffn_kernel, (x, ya, ym, wo)
        specs = [tok(d), tok(ya.shape[1]), tok(ym.shape[1]), _const_spec(wo.shape)]
    return pl.pallas_call(
        body,
        grid=(n // tm,),
        in_specs=specs + w_specs,
        out_specs=tok(d),
        out_shape=jax.ShapeDtypeStruct((n, d), F32),
        compiler_params=pltpu.CompilerParams(dimension_semantics=("parallel",), vmem_limit_bytes=VMEM_LIMIT),
        name="ffn" if mix is None else "outproj_ffn",
    )(*ins, nw, wg, wu, wd)


def _inproj_kernel(tiles_per_seq, x_ref, nw_ref, w_ref, wgr_ref, gbr_ref, qnw_ref, knw_ref, cw_ref, cb_ref,
                   qa_ref, ka_ref, va_ref, qm_ref, km_ref, vm_ref, om_ref, gr_ref, cpad):
    tm = x_ref.shape[0]
    aw = qa_ref.shape[1]
    mw = qm_ref.shape[1]
    h = _rmsnorm(x_ref[...], nw_ref[...]).astype(BF16)
    first = lax.broadcasted_iota(jnp.int32, (1, LANES), 1) < ATTN_HEAD_DIM

    def head_norm(dst, p, w_row, scale):
        for c in range(0, aw, LANES):
            ps = p[:, c:c + LANES]
            sq = ps * ps
            ss0 = jnp.sum(jnp.where(first, sq, 0.0), axis=-1, keepdims=True)
            ss1 = jnp.sum(jnp.where(first, 0.0, sq), axis=-1, keepdims=True)
            ms = jnp.where(first, ss0, ss1) * (1.0 / ATTN_HEAD_DIM)
            dst[:, c:c + LANES] = ps * lax.rsqrt(ms + NORM_EPS) * (w_row[:, c:c + LANES] * scale)

    head_norm(qa_ref, _dot(h, w_ref[:, 0:aw]), qnw_ref[...], ATTN_HEAD_DIM ** -0.5 * LOG2E)
    head_norm(ka_ref, _dot(h, w_ref[:, aw:2 * aw]), knw_ref[...], 1.0)
    va_ref[...] = _dot(h, w_ref[:, 2 * aw:3 * aw])

    base = 3 * aw
    hist = CONV_WIDTH - 1

    nslab = 2 * mw // LANES

    @pl.when(pl.program_id(0) % tiles_per_seq == 0)
    def _():
        cpad[:, 0:SUBLANES, :] = jnp.zeros((nslab, SUBLANES, LANES), F32)

    raw = _dot(h, w_ref[:, base:base + 2 * mw])
    row0 = jnp.minimum(pl.program_id(0), 0) + (SUBLANES - hist)
    for s in range(nslab):
        lanes = slice(s * LANES, (s + 1) * LANES)
        cpad[s, SUBLANES:, :] = raw[:, lanes]
        for r in range(0, tm, CONV_ROWS):
            y = cb_ref[:, lanes]
            for j in range(CONV_WIDTH):
                y = y + cpad[s, pl.ds(row0 + r + j, CONV_ROWS), :] * cw_ref[j:j + 1, lanes]
            y = y * jax.nn.sigmoid(y)
            if s * LANES < mw:
                qm_ref[r:r + CONV_ROWS, lanes] = y.astype(BF16)
            else:
                y = y * (MLSTM_HEAD_DIM ** -0.5)
                km_ref[r:r + CONV_ROWS, s * LANES - mw:(s + 1) * LANES - mw] = y.astype(BF16)
        cpad[s, 0:SUBLANES, :] = cpad[s, tm:tm + SUBLANES, :]

    vm_ref[...] = _dot(h, w_ref[:, base + 2 * mw:base + 3 * mw]).astype(BF16)
    om_ref[...] = jax.nn.sigmoid(_dot(h, w_ref[:, base + 3 * mw:base + 4 * mw])).astype(BF16)
    ng = gr_ref.shape[0]
    gr_ref[...] = _dot_nt(wgr_ref[...], h)[:ng, :] + gbr_ref[...]


def _inproj(x, seq, nw, w_main, w_gate, gate_bias, qnw, knw, conv_w, conv_b):
    n, d = x.shape
    tm = INPROJ_TOKENS
    aw = ATTN_HEADS * ATTN_HEAD_DIM
    mw = MLSTM_HEADS * MLSTM_HEAD_DIM
    ng = 2 * MLSTM_HEADS
    wgr = jnp.zeros((BF16_ROWS, d), BF16).at[:ng, :].set(w_gate.T)
    gbr = gate_bias.reshape(ng, 1)
    qnw_t = jnp.tile(qnw, ATTN_HEADS).reshape(1, aw)
    knw_t = jnp.tile(knw, ATTN_HEADS).reshape(1, aw)
    tok = lambda w: pl.BlockSpec((tm, w), lambda i: (i, 0))
    out_shape = ([jax.ShapeDtypeStruct((n, aw), F32)] * 3 + [jax.ShapeDtypeStruct((n, mw), BF16)] * 4
                 + [jax.ShapeDtypeStruct((ng, n), F32)])
    out_specs = [tok(aw)] * 3 + [tok(mw)] * 4 + [pl.BlockSpec((ng, tm), lambda i: (0, i))]
    return pl.pallas_call(
        functools.partial(_inproj_kernel, seq // tm),
        grid=(n // tm,),
        in_specs=[tok(d), _const_spec((1, d)), _const_spec(w_main.shape), _const_spec(wgr.shape),
                  _const_spec(gbr.shape), _const_spec((1, aw)), _const_spec((1, aw)),
                  _const_spec(conv_w.shape), _const_spec((1, 2 * mw))],
        out_specs=out_specs,
        out_shape=out_shape,
        scratch_shapes=[pltpu.VMEM((2 * mw // LANES, tm + SUBLANES, LANES), F32)],
        compiler_params=pltpu.CompilerParams(dimension_semantics=("arbitrary",), vmem_limit_bytes=VMEM_LIMIT),
        name="in_proj",
    )(x, nw.reshape(1, d), w_main, wgr, gbr, qnw_t, knw_t, conv_w, conv_b.reshape(1, 2 * mw))


def _attn_kernel(slopes_ref, q_ref, k_ref, v_ref, gain_ref, o_ref, qs, kk, vv, x4f, part):
    blk = ATTN_BLOCK
    seq = q_ref.shape[1]
    nblk = seq // blk
    grp = ATTN_GROUP
    assert [d for _, d in DILATED_CONFIGS] == [1, grp, grp * grp] and nblk == grp * grp
    pair = pl.program_id(1)
    lane = lax.broadcasted_iota(jnp.int32, (blk, LANES), 1)
    first = lane < ATTN_HEAD_DIM
    slope = (slopes_ref[2 * pair], slopes_ref[2 * pair + 1])
    refs = (q_ref, k_ref, v_ref)

    def stage(order, j, q, k, v):
        qs[order, 2 * j * blk:(2 * j + 1) * blk, :] = jnp.where(first, q, 0.0).astype(BF16)
        qs[order, (2 * j + 1) * blk:(2 * j + 2) * blk, :] = jnp.where(first, 0.0, q).astype(BF16)
        kk[order, j * blk:(j + 1) * blk, :] = k.astype(BF16)
        vv[order, j * blk:(j + 1) * blk, :] = v.astype(BF16)

    for j in range(nblk):
        stage(0, j, *(ref[0, j * blk:(j + 1) * blk, :] for ref in refs))
    for j in range(nblk):
        r, c = divmod(j, grp)
        x = [ref[0, pl.ds(c * blk * grp + r, blk, stride=grp), :] for ref in refs]
        for i in range(3):
            x4f[i, j * blk:(j + 1) * blk, :] = x[i]
        stage(1, j, *x)
    for j in range(nblk):
        start = (j % grp) * (seq // grp) + j // grp
        stage(2, j, *(x4f[i, pl.ds(start, blk, stride=grp), :] for i in range(3)))

    def make_bias(dil, n_steps):
        qi = lax.broadcasted_iota(jnp.int32, (blk, 2 * blk), 0)
        ki = lax.broadcasted_iota(jnp.int32, (blk, 2 * blk), 1)
        steps = qi + blk - ki
        band = (steps >= 0) & (steps <= n_steps)
        dist = (steps * dil).astype(F32) * LOG2E
        return jnp.concatenate([jnp.where(band, -slope[h] * dist, NEG_BIG) for h in range(2)], axis=0)

    ones = jnp.ones((2 * blk, LANES), BF16)

    def partial(order, off, kcat, vcat, bias):
        nk = kcat.shape[0]
        s = _dot_nt(qs[order, pl.ds(2 * off, 2 * blk), :], kcat) + bias
        m = jnp.max(s, axis=-1, keepdims=True)
        p = jnp.exp2(s - m).astype(BF16)
        r = _dot(p, jnp.concatenate([vcat, ones[:nk]], axis=1))
        return (jnp.where(first, m[:blk], m[blk:]),
                jnp.where(first, r[:blk, LANES:], r[blk:, LANES:]),
                jnp.where(first, r[:blk, :LANES], r[blk:, :LANES]))

    def merge(a, b):
        m = jnp.maximum(a[0], b[0])
        wa = jnp.exp2(a[0] - m)
        wb = jnp.exp2(b[0] - m)
        return m, a[1] * wa + b[1] * wb, a[2] * wa + b[2] * wb

    def load_part(stage_i, idx):
        return tuple(part[stage_i, i, idx, :] for i in range(3))

    def store_part(stage_i, idx, val):
        for i in range(3):
            part[stage_i, i, idx, :] = val[i]

    (w1, d1), (w4, d4), (w16, d16) = DILATED_CONFIGS
    bias16 = make_bias(d16, w16 // d16)[:, blk:]
    bias4 = make_bias(d4, w4 // d4)
    bias1 = make_bias(d1, w1 // d1)

    def group16(g, carry):
        for c in range(grp):
            off = (g * grp + c) * blk
            cur = pl.ds(off, blk)
            res = partial(2, off, kk[2, cur, :], vv[2, cur, :], bias16)
            store_part(0, pl.ds(c * (seq // grp) + g, blk, stride=grp), res)
        return carry

    for g in range(grp):
        group16(g, 0)

    def group4(g, carry):
        k_prev = v_prev = None
        for c in range(grp):
            off = (g * grp + c) * blk
            cur = pl.ds(off, blk)
            k_cur, v_cur = kk[1, cur, :], vv[1, cur, :]
            if c == 0:
                res = partial(1, off, k_cur, v_cur, bias4[:, blk:])
            else:
                res = partial(1, off, jnp.concatenate([k_prev, k_cur], axis=0),
                              jnp.concatenate([v_prev, v_cur], axis=0), bias4)
            k_prev, v_prev = k_cur, v_cur
            res = merge(res, load_part(0, cur))
            store_part(1, pl.ds(c * blk * grp + g, blk, stride=grp), res)
        return carry

    for g in range(grp):
        group4(g, 0)

    def group1(g, carry):
        k_prev = v_prev = None
        for c in range(grp):
            off = (g * grp + c) * blk
            cur = pl.ds(off, blk)
            k_cur, v_cur = kk[0, cur, :], vv[0, cur, :]
            if g == 0 and c == 0:
                res = partial(0, off, k_cur, v_cur, bias1[:, blk:])
            else:
                if c == 0:
                    prev = pl.ds(off - blk, blk)
                    k_prev, v_prev = kk[0, prev, :], vv[0, prev, :]
                res = partial(0, off, jnp.concatenate([k_prev, k_cur], axis=0),
                              jnp.concatenate([v_prev, v_cur], axis=0), bias1)
            k_prev, v_prev = k_cur, v_cur
            _, l_tot, a_tot = merge(res, load_part(1, cur))
            o = a_tot / l_tot
            sq = o * o
            ss0 = jnp.sum(jnp.where(first, sq, 0.0), axis=-1, keepdims=True)
            ss1 = jnp.sum(jnp.where(first, 0.0, sq), axis=-1, keepdims=True)
            ms = jnp.where(first, ss0, ss1) * (1.0 / ATTN_HEAD_DIM)
            o_ref[0, cur, :] = (o * lax.rsqrt(ms + NORM_EPS) * gain_ref[...]).astype(o_ref.dtype)
        return carry

    for g in range(grp):
        group1(g, 0)


def _attention(qa, ka, va, gain, slopes):
    b, seq, aw = qa.shape
    pairs = aw // LANES
    nbr = len(DILATED_CONFIGS)
    blk = pl.BlockSpec((1, seq, LANES), lambda i, p: (i, 0, p))
    return pl.pallas_call(
        _attn_kernel,
        grid=(b, pairs),
        in_specs=[pl.BlockSpec(memory_space=pltpu.SMEM), blk, blk, blk,
                  pl.BlockSpec((1, LANES), lambda i, p: (0, p))],
        out_specs=blk,
        out_shape=jax.ShapeDtypeStruct((b, seq, aw), BF16),
        scratch_shapes=[pltpu.VMEM((nbr, 2 * seq, LANES), BF16),
                        pltpu.VMEM((nbr, seq, LANES), BF16),
                        pltpu.VMEM((nbr, seq, LANES), BF16),
                        pltpu.VMEM((3, seq, LANES), F32),
                        pltpu.VMEM((2, 3, seq, LANES), F32)],
        compiler_params=pltpu.CompilerParams(dimension_semantics=("parallel", "parallel"),
                                             vmem_limit_bytes=VMEM_LIMIT),
        name="dilated_attention",
    )(slopes, qa, ka, va, gain.reshape(1, aw))


def _split2(x):
    h1 = x.astype(BF16).astype(F32)
    return h1, (x - h1).astype(BF16).astype(F32)


def _mlstm_kernel(q_ref, k_ref, v_ref, og_ref, g_ref, gain_ref, o_ref, rowq, colq, lhs_s, guard_s, upd_s):
    chunk = MLSTM_CHUNK
    dh = MLSTM_HEAD_DIM
    seq = q_ref.shape[1]
    heads = q_ref.shape[2] // dh
    nc = seq // chunk
    nr = nc * heads
    assert chunk == LANES and dh == LANES

    ti = lax.broadcasted_iota(jnp.int32, (chunk, chunk), 0)
    si = lax.broadcasted_iota(jnp.int32, (chunk, chunk), 1)
    causal = si <= ti
    eye = (si == ti).astype(BF16)
    eye2 = jnp.concatenate([eye, eye], axis=1)
    tri_t = (ti <= si).astype(BF16)
    lane = lax.broadcasted_iota(jnp.int32, (nr, chunk), 1)
    row = lax.broadcasted_iota(jnp.int32, (nr, chunk), 0)

    i_pre = g_ref[0, 0] * LOG2E
    f_pre = g_ref[0, 1]
    lf = (jnp.minimum(f_pre, 0.0) - jnp.log1p(jnp.exp(-jnp.abs(f_pre)))) * LOG2E
    b = sum(_dot(part, tri_t) for part in _split3(lf))
    u = i_pre - b
    cm = u
    span = 1
    while span < chunk:
        nxt = min(span * 8, chunk)
        terms = [jnp.where(lane >= sh, pltpu.roll(cm, sh, axis=1), cm) for sh in range(span, nxt, span)]
        cm = functools.reduce(jnp.maximum, terms, cm)
        span = nxt
    b_last = jnp.broadcast_to(b[:, chunk - 1:chunk], (nr, chunk))
    u_max = jnp.broadcast_to(cm[:, chunk - 1:chunk], (nr, chunk))
    m = jnp.zeros((heads, chunk), F32)
    m_start = jnp.zeros((nr, chunk), F32)
    for c in range(nc - 1):
        rows_c = slice(c * heads, (c + 1) * heads)
        m = b_last[rows_c, :] + jnp.maximum(m, u_max[rows_c, :])
        m_start = jnp.where(row // heads == c + 1, jnp.concatenate([m] * nc, axis=0), m_start)
    m_next = b_last + jnp.maximum(m_start, u_max)
    z1, z2 = _split2(jnp.maximum(m_start, cm))
    z = z1 + z2
    r1, r2 = _split2(b + z)
    rowq[0] = u
    rowq[1] = m_start
    rowq[2] = jnp.exp2(b_last + u - m_next)
    rowq[3] = jnp.exp2(b_last + m_start - m_next)
    colq[0] = jnp.concatenate([z1, z2], axis=1)
    colq[1] = jnp.concatenate([r1, r2], axis=1)

    ones = jnp.ones((chunk, dh), BF16)

    for h in range(heads):
        cols = slice(h * dh, (h + 1) * dh)

        def independent(c, carry, h=h, cols=cols):
            rows = pl.ds(pl.multiple_of(c * chunk, chunk), chunk)
            gate_row = pl.ds(c * heads + h, 1)
            qb, kb = q_ref[0, rows, cols], k_ref[0, rows, cols]
            v_ext = jnp.concatenate([v_ref[0, rows, cols], ones], axis=1)
            tiles = []
            for j in range(2):
                v = jnp.broadcast_to(colq[j, gate_row, :], (BF16_ROWS, 2 * chunk)).astype(BF16)
                tiles.extend([v] * (chunk // BF16_ROWS))
            colf = _dot_nt(eye2, jnp.concatenate(tiles, axis=0))
            z_col, mrow_col = colf[:, :chunk], colf[:, chunk:]
            dw = jnp.where(causal, jnp.exp2(rowq[0, gate_row, :] - z_col), 0.0)
            gw = jnp.exp2(rowq[1, gate_row, :] - z_col)
            sc = _dot_nt(qb, kb) * dw
            lhs_s[rows, :] = jnp.concatenate([(gw * qb.astype(F32)).astype(BF16), sc.astype(BF16)], axis=1)
            guard_s[rows, :] = jnp.exp2(-mrow_col)
            wk_t = (kb.astype(F32).T * rowq[2, gate_row, :]).astype(BF16)
            upd_s[rows, :] = _dot(wk_t, v_ext)
            return carry

        lax.fori_loop(0, nc, independent, 0, unroll=16)

        def recurrent(c, state, h=h, cols=cols):
            rows = pl.ds(pl.multiple_of(c * chunk, chunk), chunk)
            v_ext = jnp.concatenate([v_ref[0, rows, cols], ones], axis=1)
            nd = _dot(lhs_s[rows, :], jnp.concatenate([state.astype(BF16), v_ext], axis=0))
            hidden = nd[:, :dh] / jnp.maximum(jnp.abs(nd[:, dh:]), guard_s[rows, :])
            gated = og_ref[0, rows, cols].astype(F32) * hidden
            o_ref[0, rows, cols] = _rmsnorm(gated, gain_ref[:, cols]).astype(o_ref.dtype)
            dec = rowq[3, pl.ds(c * heads + h, 1), :]
            return jnp.concatenate([dec, dec], axis=1) * state + upd_s[rows, :]

        lax.fori_loop(0, nc, recurrent, jnp.zeros((dh, 2 * dh), F32), unroll=16)


def _mlstm(qm, km, vm, om, gates, gain):
    b, seq, mw = qm.shape
    dh = MLSTM_HEAD_DIM
    nr = gates.shape[2]
    blk = pl.BlockSpec((1, seq, mw), lambda i: (i, 0, 0))
    return pl.pallas_call(
        _mlstm_kernel,
        grid=(b,),
        in_specs=[blk, blk, blk, blk,
                  pl.BlockSpec((1, 2, nr, MLSTM_CHUNK), lambda i: (i, 0, 0, 0)),
                  _const_spec((1, mw))],
        out_specs=blk,
        out_shape=jax.ShapeDtypeStruct((b, seq, mw), BF16),
        scratch_shapes=[pltpu.VMEM((4, nr, MLSTM_CHUNK), F32),
                        pltpu.VMEM((2, nr, 2 * MLSTM_CHUNK), F32),
                        pltpu.VMEM((seq, 2 * dh), BF16),
                        pltpu.VMEM((seq, dh), F32),
                        pltpu.VMEM((seq, 2 * dh), F32)],
        compiler_params=pltpu.CompilerParams(dimension_semantics=("parallel",), vmem_limit_bytes=VMEM_LIMIT),
        name="mlstm",
    )(qm, km, vm, om, gates, gain.reshape(1, mw))


def kernel(x, ffn1_norm_w, ffn1_w_gate, ffn1_w_up, ffn1_w_down, mix_norm_w, w_in, q_norm_w, k_norm_w, conv_w,
           conv_b, i_bias, f_bias, attn_out_gain, mlstm_out_gain, w_out, ffn2_norm_w, ffn2_w_gate, ffn2_w_up,
           ffn2_w_down):
    b, seq, d = x.shape
    n = b * seq
    aw = ATTN_HEADS * ATTN_HEAD_DIM
    mw = MLSTM_HEADS * MLSTM_HEAD_DIM
    n_main = 3 * aw + 4 * mw
    hm = MLSTM_HEADS
    slopes = jnp.asarray(2.0 ** (-8.0 * np.arange(1, ATTN_HEADS + 1, dtype=np.float32) / ATTN_HEADS), F32)
    xt = x.reshape(n, d)
    for l in range(ffn1_norm_w.shape[0]):
        g1, u1, d1, w_i, w_o, g2, u2, d2 = _to_bf16(ffn1_w_gate[l], ffn1_w_up[l], ffn1_w_down[l], w_in[l], w_out[l],
                                                    ffn2_w_gate[l], ffn2_w_up[l], ffn2_w_down[l])
        xt = _ffn(xt, ffn1_norm_w[l].reshape(1, d), g1, u1, d1)
        gate_bias = jnp.concatenate([i_bias[l], f_bias[l]]).astype(F32)
        qa, ka, va, qm, km, vm, om, gr = _inproj(xt, seq, mix_norm_w[l], w_i, w_in[l][:, n_main:].astype(BF16),
                                                 gate_bias, q_norm_w[l], k_norm_w[l], conv_w[l], conv_b[l])
        shp = lambda t: t.reshape(b, seq, -1)
        attn = _attention(shp(qa), shp(ka), shp(va), attn_out_gain[l], slopes)
        gates = gr.reshape(2, hm, b, seq // MLSTM_CHUNK, MLSTM_CHUNK).transpose(2, 0, 3, 1, 4)
        gates = gates.reshape(b, 2, (seq // MLSTM_CHUNK) * hm, MLSTM_CHUNK)
        mls = _mlstm(shp(qm), shp(km), shp(vm), shp(om), gates, mlstm_out_gain[l])
        xt = _ffn(xt, ffn2_norm_w[l].reshape(1, d), g2, u2, d2,
                  mix=(attn.reshape(n, aw), mls.reshape(n, mw), w_o))
    return xt.reshape(b, seq, d)
```

```python
import functools

import numpy as np
import jax
import jax.numpy as jnp
from jax import lax
from jax.experimental import pallas as pl
from jax.experimental.pallas import tpu as pltpu

F32 = jnp.float32
BF16 = jnp.bfloat16

LANES = 128
SUBLANES = 8
BF16_ROWS = 16
NORM_EPS = 1e-6
ATTN_HEADS = 8
ATTN_HEAD_DIM = 64
MLSTM_HEADS = 4
MLSTM_HEAD_DIM = 128
CONV_WIDTH = 4
DILATED_CONFIGS = ((128, 1), (512, 4), (2048, 16))
ATTN_BLOCK = 128
MLSTM_CHUNK = 128
NEG_BIG = -1e30
ATTN_GROUP = 4
LOG2E = 1.4426950408889634
FFN_TOKENS = 512
INPROJ_TOKENS = 512
CONV_ROWS = 64
CAST_STEPS = 16
VMEM_LIMIT = 56 * 1024 * 1024


def _const_spec(shape):
    nd = len(shape)
    return pl.BlockSpec(shape, lambda *_: (0,) * nd, pipeline_mode=pl.Buffered(1))


def _rmsnorm(x, w):
    ms = jnp.mean(x * x, axis=-1, keepdims=True)
    return x * lax.rsqrt(ms + NORM_EPS) * w


def _dot(a, b):
    return jnp.dot(a, b, preferred_element_type=F32)


def _dot_nt(a, b):
    return lax.dot_general(a, b, (((1,), (1,)), ((), ())), preferred_element_type=F32)


def _split3(x):
    h1 = x.astype(BF16)
    r1 = x - h1.astype(F32)
    h2 = r1.astype(BF16)
    h3 = (r1 - h2.astype(F32)).astype(BF16)
    return h1, h2, h3


def _cast_kernel(*refs):
    half = len(refs) // 2
    for src, dst in zip(refs[:half], refs[half:]):
        dst[...] = src[...].astype(dst.dtype)


def _to_bf16(layer, *ws):
    chunk = lambda w: (w.shape[1] // CAST_STEPS, w.shape[2])
    return pl.pallas_call(
        _cast_kernel,
        grid=(CAST_STEPS,),
        in_specs=[pl.BlockSpec((None,) + chunk(w), lambda i: (layer, i, 0)) for w in ws],
        out_specs=[pl.BlockSpec(chunk(w), lambda i: (i, 0)) for w in ws],
        out_shape=[jax.ShapeDtypeStruct(w.shape[1:], BF16) for w in ws],
        compiler_params=pltpu.CompilerParams(dimension_semantics=("parallel",), vmem_limit_bytes=VMEM_LIMIT),
        name="cast_weights",
    )(*ws)


def _swiglu_half(x, nw_ref, wg_ref, wu_ref, wd_ref):
    h = _rmsnorm(x, nw_ref[...]).astype(BF16)
    g = _dot(h, wg_ref[...])
    u = _dot(h, wu_ref[...])
    a = (g * jax.nn.sigmoid(g) * u).astype(BF16)
    return x + 0.5 * _dot(a, wd_ref[...])


def _ffn_kernel(x_ref, nw_ref, wg_ref, wu_ref, wd_ref, o_ref):
    o_ref[...] = _swiglu_half(x_ref[...], nw_ref, wg_ref, wu_ref, wd_ref)


def _outproj_ffn_kernel(x_ref, ya_ref, ym_ref, wo_ref, nw_ref, wg_ref, wu_ref, wd_ref, o_ref):
    aw = ya_ref.shape[1]
    x = x_ref[...] + _dot(ya_ref[...], wo_ref[0:aw, :]) + _dot(ym_ref[...], wo_ref[aw:, :])
    o_ref[...] = _swiglu_half(x, nw_ref, wg_ref, wu_ref, wd_ref)


def _ffn(x, nw, wg, wu, wd, mix=None):
    n, d = x.shape
    f = wg.shape[1]
    tm = FFN_TOKENS
    tok = lambda w: pl.BlockSpec((tm, w), lambda i: (i, 0))
    w_specs = [_const_spec((1, d)), _const_spec((d, f)), _const_spec((d, f)), _const_spec((f, d))]
    if mix is None:
        body, ins, specs = _ffn_kernel, (x,), [tok(d)]
    else:
        ya, ym, wo = mix
        body, ins = _outproj_ffn_kernel, (x, ya, ym, wo)
        specs = [tok(d), tok(ya.shape[1]), tok(ym.shape[1]), _const_spec(wo.shape)]
    return pl.pallas_call(
        body,
        grid=(n // tm,),
        in_specs=specs + w_specs,
        out_specs=tok(d),
        out_shape=jax.ShapeDtypeStruct((n, d), F32),
        compiler_params=pltpu.CompilerParams(dimension_semantics=("parallel",), vmem_limit_bytes=VMEM_LIMIT),
        name="ffn" if mix is None else "outproj_ffn",
    )(*ins, nw, wg, wu, wd)


def _inproj_kernel(tiles_per_seq, x_ref, nw_ref, w_ref, wgr_ref, gbr_ref, qnw_ref, knw_ref, cw_ref, cb_ref,
                   qa_ref, ka_ref, va_ref, qm_ref, km_ref, vm_ref, om_ref, gr_ref, cpad):
    tm = x_ref.shape[0]
    aw = qa_ref.shape[1]
    mw = qm_ref.shape[1]
    h = _rmsnorm(x_ref[...], nw_ref[...]).astype(BF16)
    first = lax.broadcasted_iota(jnp.int32, (1, LANES), 1) < ATTN_HEAD_DIM

    def head_norm(dst, p, w_row, scale):
        for c in range(0, aw, LANES):
            ps = p[:, c:c + LANES]
            sq = ps * ps
            ss0 = jnp.sum(jnp.where(first, sq, 0.0), axis=-1, keepdims=True)
            ss1 = jnp.sum(jnp.where(first, 0.0, sq), axis=-1, keepdims=True)
            ms = jnp.where(first, ss0, ss1) * (1.0 / ATTN_HEAD_DIM)
            dst[:, c:c + LANES] = ps * lax.rsqrt(ms + NORM_EPS) * (w_row[:, c:c + LANES] * scale)

    head_norm(qa_ref, _dot(h, w_ref[:, 0:aw]), qnw_ref[...], ATTN_HEAD_DIM ** -0.5 * LOG2E)
    head_norm(ka_ref, _dot(h, w_ref[:, aw:2 * aw]), knw_ref[...], 1.0)
    va_ref[...] = _dot(h, w_ref[:, 2 * aw:3 * aw])

    base = 3 * aw
    hist = CONV_WIDTH - 1

    nslab = 2 * mw // LANES

    @pl.when(pl.program_id(0) % tiles_per_seq == 0)
    def _():
        cpad[:, 0:SUBLANES, :] = jnp.zeros((nslab, SUBLANES, LANES), F32)

    raw = _dot(h, w_ref[:, base:base + 2 * mw])
    row0 = jnp.minimum(pl.program_id(0), 0) + (SUBLANES - hist)
    for s in range(nslab):
        lanes = slice(s * LANES, (s + 1) * LANES)
        cpad[s, SUBLANES:, :] = raw[:, lanes]
        for r in range(0, tm, CONV_ROWS):
            y = cb_ref[:, lanes]
            for j in range(CONV_WIDTH):
                y = y + cpad[s, pl.ds(row0 + r + j, CONV_ROWS), :] * cw_ref[j:j + 1, lanes]
            y = y * jax.nn.sigmoid(y)
            if s * LANES < mw:
                qm_ref[r:r + CONV_ROWS, lanes] = y.astype(BF16)
            else:
                y = y * (MLSTM_HEAD_DIM ** -0.5)
                km_ref[r:r + CONV_ROWS, s * LANES - mw:(s + 1) * LANES - mw] = y.astype(BF16)
        cpad[s, 0:SUBLANES, :] = cpad[s, tm:tm + SUBLANES, :]

    vm_ref[...] = _dot(h, w_ref[:, base + 2 * mw:base + 3 * mw]).astype(BF16)
    om_ref[...] = jax.nn.sigmoid(_dot(h, w_ref[:, base + 3 * mw:base + 4 * mw])).astype(BF16)
    ng = gr_ref.shape[0]
    gr_ref[...] = _dot_nt(wgr_ref[...], h)[:ng, :] + gbr_ref[...]


def _inproj(x, seq, nw, w_main, w_gate, gate_bias, qnw, knw, conv_w, conv_b):
    n, d = x.shape
    tm = INPROJ_TOKENS
    aw = ATTN_HEADS * ATTN_HEAD_DIM
    mw = MLSTM_HEADS * MLSTM_HEAD_DIM
    ng = 2 * MLSTM_HEADS
    wgr = jnp.zeros((BF16_ROWS, d), BF16).at[:ng, :].set(w_gate.T)
    gbr = gate_bias.reshape(ng, 1)
    qnw_t = jnp.tile(qnw, ATTN_HEADS).reshape(1, aw)
    knw_t = jnp.tile(knw, ATTN_HEADS).reshape(1, aw)
    tok = lambda w: pl.BlockSpec((tm, w), lambda i: (i, 0))
    out_shape = ([jax.ShapeDtypeStruct((n, aw), F32)] * 3 + [jax.ShapeDtypeStruct((n, mw), BF16)] * 4
                 + [jax.ShapeDtypeStruct((ng, n), F32)])
    out_specs = [tok(aw)] * 3 + [tok(mw)] * 4 + [pl.BlockSpec((ng, tm), lambda i: (0, i))]
    return pl.pallas_call(
        functools.partial(_inproj_kernel, seq // tm),
        grid=(n // tm,),
        in_specs=[tok(d), _const_spec((1, d)), _const_spec(w_main.shape), _const_spec(wgr.shape),
                  _const_spec(gbr.shape), _const_spec((1, aw)), _const_spec((1, aw)),
                  _const_spec(conv_w.shape), _const_spec((1, 2 * mw))],
        out_specs=out_specs,
        out_shape=out_shape,
        scratch_shapes=[pltpu.VMEM((2 * mw // LANES, tm + SUBLANES, LANES), F32)],
        compiler_params=pltpu.CompilerParams(dimension_semantics=("arbitrary",), vmem_limit_bytes=VMEM_LIMIT),
        name="in_proj",
    )(x, nw.reshape(1, d), w_main, wgr, gbr, qnw_t, knw_t, conv_w, conv_b.reshape(1, 2 * mw))


def _attn_kernel(slopes_ref, q_ref, k_ref, v_ref, gain_ref, o_ref, qs, kk, vv, x4f, part):
    blk = ATTN_BLOCK
    seq = q_ref.shape[1]
    nblk = seq // blk
    grp = ATTN_GROUP
    assert [d for _, d in DILATED_CONFIGS] == [1, grp, grp * grp] and nblk == grp * grp
    pair = pl.program_id(1)
    lane = lax.broadcasted_iota(jnp.int32, (blk, LANES), 1)
    first = lane < ATTN_HEAD_DIM
    slope = (slopes_ref[2 * pair], slopes_ref[2 * pair + 1])
    refs = (q_ref, k_ref, v_ref)

    def stage(order, j, q, k, v):
        qs[order, 2 * j * blk:(2 * j + 1) * blk, :] = jnp.where(first, q, 0.0).astype(BF16)
        qs[order, (2 * j + 1) * blk:(2 * j + 2) * blk, :] = jnp.where(first, 0.0, q).astype(BF16)
        kk[order, j * blk:(j + 1) * blk, :] = k.astype(BF16)
        vv[order, j * blk:(j + 1) * blk, :] = v.astype(BF16)

    for j in range(nblk):
        stage(0, j, *(ref[0, j * blk:(j + 1) * blk, :] for ref in refs))
    for j in range(nblk):
        r, c = divmod(j, grp)
        x = [ref[0, pl.ds(c * blk * grp + r, blk, stride=grp), :] for ref in refs]
        for i in range(3):
            x4f[i, j * blk:(j + 1) * blk, :] = x[i]
        stage(1, j, *x)
    for j in range(nblk):
        start = (j % grp) * (seq // grp) + j // grp
        stage(2, j, *(x4f[i, pl.ds(start, blk, stride=grp), :] for i in range(3)))

    def make_bias(dil, n_steps):
        qi = lax.broadcasted_iota(jnp.int32, (blk, 2 * blk), 0)
        ki = lax.broadcasted_iota(jnp.int32, (blk, 2 * blk), 1)
        steps = qi + blk - ki
        band = (steps >= 0) & (steps <= n_steps)
        dist = (steps * dil).astype(F32) * LOG2E
        return jnp.concatenate([jnp.where(band, -slope[h] * dist, NEG_BIG) for h in range(2)], axis=0)

    ones = jnp.ones((2 * blk, LANES), BF16)

    def partial(order, off, kcat, vcat, bias):
        nk = kcat.shape[0]
        s = _dot_nt(qs[order, pl.ds(2 * off, 2 * blk), :], kcat) + bias
        m = jnp.max(s, axis=-1, keepdims=True)
        p = jnp.exp2(s - m).astype(BF16)
        r = _dot(p, jnp.concatenate([vcat, ones[:nk]], axis=1))
        return (jnp.where(first, m[:blk], m[blk:]),
                jnp.where(first, r[:blk, LANES:], r[blk:, LANES:]),
                jnp.where(first, r[:blk, :LANES], r[blk:, :LANES]))

    def merge(a, b):
        m = jnp.maximum(a[0], b[0])
        wa = jnp.exp2(a[0] - m)
        wb = jnp.exp2(b[0] - m)
        return m, a[1] * wa + b[1] * wb, a[2] * wa + b[2] * wb

    def load_part(stage_i, idx):
        return tuple(part[stage_i, i, idx, :] for i in range(3))

    def store_part(stage_i, idx, val):
        for i in range(3):
            part[stage_i, i, idx, :] = val[i]

    (w1, d1), (w4, d4), (w16, d16) = DILATED_CONFIGS
    bias16 = make_bias(d16, w16 // d16)[:, blk:]
    bias4 = make_bias(d4, w4 // d4)
    bias1 = make_bias(d1, w1 // d1)

    def group16(g, carry):
        for c in range(grp):
            off = (g * grp + c) * blk
            cur = pl.ds(off, blk)
            res = partial(2, off, kk[2, cur, :], vv[2, cur, :], bias16)
            store_part(0, pl.ds(c * (seq // grp) + g, blk, stride=grp), res)
        return carry

    for g in range(grp):
        group16(g, 0)

    def group4(g, carry):
        k_prev = v_prev = None
        for c in range(grp):
            off = (g * grp + c) * blk
            cur = pl.ds(off, blk)
            k_cur, v_cur = kk[1, cur, :], vv[1, cur, :]
            if c == 0:
                res = partial(1, off, k_cur, v_cur, bias4[:, blk:])
            else:
                res = partial(1, off, jnp.concatenate([k_prev, k_cur], axis=0),
                              jnp.concatenate([v_prev, v_cur], axis=0), bias4)
            k_prev, v_prev = k_cur, v_cur
            res = merge(res, load_part(0, cur))
            store_part(1, pl.ds(c * blk * grp + g, blk, stride=grp), res)
        return carry

    for g in range(grp):
        group4(g, 0)

    def group1(g, carry):
        k_prev = v_prev = None
        for c in range(grp):
            off = (g * grp + c) * blk
            cur = pl.ds(off, blk)
            k_cur, v_cur = kk[0, cur, :], vv[0, cur, :]
            if g == 0 and c == 0:
                res = partial(0, off, k_cur, v_cur, bias1[:, blk:])
            else:
                if c == 0:
                    prev = pl.ds(off - blk, blk)
                    k_prev, v_prev = kk[0, prev, :], vv[0, prev, :]
                res = partial(0, off, jnp.concatenate([k_prev, k_cur], axis=0),
                              jnp.concatenate([v_prev, v_cur], axis=0), bias1)
            k_prev, v_prev = k_cur, v_cur
            _, l_tot, a_tot = merge(res, load_part(1, cur))
            o = a_tot / l_tot
            sq = o * o
            ss0 = jnp.sum(jnp.where(first, sq, 0.0), axis=-1, keepdims=True)
            ss1 = jnp.sum(jnp.where(first, 0.0, sq), axis=-1, keepdims=True)
            ms = jnp.where(first, ss0, ss1) * (1.0 / ATTN_HEAD_DIM)
            o_ref[0, cur, :] = (o * lax.rsqrt(ms + NORM_EPS) * gain_ref[...]).astype(o_ref.dtype)
        return carry

    for g in range(grp):
        group1(g, 0)


def _attention(qa, ka, va, gain, slopes):
    b, seq, aw = qa.shape
    pairs = aw // LANES
    nbr = len(DILATED_CONFIGS)
    blk = pl.BlockSpec((1, seq, LANES), lambda i, p: (i, 0, p))
    return pl.pallas_call(
        _attn_kernel,
        grid=(b, pairs),
        in_specs=[pl.BlockSpec(memory_space=pltpu.SMEM), blk, blk, blk,
                  pl.BlockSpec((1, LANES), lambda i, p: (0, p))],
        out_specs=blk,
        out_shape=jax.ShapeDtypeStruct((b, seq, aw), BF16),
        scratch_shapes=[pltpu.VMEM((nbr, 2 * seq, LANES), BF16),
                        pltpu.VMEM((nbr, seq, LANES), BF16),
                        pltpu.VMEM((nbr, seq, LANES), BF16),
                        pltpu.VMEM((3, seq, LANES), F32),
                        pltpu.VMEM((2, 3, seq, LANES), F32)],
        compiler_params=pltpu.CompilerParams(dimension_semantics=("parallel", "parallel"),
                                             vmem_limit_bytes=VMEM_LIMIT),
        name="dilated_attention",
    )(slopes, qa, ka, va, gain.reshape(1, aw))


def _split2(x):
    h1 = x.astype(BF16).astype(F32)
    return h1, (x - h1).astype(BF16).astype(F32)


def _mlstm_kernel(q_ref, k_ref, v_ref, og_ref, g_ref, gain_ref, o_ref, rowq, colq, lhs_s, guard_s, upd_s):
    chunk = MLSTM_CHUNK
    dh = MLSTM_HEAD_DIM
    seq = q_ref.shape[1]
    heads = q_ref.shape[2] // dh
    nc = seq // chunk
    nr = nc * heads
    assert chunk == LANES and dh == LANES

    ti = lax.broadcasted_iota(jnp.int32, (chunk, chunk), 0)
    si = lax.broadcasted_iota(jnp.int32, (chunk, chunk), 1)
    causal = si <= ti
    eye = (si == ti).astype(BF16)
    eye2 = jnp.concatenate([eye, eye], axis=1)
    tri_t = (ti <= si).astype(BF16)
    lane = lax.broadcasted_iota(jnp.int32, (nr, chunk), 1)
    row = lax.broadcasted_iota(jnp.int32, (nr, chunk), 0)

    i_pre = g_ref[0, 0] * LOG2E
    f_pre = g_ref[0, 1]
    lf = (jnp.minimum(f_pre, 0.0) - jnp.log1p(jnp.exp(-jnp.abs(f_pre)))) * LOG2E
    b = sum(_dot(part, tri_t) for part in _split3(lf))
    u = i_pre - b
    cm = u
    span = 1
    while span < chunk:
        nxt = min(span * 8, chunk)
        terms = [jnp.where(lane >= sh, pltpu.roll(cm, sh, axis=1), cm) for sh in range(span, nxt, span)]
        cm = functools.reduce(jnp.maximum, terms, cm)
        span = nxt
    b_last = jnp.broadcast_to(b[:, chunk - 1:chunk], (nr, chunk))
    u_max = jnp.broadcast_to(cm[:, chunk - 1:chunk], (nr, chunk))
    m = jnp.zeros((heads, chunk), F32)
    m_start = jnp.zeros((nr, chunk), F32)
    for c in range(nc - 1):
        rows_c = slice(c * heads, (c + 1) * heads)
        m = b_last[rows_c, :] + jnp.maximum(m, u_max[rows_c, :])
        m_start = jnp.where(row // heads == c + 1, jnp.concatenate([m] * nc, axis=0), m_start)
    m_next = b_last + jnp.maximum(m_start, u_max)
    z1, z2 = _split2(jnp.maximum(m_start, cm))
    z = z1 + z2
    r1, r2 = _split2(b + z)
    rowq[0] = u
    rowq[1] = m_start
    rowq[2] = jnp.exp2(b_last + u - m_next)
    rowq[3] = jnp.exp2(b_last + m_start - m_next)
    colq[0] = jnp.concatenate([z1, z2], axis=1)
    colq[1] = jnp.concatenate([r1, r2], axis=1)

    ones = jnp.ones((chunk, dh), BF16)

    for h in range(heads):
        cols = slice(h * dh, (h + 1) * dh)

        def independent(c, carry, h=h, cols=cols):
            rows = pl.ds(pl.multiple_of(c * chunk, chunk), chunk)
            gate_row = pl.ds(c * heads + h, 1)
            qb, kb = q_ref[0, rows, cols], k_ref[0, rows, cols]
            v_ext = jnp.concatenate([v_ref[0, rows, cols], ones], axis=1)
            tiles = []
            for j in range(2):
                v = jnp.broadcast_to(colq[j, gate_row, :], (BF16_ROWS, 2 * chunk)).astype(BF16)
                tiles.extend([v] * (chunk // BF16_ROWS))
            colf = _dot_nt(eye2, jnp.concatenate(tiles, axis=0))
            z_col, mrow_col = colf[:, :chunk], colf[:, chunk:]
            dw = jnp.where(causal, jnp.exp2(rowq[0, gate_row, :] - z_col), 0.0)
            gw = jnp.exp2(rowq[1, gate_row, :] - z_col)
            sc = _dot_nt(qb, kb) * dw
            lhs_s[rows, :] = jnp.concatenate([(gw * qb.astype(F32)).astype(BF16), sc.astype(BF16)], axis=1)
            guard_s[rows, :] = jnp.exp2(-mrow_col)
            wk_t = (kb.astype(F32).T * rowq[2, gate_row, :]).astype(BF16)
            upd_s[rows, :] = _dot(wk_t, v_ext)
            return carry

        lax.fori_loop(0, nc, independent, 0, unroll=16)

        def recurrent(c, state, h=h, cols=cols):
            rows = pl.ds(pl.multiple_of(c * chunk, chunk), chunk)
            v_ext = jnp.concatenate([v_ref[0, rows, cols], ones], axis=1)
            nd = _dot(lhs_s[rows, :], jnp.concatenate([state.astype(BF16), v_ext], axis=0))
            hidden = nd[:, :dh] / jnp.maximum(jnp.abs(nd[:, dh:]), guard_s[rows, :])
            gated = og_ref[0, rows, cols].astype(F32) * hidden
            o_ref[0, rows, cols] = _rmsnorm(gated, gain_ref[:, cols]).astype(o_ref.dtype)
            dec = rowq[3, pl.ds(c * heads + h, 1), :]
            return jnp.concatenate([dec, dec], axis=1) * state + upd_s[rows, :]

        lax.fori_loop(0, nc, recurrent, jnp.zeros((dh, 2 * dh), F32), unroll=16)


def _mlstm(qm, km, vm, om, gates, gain):
    b, seq, mw = qm.shape
    dh = MLSTM_HEAD_DIM
    nr = gates.shape[2]
    blk = pl.BlockSpec((1, seq, mw), lambda i: (i, 0, 0))
    return pl.pallas_call(
        _mlstm_kernel,
        grid=(b,),
        in_specs=[blk, blk, blk, blk,
                  pl.BlockSpec((1, 2, nr, MLSTM_CHUNK), lambda i: (i, 0, 0, 0)),
                  _const_spec((1, mw))],
        out_specs=blk,
        out_shape=jax.ShapeDtypeStruct((b, seq, mw), BF16),
        scratch_shapes=[pltpu.VMEM((4, nr, MLSTM_CHUNK), F32),
                        pltpu.VMEM((2, nr, 2 * MLSTM_CHUNK), F32),
                        pltpu.VMEM((seq, 2 * dh), BF16),
                        pltpu.VMEM((seq, dh), F32),
                        pltpu.VMEM((seq, 2 * dh), F32)],
        compiler_params=pltpu.CompilerParams(dimension_semantics=("parallel",), vmem_limit_bytes=VMEM_LIMIT),
        name="mlstm",
    )(qm, km, vm, om, gates, gain.reshape(1, mw))


def kernel(x, ffn1_norm_w, ffn1_w_gate, ffn1_w_up, ffn1_w_down, mix_norm_w, w_in, q_norm_w, k_norm_w, conv_w,
           conv_b, i_bias, f_bias, attn_out_gain, mlstm_out_gain, w_out, ffn2_norm_w, ffn2_w_gate, ffn2_w_up,
           ffn2_w_down):
    b, seq, d = x.shape
    n = b * seq
    aw = ATTN_HEADS * ATTN_HEAD_DIM
    mw = MLSTM_HEADS * MLSTM_HEAD_DIM
    n_main = 3 * aw + 4 * mw
    hm = MLSTM_HEADS
    slopes = jnp.asarray(2.0 ** (-8.0 * np.arange(1, ATTN_HEADS + 1, dtype=np.float32) / ATTN_HEADS), F32)
    xt = x.reshape(n, d)
    for l in range(ffn1_norm_w.shape[0]):
        g1, u1, d1, w_i, w_o, g2, u2, d2 = _to_bf16(l, ffn1_w_gate, ffn1_w_up, ffn1_w_down, w_in, w_out,
                                                    ffn2_w_gate, ffn2_w_up, ffn2_w_down)
        xt = _ffn(xt, ffn1_norm_w[l].reshape(1, d), g1, u1, d1)
        gate_bias = jnp.concatenate([i_bias[l], f_bias[l]]).astype(F32)
        qa, ka, va, qm, km, vm, om, gr = _inproj(xt, seq, mix_norm_w[l], w_i, w_in[l][:, n_main:].astype(BF16),
                                                 gate_bias, q_norm_w[l], k_norm_w[l], conv_w[l], conv_b[l])
        shp = lambda t: t.reshape(b, seq, -1)
        attn = _attention(shp(qa), shp(ka), shp(va), attn_out_gain[l], slopes)
        gates = gr.reshape(2, hm, b, seq // MLSTM_CHUNK, MLSTM_CHUNK).transpose(2, 0, 3, 1, 4)
        gates = gates.reshape(b, 2, (seq // MLSTM_CHUNK) * hm, MLSTM_CHUNK)
        mls = _mlstm(shp(qm), shp(km), shp(vm), shp(om), gates, mlstm_out_gain[l])
        xt = _ffn(xt, ffn2_norm_w[l].reshape(1, d), g2, u2, d2,
                  mix=(attn.reshape(n, aw), mls.reshape(n, mw), w_o))
    return xt.reshape(b, seq, d)
```

```python
import functools

import numpy as np
import jax
import jax.numpy as jnp
from jax import lax
from jax.experimental import pallas as pl
from jax.experimental.pallas import tpu as pltpu

F32 = jnp.float32
BF16 = jnp.bfloat16

LANES = 128
SUBLANES = 8
BF16_ROWS = 16
NORM_EPS = 1e-6
ATTN_HEADS = 8
ATTN_HEAD_DIM = 64
MLSTM_HEADS = 4
MLSTM_HEAD_DIM = 128
CONV_WIDTH = 4
DILATED_CONFIGS = ((128, 1), (512, 4), (2048, 16))
ATTN_BLOCK = 128
MLSTM_CHUNK = 128
NEG_BIG = -1e30
ATTN_GROUP = 4
ATTN_AHEAD = 2
LOG2E = 1.4426950408889634
FFN_TOKENS = 512
INPROJ_TOKENS = 512
CONV_ROWS = 64
CAST_STEPS = 16
VMEM_LIMIT = 56 * 1024 * 1024


def _const_spec(shape):
    nd = len(shape)
    return pl.BlockSpec(shape, lambda *_: (0,) * nd, pipeline_mode=pl.Buffered(1))


def _rmsnorm(x, w):
    ms = jnp.mean(x * x, axis=-1, keepdims=True)
    return x * lax.rsqrt(ms + NORM_EPS) * w


def _dot(a, b):
    return jnp.dot(a, b, preferred_element_type=F32)


def _dot_nt(a, b):
    return lax.dot_general(a, b, (((1,), (1,)), ((), ())), preferred_element_type=F32)


def _split3(x):
    h1 = x.astype(BF16)
    r1 = x - h1.astype(F32)
    h2 = r1.astype(BF16)
    h3 = (r1 - h2.astype(F32)).astype(BF16)
    return h1, h2, h3


def _cast_kernel(*refs):
    half = len(refs) // 2
    for src, dst in zip(refs[:half], refs[half:]):
        dst[...] = src[...].astype(dst.dtype)


def _to_bf16(*ws):
    specs = [pl.BlockSpec((w.shape[0] // CAST_STEPS, w.shape[1]), lambda i: (i, 0)) for w in ws]
    return pl.pallas_call(
        _cast_kernel,
        grid=(CAST_STEPS,),
        in_specs=specs,
        out_specs=specs,
        out_shape=[jax.ShapeDtypeStruct(w.shape, BF16) for w in ws],
        compiler_params=pltpu.CompilerParams(dimension_semantics=("parallel",), vmem_limit_bytes=VMEM_LIMIT),
        name="cast_weights",
    )(*ws)


def _swiglu_half(x, nw_ref, wg_ref, wu_ref, wd_ref):
    h = _rmsnorm(x, nw_ref[...]).astype(BF16)
    g = _dot(h, wg_ref[...])
    u = _dot(h, wu_ref[...])
    a = (g * jax.nn.sigmoid(g) * u).astype(BF16)
    return x + 0.5 * _dot(a, wd_ref[...])


def _ffn_kernel(x_ref, nw_ref, wg_ref, wu_ref, wd_ref, o_ref):
    o_ref[...] = _swiglu_half(x_ref[...], nw_ref, wg_ref, wu_ref, wd_ref)


def _outproj_ffn_kernel(x_ref, ya_ref, ym_ref, wo_ref, nw_ref, wg_ref, wu_ref, wd_ref, o_ref):
    aw = ya_ref.shape[1]
    x = x_ref[...] + _dot(ya_ref[...], wo_ref[0:aw, :]) + _dot(ym_ref[...], wo_ref[aw:, :])
    o_ref[...] = _swiglu_half(x, nw_ref, wg_ref, wu_ref, wd_ref)


def _ffn(x, nw, wg, wu, wd, mix=None):
    n, d = x.shape
    f = wg.shape[1]
    tm = FFN_TOKENS
    tok = lambda w: pl.BlockSpec((tm, w), lambda i: (i, 0))
    w_specs = [_const_spec((1, d)), _const_spec((d, f)), _const_spec((d, f)), _const_spec((f, d))]
    if mix is None:
        body, ins, specs = _ffn_kernel, (x,), [tok(d)]
    else:
        ya, ym, wo = mix
        body, ins = _outproj_ffn_kernel, (x, ya, ym, wo)
        specs = [tok(d), tok(ya.shape[1]), tok(ym.shape[1]), _const_spec(wo.shape)]
    return pl.pallas_call(
        body,
        grid=(n // tm,),
        in_specs=specs + w_specs,
        out_specs=tok(d),
        out_shape=jax.ShapeDtypeStruct((n, d), F32),
        compiler_params=pltpu.CompilerParams(dimension_semantics=("parallel",), vmem_limit_bytes=VMEM_LIMIT),
        name="ffn" if mix is None else "outproj_ffn",
    )(*ins, nw, wg, wu, wd)


def _inproj_kernel(tiles_per_seq, x_ref, nw_ref, w_ref, wgr_ref, gbr_ref, qnw_ref, knw_ref, cw_ref, cb_ref,
                   qa_ref, ka_ref, va_ref, qm_ref, km_ref, vm_ref, om_ref, gr_ref, cpad):
    tm = x_ref.shape[0]
    aw = qa_ref.shape[1]
    mw = qm_ref.shape[1]
    h = _rmsnorm(x_ref[...], nw_ref[...]).astype(BF16)
    first = lax.broadcasted_iota(jnp.int32, (1, LANES), 1) < ATTN_HEAD_DIM

    def head_norm(dst, p, w_row, scale):
        for c in range(0, aw, LANES):
            ps = p[:, c:c + LANES]
            sq = ps * ps
            ss0 = jnp.sum(jnp.where(first, sq, 0.0), axis=-1, keepdims=True)
            ss1 = jnp.sum(jnp.where(first, 0.0, sq), axis=-1, keepdims=True)
            ms = jnp.where(first, ss0, ss1) * (1.0 / ATTN_HEAD_DIM)
            dst[:, c:c + LANES] = ps * lax.rsqrt(ms + NORM_EPS) * (w_row[:, c:c + LANES] * scale)

    head_norm(qa_ref, _dot(h, w_ref[:, 0:aw]), qnw_ref[...], ATTN_HEAD_DIM ** -0.5 * LOG2E)
    head_norm(ka_ref, _dot(h, w_ref[:, aw:2 * aw]), knw_ref[...], 1.0)
    va_ref[...] = _dot(h, w_ref[:, 2 * aw:3 * aw])

    base = 3 * aw
    hist = CONV_WIDTH - 1

    nslab = 2 * mw // LANES

    @pl.when(pl.program_id(0) % tiles_per_seq == 0)
    def _():
        cpad[:, 0:SUBLANES, :] = jnp.zeros((nslab, SUBLANES, LANES), F32)

    raw = _dot(h, w_ref[:, base:base + 2 * mw])
    row0 = jnp.minimum(pl.program_id(0), 0) + (SUBLANES - hist)
    for s in range(nslab):
        lanes = slice(s * LANES, (s + 1) * LANES)
        cpad[s, SUBLANES:, :] = raw[:, lanes]
        for r in range(0, tm, CONV_ROWS):
            y = cb_ref[:, lanes]
            for j in range(CONV_WIDTH):
                y = y + cpad[s, pl.ds(row0 + r + j, CONV_ROWS), :] * cw_ref[j:j + 1, lanes]
            y = y * jax.nn.sigmoid(y)
            if s * LANES < mw:
                qm_ref[r:r + CONV_ROWS, lanes] = y.astype(BF16)
            else:
                y = y * (MLSTM_HEAD_DIM ** -0.5)
                km_ref[r:r + CONV_ROWS, s * LANES - mw:(s + 1) * LANES - mw] = y.astype(BF16)
        cpad[s, 0:SUBLANES, :] = cpad[s, tm:tm + SUBLANES, :]

    vm_ref[...] = _dot(h, w_ref[:, base + 2 * mw:base + 3 * mw]).astype(BF16)
    om_ref[...] = jax.nn.sigmoid(_dot(h, w_ref[:, base + 3 * mw:base + 4 * mw])).astype(BF16)
    ng = gr_ref.shape[0]
    gr_ref[...] = _dot_nt(wgr_ref[...], h)[:ng, :] + gbr_ref[...]


def _inproj(x, seq, nw, w_main, w_gate, gate_bias, qnw, knw, conv_w, conv_b):
    n, d = x.shape
    tm = INPROJ_TOKENS
    aw = ATTN_HEADS * ATTN_HEAD_DIM
    mw = MLSTM_HEADS * MLSTM_HEAD_DIM
    ng = 2 * MLSTM_HEADS
    wgr = jnp.zeros((BF16_ROWS, d), BF16).at[:ng, :].set(w_gate.T)
    gbr = gate_bias.reshape(ng, 1)
    qnw_t = jnp.tile(qnw, ATTN_HEADS).reshape(1, aw)
    knw_t = jnp.tile(knw, ATTN_HEADS).reshape(1, aw)
    tok = lambda w: pl.BlockSpec((tm, w), lambda i: (i, 0))
    out_shape = ([jax.ShapeDtypeStruct((n, aw), F32)] * 3 + [jax.ShapeDtypeStruct((n, mw), BF16)] * 4
                 + [jax.ShapeDtypeStruct((ng, n), F32)])
    out_specs = [tok(aw)] * 3 + [tok(mw)] * 4 + [pl.BlockSpec((ng, tm), lambda i: (0, i))]
    return pl.pallas_call(
        functools.partial(_inproj_kernel, seq // tm),
        grid=(n // tm,),
        in_specs=[tok(d), _const_spec((1, d)), _const_spec(w_main.shape), _const_spec(wgr.shape),
                  _const_spec(gbr.shape), _const_spec((1, aw)), _const_spec((1, aw)),
                  _const_spec(conv_w.shape), _const_spec((1, 2 * mw))],
        out_specs=out_specs,
        out_shape=out_shape,
        scratch_shapes=[pltpu.VMEM((2 * mw // LANES, tm + SUBLANES, LANES), F32)],
        compiler_params=pltpu.CompilerParams(dimension_semantics=("arbitrary",), vmem_limit_bytes=VMEM_LIMIT),
        name="in_proj",
    )(x, nw.reshape(1, d), w_main, wgr, gbr, qnw_t, knw_t, conv_w, conv_b.reshape(1, 2 * mw))


def _attn_kernel(slopes_ref, q_ref, k_ref, v_ref, gain_ref, o_ref, qs, kk, vv, x4f, part):
    blk = ATTN_BLOCK
    seq = q_ref.shape[1]
    nblk = seq // blk
    grp = ATTN_GROUP
    assert [d for _, d in DILATED_CONFIGS] == [1, grp, grp * grp] and nblk == grp * grp
    pair = pl.program_id(1)
    lane = lax.broadcasted_iota(jnp.int32, (blk, LANES), 1)
    first = lane < ATTN_HEAD_DIM
    slope = (slopes_ref[2 * pair], slopes_ref[2 * pair + 1])
    refs = (q_ref, k_ref, v_ref)

    def stage(order, j, q, k, v):
        qs[order, 2 * j * blk:(2 * j + 1) * blk, :] = jnp.where(first, q, 0.0).astype(BF16)
        qs[order, (2 * j + 1) * blk:(2 * j + 2) * blk, :] = jnp.where(first, 0.0, q).astype(BF16)
        kk[order, j * blk:(j + 1) * blk, :] = k.astype(BF16)
        vv[order, j * blk:(j + 1) * blk, :] = v.astype(BF16)

    for j in range(nblk):
        stage(0, j, *(ref[0, j * blk:(j + 1) * blk, :] for ref in refs))
    for j in range(nblk):
        r, c = divmod(j, grp)
        x = [ref[0, pl.ds(c * blk * grp + r, blk, stride=grp), :] for ref in refs]
        for i in range(3):
            x4f[i, j * blk:(j + 1) * blk, :] = x[i]
        stage(1, j, *x)
    for j in range(nblk):
        start = (j % grp) * (seq // grp) + j // grp
        stage(2, j, *(x4f[i, pl.ds(start, blk, stride=grp), :] for i in range(3)))

    def make_bias(dil, n_steps):
        qi = lax.broadcasted_iota(jnp.int32, (blk, 2 * blk), 0)
        ki = lax.broadcasted_iota(jnp.int32, (blk, 2 * blk), 1)
        steps = qi + blk - ki
        band = (steps >= 0) & (steps <= n_steps)
        dist = (steps * dil).astype(F32) * LOG2E
        return jnp.concatenate([jnp.where(band, -slope[h] * dist, NEG_BIG) for h in range(2)], axis=0)

    ones = jnp.ones((2 * blk, LANES), BF16)

    def scores(order, off, kcat, bias):
        s = _dot_nt(qs[order, pl.ds(2 * off, 2 * blk), :], kcat) + bias
        m = jnp.max(s, axis=-1, keepdims=True)
        return m, jnp.exp2(s - m).astype(BF16)

    def weighted(m, p, vcat):
        r = _dot(p, jnp.concatenate([vcat, ones[:vcat.shape[0]]], axis=1))
        return (jnp.where(first, m[:blk], m[blk:]),
                jnp.where(first, r[:blk, LANES:], r[blk:, LANES:]),
                jnp.where(first, r[:blk, :LANES], r[blk:, :LANES]))

    def merge(a, b):
        m = jnp.maximum(a[0], b[0])
        wa = jnp.exp2(a[0] - m)
        wb = jnp.exp2(b[0] - m)
        return m, a[1] * wa + b[1] * wb, a[2] * wa + b[2] * wb

    def load_part(stage_i, idx):
        return tuple(part[stage_i, i, idx, :] for i in range(3))

    def store_part(stage_i, idx, val):
        for i in range(3):
            part[stage_i, i, idx, :] = val[i]

    (w1, d1), (w4, d4), (w16, d16) = DILATED_CONFIGS
    biases = (make_bias(d1, w1 // d1), make_bias(d4, w4 // d4), make_bias(d16, w16 // d16))

    def finish(order, j, m, p, vcat):
        g, c = divmod(j, grp)
        cur = pl.ds(j * blk, blk)
        res = weighted(m, p, vcat)
        if order == 2:
            store_part(0, pl.ds(c * (seq // grp) + g, blk, stride=grp), res)
        elif order == 1:
            res = merge(res, load_part(0, cur))
            store_part(1, pl.ds(c * blk * grp + g, blk, stride=grp), res)
        else:
            _, l_tot, a_tot = merge(res, load_part(1, cur))
            o = a_tot / l_tot
            sq = o * o
            ss0 = jnp.sum(jnp.where(first, sq, 0.0), axis=-1, keepdims=True)
            ss1 = jnp.sum(jnp.where(first, 0.0, sq), axis=-1, keepdims=True)
            ms = jnp.where(first, ss0, ss1) * (1.0 / ATTN_HEAD_DIM)
            o_ref[0, cur, :] = (o * lax.rsqrt(ms + NORM_EPS) * gain_ref[...]).astype(o_ref.dtype)

    pending = []
    for order, n_class in ((2, 1), (1, grp), (0, nblk)):
        bias = biases[order]
        k_prev = v_prev = None
        for j in range(nblk):
            cur = pl.ds(j * blk, blk)
            k_cur, v_cur = kk[order, cur, :], vv[order, cur, :]
            if j % n_class == 0:
                kcat, vcat, b_use = k_cur, v_cur, bias[:, blk:]
            else:
                kcat = jnp.concatenate([k_prev, k_cur], axis=0)
                vcat = jnp.concatenate([v_prev, v_cur], axis=0)
                b_use = bias
            k_prev, v_prev = k_cur, v_cur
            pending.append((order, j) + scores(order, j * blk, kcat, b_use) + (vcat,))
            if len(pending) > ATTN_AHEAD:
                finish(*pending.pop(0))
    while pending:
        finish(*pending.pop(0))


def _attention(qa, ka, va, gain, slopes):
    b, seq, aw = qa.shape
    pairs = aw // LANES
    nbr = len(DILATED_CONFIGS)
    blk = pl.BlockSpec((1, seq, LANES), lambda i, p: (i, 0, p))
    return pl.pallas_call(
        _attn_kernel,
        grid=(b, pairs),
        in_specs=[pl.BlockSpec(memory_space=pltpu.SMEM), blk, blk, blk,
                  pl.BlockSpec((1, LANES), lambda i, p: (0, p))],
        out_specs=blk,
        out_shape=jax.ShapeDtypeStruct((b, seq, aw), BF16),
        scratch_shapes=[pltpu.VMEM((nbr, 2 * seq, LANES), BF16),
                        pltpu.VMEM((nbr, seq, LANES), BF16),
                        pltpu.VMEM((nbr, seq, LANES), BF16),
                        pltpu.VMEM((3, seq, LANES), F32),
                        pltpu.VMEM((2, 3, seq, LANES), F32)],
        compiler_params=pltpu.CompilerParams(dimension_semantics=("parallel", "parallel"),
                                             vmem_limit_bytes=VMEM_LIMIT),
        name="dilated_attention",
    )(slopes, qa, ka, va, gain.reshape(1, aw))


def _split2(x):
    h1 = x.astype(BF16).astype(F32)
    return h1, (x - h1).astype(BF16).astype(F32)


def _mlstm_kernel(q_ref, k_ref, v_ref, og_ref, g_ref, gain_ref, o_ref, rowq, colq, lhs_s, guard_s, upd_s):
    chunk = MLSTM_CHUNK
    dh = MLSTM_HEAD_DIM
    seq = q_ref.shape[1]
    heads = q_ref.shape[2] // dh
    nc = seq // chunk
    nr = nc * heads
    assert chunk == LANES and dh == LANES

    ti = lax.broadcasted_iota(jnp.int32, (chunk, chunk), 0)
    si = lax.broadcasted_iota(jnp.int32, (chunk, chunk), 1)
    causal = si <= ti
    eye = (si == ti).astype(BF16)
    eye2 = jnp.concatenate([eye, eye], axis=1)
    tri_t = (ti <= si).astype(BF16)
    lane = lax.broadcasted_iota(jnp.int32, (nr, chunk), 1)
    row = lax.broadcasted_iota(jnp.int32, (nr, chunk), 0)

    i_pre = g_ref[0, 0] * LOG2E
    f_pre = g_ref[0, 1]
    lf = (jnp.minimum(f_pre, 0.0) - jnp.log1p(jnp.exp(-jnp.abs(f_pre)))) * LOG2E
    b = sum(_dot(part, tri_t) for part in _split3(lf))
    u = i_pre - b
    cm = u
    span = 1
    while span < chunk:
        nxt = min(span * 8, chunk)
        terms = [jnp.where(lane >= sh, pltpu.roll(cm, sh, axis=1), cm) for sh in range(span, nxt, span)]
        cm = functools.reduce(jnp.maximum, terms, cm)
        span = nxt
    b_last = jnp.broadcast_to(b[:, chunk - 1:chunk], (nr, chunk))
    u_max = jnp.broadcast_to(cm[:, chunk - 1:chunk], (nr, chunk))
    m = jnp.zeros((heads, chunk), F32)
    m_start = jnp.zeros((nr, chunk), F32)
    for c in range(nc - 1):
        rows_c = slice(c * heads, (c + 1) * heads)
        m = b_last[rows_c, :] + jnp.maximum(m, u_max[rows_c, :])
        m_start = jnp.where(row // heads == c + 1, jnp.concatenate([m] * nc, axis=0), m_start)
    m_next = b_last + jnp.maximum(m_start, u_max)
    z1, z2 = _split2(jnp.maximum(m_start, cm))
    z = z1 + z2
    r1, r2 = _split2(b + z)
    rowq[0] = u
    rowq[1] = m_start
    rowq[2] = jnp.exp2(b_last + u - m_next)
    rowq[3] = jnp.exp2(b_last + m_start - m_next)
    colq[0] = jnp.concatenate([z1, z2], axis=1)
    colq[1] = jnp.concatenate([r1, r2], axis=1)

    ones = jnp.ones((chunk, dh), BF16)

    for h in range(heads):
        cols = slice(h * dh, (h + 1) * dh)

        def independent(c, carry, h=h, cols=cols):
            rows = pl.ds(pl.multiple_of(c * chunk, chunk), chunk)
            gate_row = pl.ds(c * heads + h, 1)
            qb, kb = q_ref[0, rows, cols], k_ref[0, rows, cols]
            v_ext = jnp.concatenate([v_ref[0, rows, cols], ones], axis=1)
            tiles = []
            for j in range(2):
                v = jnp.broadcast_to(colq[j, gate_row, :], (BF16_ROWS, 2 * chunk)).astype(BF16)
                tiles.extend([v] * (chunk // BF16_ROWS))
            colf = _dot_nt(eye2, jnp.concatenate(tiles, axis=0))
            z_col, mrow_col = colf[:, :chunk], colf[:, chunk:]
            dw = jnp.where(causal, jnp.exp2(rowq[0, gate_row, :] - z_col), 0.0)
            gw = jnp.exp2(rowq[1, gate_row, :] - z_col)
            sc = _dot_nt(qb, kb) * dw
            lhs_s[rows, :] = jnp.concatenate([(gw * qb.astype(F32)).astype(BF16), sc.astype(BF16)], axis=1)
            guard_s[rows, :] = jnp.exp2(-mrow_col)
            wk_t = (kb.astype(F32).T * rowq[2, gate_row, :]).astype(BF16)
            upd_s[rows, :] = _dot(wk_t, v_ext)
            return carry

        lax.fori_loop(0, nc, independent, 0, unroll=16)

        def recurrent(c, state, h=h, cols=cols):
            rows = pl.ds(pl.multiple_of(c * chunk, chunk), chunk)
            v_ext = jnp.concatenate([v_ref[0, rows, cols], ones], axis=1)
            nd = _dot(lhs_s[rows, :], jnp.concatenate([state.astype(BF16), v_ext], axis=0))
            hidden = nd[:, :dh] / jnp.maximum(jnp.abs(nd[:, dh:]), guard_s[rows, :])
            gated = og_ref[0, rows, cols].astype(F32) * hidden
            o_ref[0, rows, cols] = _rmsnorm(gated, gain_ref[:, cols]).astype(o_ref.dtype)
            dec = rowq[3, pl.ds(c * heads + h, 1), :]
            return jnp.concatenate([dec, dec], axis=1) * state + upd_s[rows, :]

        lax.fori_loop(0, nc, recurrent, jnp.zeros((dh, 2 * dh), F32), unroll=16)


def _mlstm(qm, km, vm, om, gates, gain):
    b, seq, mw = qm.shape
    dh = MLSTM_HEAD_DIM
    nr = gates.shape[2]
    blk = pl.BlockSpec((1, seq, mw), lambda i: (i, 0, 0))
    return pl.pallas_call(
        _mlstm_kernel,
        grid=(b,),
        in_specs=[blk, blk, blk, blk,
                  pl.BlockSpec((1, 2, nr, MLSTM_CHUNK), lambda i: (i, 0, 0, 0)),
                  _const_spec((1, mw))],
        out_specs=blk,
        out_shape=jax.ShapeDtypeStruct((b, seq, mw), BF16),
        scratch_shapes=[pltpu.VMEM((4, nr, MLSTM_CHUNK), F32),
                        pltpu.VMEM((2, nr, 2 * MLSTM_CHUNK), F32),
                        pltpu.VMEM((seq, 2 * dh), BF16),
                        pltpu.VMEM((seq, dh), F32),
                        pltpu.VMEM((seq, 2 * dh), F32)],
        compiler_params=pltpu.CompilerParams(dimension_semantics=("parallel",), vmem_limit_bytes=VMEM_LIMIT),
        name="mlstm",
    )(qm, km, vm, om, gates, gain.reshape(1, mw))


def kernel(x, ffn1_norm_w, ffn1_w_gate, ffn1_w_up, ffn1_w_down, mix_norm_w, w_in, q_norm_w, k_norm_w, conv_w,
           conv_b, i_bias, f_bias, attn_out_gain, mlstm_out_gain, w_out, ffn2_norm_w, ffn2_w_gate, ffn2_w_up,
           ffn2_w_down):
    b, seq, d = x.shape
    n = b * seq
    aw = ATTN_HEADS * ATTN_HEAD_DIM
    mw = MLSTM_HEADS * MLSTM_HEAD_DIM
    n_main = 3 * aw + 4 * mw
    hm = MLSTM_HEADS
    slopes = jnp.asarray(2.0 ** (-8.0 * np.arange(1, ATTN_HEADS + 1, dtype=np.float32) / ATTN_HEADS), F32)
    xt = x.reshape(n, d)
    for l in range(ffn1_norm_w.shape[0]):
        g1, u1, d1, w_i, w_o, g2, u2, d2 = _to_bf16(ffn1_w_gate[l], ffn1_w_up[l], ffn1_w_down[l], w_in[l], w_out[l],
                                                    ffn2_w_gate[l], ffn2_w_up[l], ffn2_w_down[l])
        xt = _ffn(xt, ffn1_norm_w[l].reshape(1, d), g1, u1, d1)
        gate_bias = jnp.concatenate([i_bias[l], f_bias[l]]).astype(F32)
        qa, ka, va, qm, km, vm, om, gr = _inproj(xt, seq, mix_norm_w[l], w_i, w_in[l][:, n_main:].astype(BF16),
                                                 gate_bias, q_norm_w[l], k_norm_w[l], conv_w[l], conv_b[l])
        shp = lambda t: t.reshape(b, seq, -1)
        attn = _attention(shp(qa), shp(ka), shp(va), attn_out_gain[l], slopes)
        gates = gr.reshape(2, hm, b, seq // MLSTM_CHUNK, MLSTM_CHUNK).transpose(2, 0, 3, 1, 4)
        gates = gates.reshape(b, 2, (seq // MLSTM_CHUNK) * hm, MLSTM_CHUNK)
        mls = _mlstm(shp(qm), shp(km), shp(vm), shp(om), gates, mlstm_out_gain[l])
        xt = _ffn(xt, ffn2_norm_w[l].reshape(1, d), g2, u2, d2,
                  mix=(attn.reshape(n, aw), mls.reshape(n, mw), w_o))
    return xt.reshape(b, seq, d)
```

```python
import functools

import numpy as np
import jax
import jax.numpy as jnp
from jax import lax
from jax.experimental import pallas as pl
from jax.experimental.pallas import tpu as pltpu

F32 = jnp.float32
BF16 = jnp.bfloat16

LANES = 128
SUBLANES = 8
BF16_ROWS = 16
NORM_EPS = 1e-6
ATTN_HEADS = 8
ATTN_HEAD_DIM = 64
MLSTM_HEADS = 4
MLSTM_HEAD_DIM = 128
CONV_WIDTH = 4
DILATED_CONFIGS = ((128, 1), (512, 4), (2048, 16))
ATTN_BLOCK = 128
MLSTM_CHUNK = 128
NEG_BIG = -1e30
ATTN_GROUP = 4
ATTN_AHEAD = (2, 2, 4)
LOG2E = 1.4426950408889634
FFN_TOKENS = 512
INPROJ_TOKENS = 512
CONV_ROWS = 64
CAST_STEPS = 16
VMEM_LIMIT = 56 * 1024 * 1024


def _const_spec(shape):
    nd = len(shape)
    return pl.BlockSpec(shape, lambda *_: (0,) * nd, pipeline_mode=pl.Buffered(1))


def _rmsnorm(x, w):
    ms = jnp.mean(x * x, axis=-1, keepdims=True)
    return x * lax.rsqrt(ms + NORM_EPS) * w


def _dot(a, b):
    return jnp.dot(a, b, preferred_element_type=F32)


def _dot_nt(a, b):
    return lax.dot_general(a, b, (((1,), (1,)), ((), ())), preferred_element_type=F32)


def _split3(x):
    h1 = x.astype(BF16)
    r1 = x - h1.astype(F32)
    h2 = r1.astype(BF16)
    h3 = (r1 - h2.astype(F32)).astype(BF16)
    return h1, h2, h3


def _cast_kernel(*refs):
    half = len(refs) // 2
    for src, dst in zip(refs[:half], refs[half:]):
        dst[...] = src[...].astype(dst.dtype)


def _to_bf16(*ws):
    specs = [pl.BlockSpec((w.shape[0] // CAST_STEPS, w.shape[1]), lambda i: (i, 0)) for w in ws]
    return pl.pallas_call(
        _cast_kernel,
        grid=(CAST_STEPS,),
        in_specs=specs,
        out_specs=specs,
        out_shape=[jax.ShapeDtypeStruct(w.shape, BF16) for w in ws],
        compiler_params=pltpu.CompilerParams(dimension_semantics=("parallel",), vmem_limit_bytes=VMEM_LIMIT),
        name="cast_weights",
    )(*ws)


def _swiglu_half(x, nw_ref, wg_ref, wu_ref, wd_ref):
    h = _rmsnorm(x, nw_ref[...]).astype(BF16)
    g = _dot(h, wg_ref[...])
    u = _dot(h, wu_ref[...])
    a = (g * jax.nn.sigmoid(g) * u).astype(BF16)
    return x + 0.5 * _dot(a, wd_ref[...])


def _ffn_kernel(x_ref, nw_ref, wg_ref, wu_ref, wd_ref, o_ref):
    o_ref[...] = _swiglu_half(x_ref[...], nw_ref, wg_ref, wu_ref, wd_ref)


def _outproj_ffn_kernel(x_ref, ya_ref, ym_ref, wo_ref, nw_ref, wg_ref, wu_ref, wd_ref, o_ref):
    aw = ya_ref.shape[1]
    x = x_ref[...] + _dot(ya_ref[...], wo_ref[0:aw, :]) + _dot(ym_ref[...], wo_ref[aw:, :])
    o_ref[...] = _swiglu_half(x, nw_ref, wg_ref, wu_ref, wd_ref)


def _ffn(x, nw, wg, wu, wd, mix=None):
    n, d = x.shape
    f = wg.shape[1]
    tm = FFN_TOKENS
    tok = lambda w: pl.BlockSpec((tm, w), lambda i: (i, 0))
    w_specs = [_const_spec((1, d)), _const_spec((d, f)), _const_spec((d, f)), _const_spec((f, d))]
    if mix is None:
        body, ins, specs = _ffn_kernel, (x,), [tok(d)]
    else:
        ya, ym, wo = mix
        body, ins = _outproj_ffn_kernel, (x, ya, ym, wo)
        specs = [tok(d), tok(ya.shape[1]), tok(ym.shape[1]), _const_spec(wo.shape)]
    return pl.pallas_call(
        body,
        grid=(n // tm,),
        in_specs=specs + w_specs,
        out_specs=tok(d),
        out_shape=jax.ShapeDtypeStruct((n, d), F32),
        compiler_params=pltpu.CompilerParams(dimension_semantics=("parallel",), vmem_limit_bytes=VMEM_LIMIT),
        name="ffn" if mix is None else "outproj_ffn",
    )(*ins, nw, wg, wu, wd)


def _inproj_kernel(tiles_per_seq, x_ref, nw_ref, w_ref, wgr_ref, gbr_ref, qnw_ref, knw_ref, cw_ref, cb_ref,
                   qa_ref, ka_ref, va_ref, qm_ref, km_ref, vm_ref, om_ref, gr_ref, cpad):
    tm = x_ref.shape[0]
    aw = qa_ref.shape[1]
    mw = qm_ref.shape[1]
    h = _rmsnorm(x_ref[...], nw_ref[...]).astype(BF16)
    first = lax.broadcasted_iota(jnp.int32, (1, LANES), 1) < ATTN_HEAD_DIM

    def head_norm(dst, p, w_row, scale):
        for c in range(0, aw, LANES):
            ps = p[:, c:c + LANES]
            sq = ps * ps
            ss0 = jnp.sum(jnp.where(first, sq, 0.0), axis=-1, keepdims=True)
            ss1 = jnp.sum(jnp.where(first, 0.0, sq), axis=-1, keepdims=True)
            ms = jnp.where(first, ss0, ss1) * (1.0 / ATTN_HEAD_DIM)
            dst[:, c:c + LANES] = ps * lax.rsqrt(ms + NORM_EPS) * (w_row[:, c:c + LANES] * scale)

    head_norm(qa_ref, _dot(h, w_ref[:, 0:aw]), qnw_ref[...], ATTN_HEAD_DIM ** -0.5 * LOG2E)
    head_norm(ka_ref, _dot(h, w_ref[:, aw:2 * aw]), knw_ref[...], 1.0)
    va_ref[...] = _dot(h, w_ref[:, 2 * aw:3 * aw])

    base = 3 * aw
    hist = CONV_WIDTH - 1

    nslab = 2 * mw // LANES

    @pl.when(pl.program_id(0) % tiles_per_seq == 0)
    def _():
        cpad[:, 0:SUBLANES, :] = jnp.zeros((nslab, SUBLANES, LANES), F32)

    raw = _dot(h, w_ref[:, base:base + 2 * mw])
    row0 = jnp.minimum(pl.program_id(0), 0) + (SUBLANES - hist)
    for s in range(nslab):
        lanes = slice(s * LANES, (s + 1) * LANES)
        cpad[s, SUBLANES:, :] = raw[:, lanes]
        for r in range(0, tm, CONV_ROWS):
            y = cb_ref[:, lanes]
            for j in range(CONV_WIDTH):
                y = y + cpad[s, pl.ds(row0 + r + j, CONV_ROWS), :] * cw_ref[j:j + 1, lanes]
            y = y * jax.nn.sigmoid(y)
            if s * LANES < mw:
                qm_ref[r:r + CONV_ROWS, lanes] = y.astype(BF16)
            else:
                y = y * (MLSTM_HEAD_DIM ** -0.5)
                km_ref[r:r + CONV_ROWS, s * LANES - mw:(s + 1) * LANES - mw] = y.astype(BF16)
        cpad[s, 0:SUBLANES, :] = cpad[s, tm:tm + SUBLANES, :]

    vm_ref[...] = _dot(h, w_ref[:, base + 2 * mw:base + 3 * mw]).astype(BF16)
    om_ref[...] = jax.nn.sigmoid(_dot(h, w_ref[:, base + 3 * mw:base + 4 * mw])).astype(BF16)
    ng = gr_ref.shape[0]
    gr_ref[...] = _dot_nt(wgr_ref[...], h)[:ng, :] + gbr_ref[...]


def _inproj(x, seq, nw, w_main, w_gate, gate_bias, qnw, knw, conv_w, conv_b):
    n, d = x.shape
    tm = INPROJ_TOKENS
    aw = ATTN_HEADS * ATTN_HEAD_DIM
    mw = MLSTM_HEADS * MLSTM_HEAD_DIM
    ng = 2 * MLSTM_HEADS
    wgr = jnp.zeros((BF16_ROWS, d), BF16).at[:ng, :].set(w_gate.T)
    gbr = gate_bias.reshape(ng, 1)
    qnw_t = jnp.tile(qnw, ATTN_HEADS).reshape(1, aw)
    knw_t = jnp.tile(knw, ATTN_HEADS).reshape(1, aw)
    tok = lambda w: pl.BlockSpec((tm, w), lambda i: (i, 0))
    out_shape = ([jax.ShapeDtypeStruct((n, aw), F32)] * 3 + [jax.ShapeDtypeStruct((n, mw), BF16)] * 4
                 + [jax.ShapeDtypeStruct((ng, n), F32)])
    out_specs = [tok(aw)] * 3 + [tok(mw)] * 4 + [pl.BlockSpec((ng, tm), lambda i: (0, i))]
    return pl.pallas_call(
        functools.partial(_inproj_kernel, seq // tm),
        grid=(n // tm,),
        in_specs=[tok(d), _const_spec((1, d)), _const_spec(w_main.shape), _const_spec(wgr.shape),
                  _const_spec(gbr.shape), _const_spec((1, aw)), _const_spec((1, aw)),
                  _const_spec(conv_w.shape), _const_spec((1, 2 * mw))],
        out_specs=out_specs,
        out_shape=out_shape,
        scratch_shapes=[pltpu.VMEM((2 * mw // LANES, tm + SUBLANES, LANES), F32)],
        compiler_params=pltpu.CompilerParams(dimension_semantics=("arbitrary",), vmem_limit_bytes=VMEM_LIMIT),
        name="in_proj",
    )(x, nw.reshape(1, d), w_main, wgr, gbr, qnw_t, knw_t, conv_w, conv_b.reshape(1, 2 * mw))


def _attn_kernel(slopes_ref, q_ref, k_ref, v_ref, gain_ref, o_ref, qs, kk, vv, x4f, part):
    blk = ATTN_BLOCK
    seq = q_ref.shape[1]
    nblk = seq // blk
    grp = ATTN_GROUP
    assert [d for _, d in DILATED_CONFIGS] == [1, grp, grp * grp] and nblk == grp * grp
    pair = pl.program_id(1)
    lane = lax.broadcasted_iota(jnp.int32, (blk, LANES), 1)
    first = lane < ATTN_HEAD_DIM
    slope = (slopes_ref[2 * pair], slopes_ref[2 * pair + 1])
    refs = (q_ref, k_ref, v_ref)

    def stage(order, j, q, k, v):
        qs[order, 2 * j * blk:(2 * j + 1) * blk, :] = jnp.where(first, q, 0.0).astype(BF16)
        qs[order, (2 * j + 1) * blk:(2 * j + 2) * blk, :] = jnp.where(first, 0.0, q).astype(BF16)
        kk[order, j * blk:(j + 1) * blk, :] = k.astype(BF16)
        vv[order, j * blk:(j + 1) * blk, :] = v.astype(BF16)

    for j in range(nblk):
        stage(0, j, *(ref[0, j * blk:(j + 1) * blk, :] for ref in refs))
    for j in range(nblk):
        r, c = divmod(j, grp)
        x = [ref[0, pl.ds(c * blk * grp + r, blk, stride=grp), :] for ref in refs]
        for i in range(3):
            x4f[i, j * blk:(j + 1) * blk, :] = x[i]
        stage(1, j, *x)
    for j in range(nblk):
        start = (j % grp) * (seq // grp) + j // grp
        stage(2, j, *(x4f[i, pl.ds(start, blk, stride=grp), :] for i in range(3)))

    def make_bias(dil, n_steps):
        qi = lax.broadcasted_iota(jnp.int32, (blk, 2 * blk), 0)
        ki = lax.broadcasted_iota(jnp.int32, (blk, 2 * blk), 1)
        steps = qi + blk - ki
        band = (steps >= 0) & (steps <= n_steps)
        dist = (steps * dil).astype(F32) * LOG2E
        return jnp.concatenate([jnp.where(band, -slope[h] * dist, NEG_BIG) for h in range(2)], axis=0)

    ones = jnp.ones((2 * blk, LANES), BF16)

    def scores(order, off, kcat, bias):
        s = _dot_nt(qs[order, pl.ds(2 * off, 2 * blk), :], kcat) + bias
        m = jnp.max(s, axis=-1, keepdims=True)
        return m, jnp.exp2(s - m).astype(BF16)

    def weighted(m, p, vcat):
        r = _dot(p, jnp.concatenate([vcat, ones[:vcat.shape[0]]], axis=1))
        return (jnp.where(first, m[:blk], m[blk:]),
                jnp.where(first, r[:blk, LANES:], r[blk:, LANES:]),
                jnp.where(first, r[:blk, :LANES], r[blk:, :LANES]))

    def merge(a, b):
        m = jnp.maximum(a[0], b[0])
        wa = jnp.exp2(a[0] - m)
        wb = jnp.exp2(b[0] - m)
        return m, a[1] * wa + b[1] * wb, a[2] * wa + b[2] * wb

    def load_part(stage_i, idx):
        return tuple(part[stage_i, i, idx, :] for i in range(3))

    def store_part(stage_i, idx, val):
        for i in range(3):
            part[stage_i, i, idx, :] = val[i]

    (w1, d1), (w4, d4), (w16, d16) = DILATED_CONFIGS
    biases = (make_bias(d1, w1 // d1), make_bias(d4, w4 // d4), make_bias(d16, w16 // d16))

    def finish(order, j, m, p, vcat):
        g, c = divmod(j, grp)
        cur = pl.ds(j * blk, blk)
        res = weighted(m, p, vcat)
        if order == 2:
            store_part(0, pl.ds(c * (seq // grp) + g, blk, stride=grp), res)
        elif order == 1:
            res = merge(res, load_part(0, cur))
            store_part(1, pl.ds(c * blk * grp + g, blk, stride=grp), res)
        else:
            _, l_tot, a_tot = merge(res, load_part(1, cur))
            o = a_tot / l_tot
            sq = o * o
            ss0 = jnp.sum(jnp.where(first, sq, 0.0), axis=-1, keepdims=True)
            ss1 = jnp.sum(jnp.where(first, 0.0, sq), axis=-1, keepdims=True)
            ms = jnp.where(first, ss0, ss1) * (1.0 / ATTN_HEAD_DIM)
            o_ref[0, cur, :] = (o * lax.rsqrt(ms + NORM_EPS) * gain_ref[...]).astype(o_ref.dtype)

    pending = []
    for order, n_class in ((2, 1), (1, grp), (0, nblk)):
        bias = biases[order]
        k_prev = v_prev = None
        for j in range(nblk):
            cur = pl.ds(j * blk, blk)
            k_cur, v_cur = kk[order, cur, :], vv[order, cur, :]
            if j % n_class == 0:
                kcat, vcat, b_use = k_cur, v_cur, bias[:, blk:]
            else:
                kcat = jnp.concatenate([k_prev, k_cur], axis=0)
                vcat = jnp.concatenate([v_prev, v_cur], axis=0)
                b_use = bias
            k_prev, v_prev = k_cur, v_cur
            pending.append((order, j) + scores(order, j * blk, kcat, b_use) + (vcat,))
            while len(pending) > ATTN_AHEAD[order]:
                finish(*pending.pop(0))
    while pending:
        finish(*pending.pop(0))


def _attention(qa, ka, va, gain, slopes):
    b, seq, aw = qa.shape
    pairs = aw // LANES
    nbr = len(DILATED_CONFIGS)
    blk = pl.BlockSpec((1, seq, LANES), lambda i, p: (i, 0, p))
    return pl.pallas_call(
        _attn_kernel,
        grid=(b, pairs),
        in_specs=[pl.BlockSpec(memory_space=pltpu.SMEM), blk, blk, blk,
                  pl.BlockSpec((1, LANES), lambda i, p: (0, p))],
        out_specs=blk,
        out_shape=jax.ShapeDtypeStruct((b, seq, aw), BF16),
        scratch_shapes=[pltpu.VMEM((nbr, 2 * seq, LANES), BF16),
                        pltpu.VMEM((nbr, seq, LANES), BF16),
                        pltpu.VMEM((nbr, seq, LANES), BF16),
                        pltpu.VMEM((3, seq, LANES), F32),
                        pltpu.VMEM((2, 3, seq, LANES), F32)],
        compiler_params=pltpu.CompilerParams(dimension_semantics=("parallel", "parallel"),
                                             vmem_limit_bytes=VMEM_LIMIT),
        name="dilated_attention",
    )(slopes, qa, ka, va, gain.reshape(1, aw))


def _split2(x):
    h1 = x.astype(BF16).astype(F32)
    return h1, (x - h1).astype(BF16).astype(F32)


def _mlstm_kernel(q_ref, k_ref, v_ref, og_ref, g_ref, gain_ref, o_ref, rowq, colq, lhs_s, guard_s, upd_s):
    chunk = MLSTM_CHUNK
    dh = MLSTM_HEAD_DIM
    seq = q_ref.shape[1]
    heads = q_ref.shape[2] // dh
    nc = seq // chunk
    nr = nc * heads
    assert chunk == LANES and dh == LANES

    ti = lax.broadcasted_iota(jnp.int32, (chunk, chunk), 0)
    si = lax.broadcasted_iota(jnp.int32, (chunk, chunk), 1)
    causal = si <= ti
    eye = (si == ti).astype(BF16)
    eye2 = jnp.concatenate([eye, eye], axis=1)
    tri_t = (ti <= si).astype(BF16)
    lane = lax.broadcasted_iota(jnp.int32, (nr, chunk), 1)
    row = lax.broadcasted_iota(jnp.int32, (nr, chunk), 0)

    i_pre = g_ref[0, 0] * LOG2E
    f_pre = g_ref[0, 1]
    lf = (jnp.minimum(f_pre, 0.0) - jnp.log1p(jnp.exp(-jnp.abs(f_pre)))) * LOG2E
    b = sum(_dot(part, tri_t) for part in _split3(lf))
    u = i_pre - b
    cm = u
    span = 1
    while span < chunk:
        nxt = min(span * 8, chunk)
        terms = [jnp.where(lane >= sh, pltpu.roll(cm, sh, axis=1), cm) for sh in range(span, nxt, span)]
        cm = functools.reduce(jnp.maximum, terms, cm)
        span = nxt
    b_last = jnp.broadcast_to(b[:, chunk - 1:chunk], (nr, chunk))
    u_max = jnp.broadcast_to(cm[:, chunk - 1:chunk], (nr, chunk))
    m = jnp.zeros((heads, chunk), F32)
    m_start = jnp.zeros((nr, chunk), F32)
    for c in range(nc - 1):
        rows_c = slice(c * heads, (c + 1) * heads)
        m = b_last[rows_c, :] + jnp.maximum(m, u_max[rows_c, :])
        m_start = jnp.where(row // heads == c + 1, jnp.concatenate([m] * nc, axis=0), m_start)
    m_next = b_last + jnp.maximum(m_start, u_max)
    z1, z2 = _split2(jnp.maximum(m_start, cm))
    z = z1 + z2
    r1, r2 = _split2(b + z)
    rowq[0] = u
    rowq[1] = m_start
    rowq[2] = jnp.exp2(b_last + u - m_next)
    rowq[3] = jnp.exp2(b_last + m_start - m_next)
    colq[0] = jnp.concatenate([z1, z2], axis=1)
    colq[1] = jnp.concatenate([r1, r2], axis=1)

    ones = jnp.ones((chunk, dh), BF16)

    for h in range(heads):
        cols = slice(h * dh, (h + 1) * dh)

        def independent(c, carry, h=h, cols=cols):
            rows = pl.ds(pl.multiple_of(c * chunk, chunk), chunk)
            gate_row = pl.ds(c * heads + h, 1)
            qb, kb = q_ref[0, rows, cols], k_ref[0, rows, cols]
            v_ext = jnp.concatenate([v_ref[0, rows, cols], ones], axis=1)
            tiles = []
            for j in range(2):
                v = jnp.broadcast_to(colq[j, gate_row, :], (BF16_ROWS, 2 * chunk)).astype(BF16)
                tiles.extend([v] * (chunk // BF16_ROWS))
            colf = _dot_nt(eye2, jnp.concatenate(tiles, axis=0))
            z_col, mrow_col = colf[:, :chunk], colf[:, chunk:]
            dw = jnp.where(causal, jnp.exp2(rowq[0, gate_row, :] - z_col), 0.0)
            gw = jnp.exp2(rowq[1, gate_row, :] - z_col)
            sc = _dot_nt(qb, kb) * dw
            lhs_s[rows, :] = jnp.concatenate([(gw * qb.astype(F32)).astype(BF16), sc.astype(BF16)], axis=1)
            guard_s[rows, :] = jnp.exp2(-mrow_col)
            wk_t = (kb.astype(F32).T * rowq[2, gate_row, :]).astype(BF16)
            upd_s[rows, :] = _dot(wk_t, v_ext)
            return carry

        lax.fori_loop(0, nc, independent, 0, unroll=16)

        def recurrent(c, state, h=h, cols=cols):
            rows = pl.ds(pl.multiple_of(c * chunk, chunk), chunk)
            v_ext = jnp.concatenate([v_ref[0, rows, cols], ones], axis=1)
            nd = _dot(lhs_s[rows, :], jnp.concatenate([state.astype(BF16), v_ext], axis=0))
            hidden = nd[:, :dh] / jnp.maximum(jnp.abs(nd[:, dh:]), guard_s[rows, :])
            gated = og_ref[0, rows, cols].astype(F32) * hidden
            o_ref[0, rows, cols] = _rmsnorm(gated, gain_ref[:, cols]).astype(o_ref.dtype)
            dec = rowq[3, pl.ds(c * heads + h, 1), :]
            return jnp.concatenate([dec, dec], axis=1) * state + upd_s[rows, :]

        lax.fori_loop(0, nc, recurrent, jnp.zeros((dh, 2 * dh), F32), unroll=16)


def _mlstm(qm, km, vm, om, gates, gain):
    b, seq, mw = qm.shape
    dh = MLSTM_HEAD_DIM
    nr = gates.shape[2]
    blk = pl.BlockSpec((1, seq, mw), lambda i: (i, 0, 0))
    return pl.pallas_call(
        _mlstm_kernel,
        grid=(b,),
        in_specs=[blk, blk, blk, blk,
                  pl.BlockSpec((1, 2, nr, MLSTM_CHUNK), lambda i: (i, 0, 0, 0)),
                  _const_spec((1, mw))],
        out_specs=blk,
        out_shape=jax.ShapeDtypeStruct((b, seq, mw), BF16),
        scratch_shapes=[pltpu.VMEM((4, nr, MLSTM_CHUNK), F32),
                        pltpu.VMEM((2, nr, 2 * MLSTM_CHUNK), F32),
                        pltpu.VMEM((seq, 2 * dh), BF16),
                        pltpu.VMEM((seq, dh), F32),
                        pltpu.VMEM((seq, 2 * dh), F32)],
        compiler_params=pltpu.CompilerParams(dimension_semantics=("parallel",), vmem_limit_bytes=VMEM_LIMIT),
        name="mlstm",
    )(qm, km, vm, om, gates, gain.reshape(1, mw))


def kernel(x, ffn1_norm_w, ffn1_w_gate, ffn1_w_up, ffn1_w_down, mix_norm_w, w_in, q_norm_w, k_norm_w, conv_w,
           conv_b, i_bias, f_bias, attn_out_gain, mlstm_out_gain, w_out, ffn2_norm_w, ffn2_w_gate, ffn2_w_up,
           ffn2_w_down):
    b, seq, d = x.shape
    n = b * seq
    aw = ATTN_HEADS * ATTN_HEAD_DIM
    mw = MLSTM_HEADS * MLSTM_HEAD_DIM
    n_main = 3 * aw + 4 * mw
    hm = MLSTM_HEADS
    slopes = jnp.asarray(2.0 ** (-8.0 * np.arange(1, ATTN_HEADS + 1, dtype=np.float32) / ATTN_HEADS), F32)
    xt = x.reshape(n, d)
    for l in range(ffn1_norm_w.shape[0]):
        g1, u1, d1, w_i, w_o, g2, u2, d2 = _to_bf16(ffn1_w_gate[l], ffn1_w_up[l], ffn1_w_down[l], w_in[l], w_out[l],
                                                    ffn2_w_gate[l], ffn2_w_up[l], ffn2_w_down[l])
        xt = _ffn(xt, ffn1_norm_w[l].reshape(1, d), g1, u1, d1)
        gate_bias = jnp.concatenate([i_bias[l], f_bias[l]]).astype(F32)
        qa, ka, va, qm, km, vm, om, gr = _inproj(xt, seq, mix_norm_w[l], w_i, w_in[l][:, n_main:].astype(BF16),
                                                 gate_bias, q_norm_w[l], k_norm_w[l], conv_w[l], conv_b[l])
        shp = lambda t: t.reshape(b, seq, -1)
        attn = _attention(shp(qa), shp(ka), shp(va), attn_out_gain[l], slopes)
        gates = gr.reshape(2, hm, b, seq // MLSTM_CHUNK, MLSTM_CHUNK).transpose(2, 0, 3, 1, 4)
        gates = gates.reshape(b, 2, (seq // MLSTM_CHUNK) * hm, MLSTM_CHUNK)
        mls = _mlstm(shp(qm), shp(km), shp(vm), shp(om), gates, mlstm_out_gain[l])
        xt = _ffn(xt, ffn2_norm_w[l].reshape(1, d), g2, u2, d2,
                  mix=(attn.reshape(n, aw), mls.reshape(n, mw), w_o))
    return xt.reshape(b, seq, d)
```

```python
import functools

import numpy as np
import jax
import jax.numpy as jnp
from jax import lax
from jax.experimental import pallas as pl
from jax.experimental.pallas import tpu as pltpu

F32 = jnp.float32
BF16 = jnp.bfloat16

LANES = 128
SUBLANES = 8
BF16_ROWS = 16
NORM_EPS = 1e-6
ATTN_HEADS = 8
ATTN_HEAD_DIM = 64
MLSTM_HEADS = 4
MLSTM_HEAD_DIM = 128
CONV_WIDTH = 4
DILATED_CONFIGS = ((128, 1), (512, 4), (2048, 16))
ATTN_BLOCK = 128
MLSTM_CHUNK = 128
SCAN_RADIX = 8
NEG_BIG = -1e30
ATTN_GROUP = 4
ATTN_AHEAD = (2, 2, 4)
LOG2E = 1.4426950408889634
FFN_TOKENS = 512
INPROJ_TOKENS = 512
CONV_ROWS = 64
CAST_STEPS = 16
VMEM_LIMIT = 56 * 1024 * 1024


def _const_spec(shape):
    nd = len(shape)
    return pl.BlockSpec(shape, lambda *_: (0,) * nd, pipeline_mode=pl.Buffered(1))


def _rmsnorm(x, w):
    ms = jnp.mean(x * x, axis=-1, keepdims=True)
    return x * lax.rsqrt(ms + NORM_EPS) * w


def _dot(a, b):
    return jnp.dot(a, b, preferred_element_type=F32)


def _dot_nt(a, b):
    return lax.dot_general(a, b, (((1,), (1,)), ((), ())), preferred_element_type=F32)


def _split3(x):
    h1 = x.astype(BF16)
    r1 = x - h1.astype(F32)
    h2 = r1.astype(BF16)
    h3 = (r1 - h2.astype(F32)).astype(BF16)
    return h1, h2, h3


def _cast_kernel(*refs):
    half = len(refs) // 2
    for src, dst in zip(refs[:half], refs[half:]):
        dst[...] = src[...].astype(dst.dtype)


def _to_bf16(*ws):
    specs = [pl.BlockSpec((w.shape[0] // CAST_STEPS, w.shape[1]), lambda i: (i, 0)) for w in ws]
    return pl.pallas_call(
        _cast_kernel,
        grid=(CAST_STEPS,),
        in_specs=specs,
        out_specs=specs,
        out_shape=[jax.ShapeDtypeStruct(w.shape, BF16) for w in ws],
        compiler_params=pltpu.CompilerParams(dimension_semantics=("parallel",), vmem_limit_bytes=VMEM_LIMIT),
        name="cast_weights",
    )(*ws)


def _swiglu_half(x, nw_ref, wg_ref, wu_ref, wd_ref):
    h = _rmsnorm(x, nw_ref[...]).astype(BF16)
    g = _dot(h, wg_ref[...])
    u = _dot(h, wu_ref[...])
    a = (g * jax.nn.sigmoid(g) * u).astype(BF16)
    return x + 0.5 * _dot(a, wd_ref[...])


def _ffn_kernel(x_ref, nw_ref, wg_ref, wu_ref, wd_ref, o_ref):
    o_ref[...] = _swiglu_half(x_ref[...], nw_ref, wg_ref, wu_ref, wd_ref)


def _outproj_ffn_kernel(x_ref, ya_ref, ym_ref, wo_ref, nw_ref, wg_ref, wu_ref, wd_ref, o_ref):
    aw = ya_ref.shape[1]
    x = x_ref[...] + _dot(ya_ref[...], wo_ref[0:aw, :]) + _dot(ym_ref[...], wo_ref[aw:, :])
    o_ref[...] = _swiglu_half(x, nw_ref, wg_ref, wu_ref, wd_ref)


def _ffn(x, nw, wg, wu, wd, mix=None):
    n, d = x.shape
    f = wg.shape[1]
    tm = FFN_TOKENS
    tok = lambda w: pl.BlockSpec((tm, w), lambda i: (i, 0))
    w_specs = [_const_spec((1, d)), _const_spec((d, f)), _const_spec((d, f)), _const_spec((f, d))]
    if mix is None:
        body, ins, specs = _ffn_kernel, (x,), [tok(d)]
    else:
        ya, ym, wo = mix
        body, ins = _outproj_ffn_kernel, (x, ya, ym, wo)
        specs = [tok(d), tok(ya.shape[1]), tok(ym.shape[1]), _const_spec(wo.shape)]
    return pl.pallas_call(
        body,
        grid=(n // tm,),
        in_specs=specs + w_specs,
        out_specs=tok(d),
        out_shape=jax.ShapeDtypeStruct((n, d), F32),
        compiler_params=pltpu.CompilerParams(dimension_semantics=("parallel",), vmem_limit_bytes=VMEM_LIMIT),
        name="ffn" if mix is None else "outproj_ffn",
    )(*ins, nw, wg, wu, wd)


def _inproj_kernel(tiles_per_seq, x_ref, nw_ref, w_ref, wgr_ref, gbr_ref, qnw_ref, knw_ref, cw_ref, cb_ref,
                   qa_ref, ka_ref, va_ref, qm_ref, km_ref, vm_ref, om_ref, gr_ref, cpad):
    tm = x_ref.shape[0]
    aw = qa_ref.shape[1]
    mw = qm_ref.shape[1]
    h = _rmsnorm(x_ref[...], nw_ref[...]).astype(BF16)
    first = lax.broadcasted_iota(jnp.int32, (1, LANES), 1) < ATTN_HEAD_DIM

    def head_norm(dst, p, w_row, scale):
        for c in range(0, aw, LANES):
            ps = p[:, c:c + LANES]
            sq = ps * ps
            ss0 = jnp.sum(jnp.where(first, sq, 0.0), axis=-1, keepdims=True)
            ss1 = jnp.sum(jnp.where(first, 0.0, sq), axis=-1, keepdims=True)
            ms = jnp.where(first, ss0, ss1) * (1.0 / ATTN_HEAD_DIM)
            dst[:, c:c + LANES] = ps * lax.rsqrt(ms + NORM_EPS) * (w_row[:, c:c + LANES] * scale)

    head_norm(qa_ref, _dot(h, w_ref[:, 0:aw]), qnw_ref[...], ATTN_HEAD_DIM ** -0.5 * LOG2E)
    head_norm(ka_ref, _dot(h, w_ref[:, aw:2 * aw]), knw_ref[...], 1.0)
    va_ref[...] = _dot(h, w_ref[:, 2 * aw:3 * aw])

    base = 3 * aw
    hist = CONV_WIDTH - 1

    nslab = 2 * mw // LANES

    @pl.when(pl.program_id(0) % tiles_per_seq == 0)
    def _():
        cpad[:, 0:SUBLANES, :] = jnp.zeros((nslab, SUBLANES, LANES), F32)

    raw = _dot(h, w_ref[:, base:base + 2 * mw])
    row0 = jnp.minimum(pl.program_id(0), 0) + (SUBLANES - hist)
    for s in range(nslab):
        lanes = slice(s * LANES, (s + 1) * LANES)
        cpad[s, SUBLANES:, :] = raw[:, lanes]
        for r in range(0, tm, CONV_ROWS):
            y = cb_ref[:, lanes]
            for j in range(CONV_WIDTH):
                y = y + cpad[s, pl.ds(row0 + r + j, CONV_ROWS), :] * cw_ref[j:j + 1, lanes]
            y = y * jax.nn.sigmoid(y)
            if s * LANES < mw:
                qm_ref[r:r + CONV_ROWS, lanes] = y.astype(BF16)
            else:
                y = y * (MLSTM_HEAD_DIM ** -0.5)
                km_ref[r:r + CONV_ROWS, s * LANES - mw:(s + 1) * LANES - mw] = y.astype(BF16)
        cpad[s, 0:SUBLANES, :] = cpad[s, tm:tm + SUBLANES, :]

    vm_ref[...] = _dot(h, w_ref[:, base + 2 * mw:base + 3 * mw]).astype(BF16)
    om_ref[...] = jax.nn.sigmoid(_dot(h, w_ref[:, base + 3 * mw:base + 4 * mw])).astype(BF16)
    ng = gr_ref.shape[0]
    gr_ref[...] = _dot_nt(wgr_ref[...], h)[:ng, :] + gbr_ref[...]


def _inproj(x, seq, nw, w_main, w_gate, gate_bias, qnw, knw, conv_w, conv_b):
    n, d = x.shape
    tm = INPROJ_TOKENS
    aw = ATTN_HEADS * ATTN_HEAD_DIM
    mw = MLSTM_HEADS * MLSTM_HEAD_DIM
    ng = 2 * MLSTM_HEADS
    wgr = jnp.zeros((BF16_ROWS, d), BF16).at[:ng, :].set(w_gate.T)
    gbr = gate_bias.reshape(ng, 1)
    qnw_t = jnp.tile(qnw, ATTN_HEADS).reshape(1, aw)
    knw_t = jnp.tile(knw, ATTN_HEADS).reshape(1, aw)
    tok = lambda w: pl.BlockSpec((tm, w), lambda i: (i, 0))
    out_shape = ([jax.ShapeDtypeStruct((n, aw), F32)] * 3 + [jax.ShapeDtypeStruct((n, mw), BF16)] * 4
                 + [jax.ShapeDtypeStruct((ng, n), F32)])
    out_specs = [tok(aw)] * 3 + [tok(mw)] * 4 + [pl.BlockSpec((ng, tm), lambda i: (0, i))]
    return pl.pallas_call(
        functools.partial(_inproj_kernel, seq // tm),
        grid=(n // tm,),
        in_specs=[tok(d), _const_spec((1, d)), _const_spec(w_main.shape), _const_spec(wgr.shape),
                  _const_spec(gbr.shape), _const_spec((1, aw)), _const_spec((1, aw)),
                  _const_spec(conv_w.shape), _const_spec((1, 2 * mw))],
        out_specs=out_specs,
        out_shape=out_shape,
        scratch_shapes=[pltpu.VMEM((2 * mw // LANES, tm + SUBLANES, LANES), F32)],
        compiler_params=pltpu.CompilerParams(dimension_semantics=("arbitrary",), vmem_limit_bytes=VMEM_LIMIT),
        name="in_proj",
    )(x, nw.reshape(1, d), w_main, wgr, gbr, qnw_t, knw_t, conv_w, conv_b.reshape(1, 2 * mw))


def _attn_kernel(slopes_ref, q_ref, k_ref, v_ref, gain_ref, o_ref, qs, kk, vv, x4f, part):
    blk = ATTN_BLOCK
    seq = q_ref.shape[1]
    nblk = seq // blk
    grp = ATTN_GROUP
    assert [d for _, d in DILATED_CONFIGS] == [1, grp, grp * grp] and nblk == grp * grp
    pair = pl.program_id(1)
    lane = lax.broadcasted_iota(jnp.int32, (blk, LANES), 1)
    first = lane < ATTN_HEAD_DIM
    slope = (slopes_ref[2 * pair], slopes_ref[2 * pair + 1])
    refs = (q_ref, k_ref, v_ref)

    def stage(order, j, q, k, v):
        qs[order, 2 * j * blk:(2 * j + 1) * blk, :] = jnp.where(first, q, 0.0).astype(BF16)
        qs[order, (2 * j + 1) * blk:(2 * j + 2) * blk, :] = jnp.where(first, 0.0, q).astype(BF16)
        kk[order, j * blk:(j + 1) * blk, :] = k.astype(BF16)
        vv[order, j * blk:(j + 1) * blk, :] = v.astype(BF16)

    for j in range(nblk):
        stage(0, j, *(ref[0, j * blk:(j + 1) * blk, :] for ref in refs))
    for j in range(nblk):
        r, c = divmod(j, grp)
        x = [ref[0, pl.ds(c * blk * grp + r, blk, stride=grp), :] for ref in refs]
        for i in range(3):
            x4f[i, j * blk:(j + 1) * blk, :] = x[i]
        stage(1, j, *x)
    for j in range(nblk):
        start = (j % grp) * (seq // grp) + j // grp
        stage(2, j, *(x4f[i, pl.ds(start, blk, stride=grp), :] for i in range(3)))

    def make_bias(dil, n_steps):
        qi = lax.broadcasted_iota(jnp.int32, (blk, 2 * blk), 0)
        ki = lax.broadcasted_iota(jnp.int32, (blk, 2 * blk), 1)
        steps = qi + blk - ki
        band = (steps >= 0) & (steps <= n_steps)
        dist = (steps * dil).astype(F32) * LOG2E
        return jnp.concatenate([jnp.where(band, -slope[h] * dist, NEG_BIG) for h in range(2)], axis=0)

    ones = jnp.ones((2 * blk, LANES), BF16)

    def scores(order, off, kcat, bias):
        s = _dot_nt(qs[order, pl.ds(2 * off, 2 * blk), :], kcat) + bias
        m = jnp.max(s, axis=-1, keepdims=True)
        return m, jnp.exp2(s - m).astype(BF16)

    def weighted(m, p, vcat):
        r = _dot(p, jnp.concatenate([vcat, ones[:vcat.shape[0]]], axis=1))
        return (jnp.where(first, m[:blk], m[blk:]),
                jnp.where(first, r[:blk, LANES:], r[blk:, LANES:]),
                jnp.where(first, r[:blk, :LANES], r[blk:, :LANES]))

    def merge(a, b):
        m = jnp.maximum(a[0], b[0])
        wa = jnp.exp2(a[0] - m)
        wb = jnp.exp2(b[0] - m)
        return m, a[1] * wa + b[1] * wb, a[2] * wa + b[2] * wb

    def load_part(stage_i, idx):
        return tuple(part[stage_i, i, idx, :] for i in range(3))

    def store_part(stage_i, idx, val):
        for i in range(3):
            part[stage_i, i, idx, :] = val[i]

    (w1, d1), (w4, d4), (w16, d16) = DILATED_CONFIGS
    biases = (make_bias(d1, w1 // d1), make_bias(d4, w4 // d4), make_bias(d16, w16 // d16))

    def finish(order, j, m, p, vcat):
        g, c = divmod(j, grp)
        cur = pl.ds(j * blk, blk)
        res = weighted(m, p, vcat)
        if order == 2:
            store_part(0, pl.ds(c * (seq // grp) + g, blk, stride=grp), res)
        elif order == 1:
            res = merge(res, load_part(0, cur))
            store_part(1, pl.ds(c * blk * grp + g, blk, stride=grp), res)
        else:
            _, l_tot, a_tot = merge(res, load_part(1, cur))
            o = a_tot / l_tot
            sq = o * o
            ss0 = jnp.sum(jnp.where(first, sq, 0.0), axis=-1, keepdims=True)
            ss1 = jnp.sum(jnp.where(first, 0.0, sq), axis=-1, keepdims=True)
            ms = jnp.where(first, ss0, ss1) * (1.0 / ATTN_HEAD_DIM)
            o_ref[0, cur, :] = (o * lax.rsqrt(ms + NORM_EPS) * gain_ref[...]).astype(o_ref.dtype)

    pending = []
    for order, n_class in ((2, 1), (1, grp), (0, nblk)):
        bias = biases[order]
        k_prev = v_prev = None
        for j in range(nblk):
            cur = pl.ds(j * blk, blk)
            k_cur, v_cur = kk[order, cur, :], vv[order, cur, :]
            if j % n_class == 0:
                kcat, vcat, b_use = k_cur, v_cur, bias[:, blk:]
            else:
                kcat = jnp.concatenate([k_prev, k_cur], axis=0)
                vcat = jnp.concatenate([v_prev, v_cur], axis=0)
                b_use = bias
            k_prev, v_prev = k_cur, v_cur
            pending.append((order, j) + scores(order, j * blk, kcat, b_use) + (vcat,))
            while len(pending) > ATTN_AHEAD[order]:
                finish(*pending.pop(0))
    while pending:
        finish(*pending.pop(0))


def _attention(qa, ka, va, gain, slopes):
    b, seq, aw = qa.shape
    pairs = aw // LANES
    nbr = len(DILATED_CONFIGS)
    blk = pl.BlockSpec((1, seq, LANES), lambda i, p: (i, 0, p))
    return pl.pallas_call(
        _attn_kernel,
        grid=(b, pairs),
        in_specs=[pl.BlockSpec(memory_space=pltpu.SMEM), blk, blk, blk,
                  pl.BlockSpec((1, LANES), lambda i, p: (0, p))],
        out_specs=blk,
        out_shape=jax.ShapeDtypeStruct((b, seq, aw), BF16),
        scratch_shapes=[pltpu.VMEM((nbr, 2 * seq, LANES), BF16),
                        pltpu.VMEM((nbr, seq, LANES), BF16),
                        pltpu.VMEM((nbr, seq, LANES), BF16),
                        pltpu.VMEM((3, seq, LANES), F32),
                        pltpu.VMEM((2, 3, seq, LANES), F32)],
        compiler_params=pltpu.CompilerParams(dimension_semantics=("parallel", "parallel"),
                                             vmem_limit_bytes=VMEM_LIMIT),
        name="dilated_attention",
    )(slopes, qa, ka, va, gain.reshape(1, aw))


def _split2(x):
    h1 = x.astype(BF16).astype(F32)
    return h1, (x - h1).astype(BF16).astype(F32)


def _mlstm_kernel(q_ref, k_ref, v_ref, og_ref, g_ref, gain_ref, o_ref, rowq, colq, lhs_s, guard_s, upd_s):
    chunk = MLSTM_CHUNK
    dh = MLSTM_HEAD_DIM
    seq = q_ref.shape[1]
    heads = q_ref.shape[2] // dh
    nc = seq // chunk
    nr = nc * heads
    assert chunk == LANES and dh == LANES

    ti = lax.broadcasted_iota(jnp.int32, (chunk, chunk), 0)
    si = lax.broadcasted_iota(jnp.int32, (chunk, chunk), 1)
    causal = si <= ti
    eye = (si == ti).astype(BF16)
    eye2 = jnp.concatenate([eye, eye], axis=1)
    tri_t = (ti <= si).astype(BF16)
    lane = lax.broadcasted_iota(jnp.int32, (nr, chunk), 1)
    row = lax.broadcasted_iota(jnp.int32, (nr, chunk), 0)

    i_pre = g_ref[0, 0] * LOG2E
    f_pre = g_ref[0, 1]
    lf = (jnp.minimum(f_pre, 0.0) - jnp.log1p(jnp.exp(-jnp.abs(f_pre)))) * LOG2E
    b = sum(_dot(part, tri_t) for part in _split3(lf))
    u = i_pre - b
    cm = u
    span = 1
    while span < chunk:
        nxt = min(span * SCAN_RADIX, chunk)
        terms = [jnp.where(lane >= sh, pltpu.roll(cm, sh, axis=1), cm) for sh in range(span, nxt, span)]
        cm = functools.reduce(jnp.maximum, terms, cm)
        span = nxt
    b_last = jnp.broadcast_to(b[:, chunk - 1:chunk], (nr, chunk))
    u_max = jnp.broadcast_to(cm[:, chunk - 1:chunk], (nr, chunk))
    m = jnp.zeros((heads, chunk), F32)
    m_start = jnp.zeros((nr, chunk), F32)
    for c in range(nc - 1):
        rows_c = slice(c * heads, (c + 1) * heads)
        m = b_last[rows_c, :] + jnp.maximum(m, u_max[rows_c, :])
        m_start = jnp.where(row // heads == c + 1, jnp.concatenate([m] * nc, axis=0), m_start)
    m_next = b_last + jnp.maximum(m_start, u_max)
    z1, z2 = _split2(jnp.maximum(m_start, cm))
    z = z1 + z2
    r1, r2 = _split2(b + z)
    rowq[0] = u
    rowq[1] = m_start
    rowq[2] = jnp.exp2(b_last + u - m_next)
    rowq[3] = jnp.exp2(b_last + m_start - m_next)
    colq[0] = jnp.concatenate([z1, z2], axis=1)
    colq[1] = jnp.concatenate([r1, r2], axis=1)

    ones = jnp.ones((chunk, dh), BF16)

    for h in range(heads):
        cols = slice(h * dh, (h + 1) * dh)

        def independent(c, carry, h=h, cols=cols):
            rows = pl.ds(pl.multiple_of(c * chunk, chunk), chunk)
            gate_row = pl.ds(c * heads + h, 1)
            qb, kb = q_ref[0, rows, cols], k_ref[0, rows, cols]
            v_ext = jnp.concatenate([v_ref[0, rows, cols], ones], axis=1)
            tiles = []
            for j in range(2):
                v = jnp.broadcast_to(colq[j, gate_row, :], (BF16_ROWS, 2 * chunk)).astype(BF16)
                tiles.extend([v] * (chunk // BF16_ROWS))
            colf = _dot_nt(eye2, jnp.concatenate(tiles, axis=0))
            z_col, mrow_col = colf[:, :chunk], colf[:, chunk:]
            dw = jnp.where(causal, jnp.exp2(rowq[0, gate_row, :] - z_col), 0.0)
            gw = jnp.exp2(rowq[1, gate_row, :] - z_col)
            sc = _dot_nt(qb, kb) * dw
            lhs_s[rows, :] = jnp.concatenate([(gw * qb.astype(F32)).astype(BF16), sc.astype(BF16)], axis=1)
            guard_s[rows, :] = jnp.exp2(-mrow_col)
            wk_t = (kb.astype(F32).T * rowq[2, gate_row, :]).astype(BF16)
            upd_s[rows, :] = _dot(wk_t, v_ext)
            return carry

        lax.fori_loop(0, nc, independent, 0, unroll=16)

        def recurrent(c, state, h=h, cols=cols):
            rows = pl.ds(pl.multiple_of(c * chunk, chunk), chunk)
            v_ext = jnp.concatenate([v_ref[0, rows, cols], ones], axis=1)
            nd = _dot(lhs_s[rows, :], jnp.concatenate([state.astype(BF16), v_ext], axis=0))
            hidden = nd[:, :dh] / jnp.maximum(jnp.abs(nd[:, dh:]), guard_s[rows, :])
            gated = og_ref[0, rows, cols].astype(F32) * hidden
            o_ref[0, rows, cols] = _rmsnorm(gated, gain_ref[:, cols]).astype(o_ref.dtype)
            dec = rowq[3, pl.ds(c * heads + h, 1), :]
            return jnp.concatenate([dec, dec], axis=1) * state + upd_s[rows, :]

        lax.fori_loop(0, nc, recurrent, jnp.zeros((dh, 2 * dh), F32), unroll=16)


def _mlstm(qm, km, vm, om, gates, gain):
    b, seq, mw = qm.shape
    dh = MLSTM_HEAD_DIM
    nr = gates.shape[2]
    blk = pl.BlockSpec((1, seq, mw), lambda i: (i, 0, 0))
    return pl.pallas_call(
        _mlstm_kernel,
        grid=(b,),
        in_specs=[blk, blk, blk, blk,
                  pl.BlockSpec((1, 2, nr, MLSTM_CHUNK), lambda i: (i, 0, 0, 0)),
                  _const_spec((1, mw))],
        out_specs=blk,
        out_shape=jax.ShapeDtypeStruct((b, seq, mw), BF16),
        scratch_shapes=[pltpu.VMEM((4, nr, MLSTM_CHUNK), F32),
                        pltpu.VMEM((2, nr, 2 * MLSTM_CHUNK), F32),
                        pltpu.VMEM((seq, 2 * dh), BF16),
                        pltpu.VMEM((seq, dh), F32),
                        pltpu.VMEM((seq, 2 * dh), F32)],
        compiler_params=pltpu.CompilerParams(dimension_semantics=("parallel",), vmem_limit_bytes=VMEM_LIMIT),
        name="mlstm",
    )(qm, km, vm, om, gates, gain.reshape(1, mw))


def kernel(x, ffn1_norm_w, ffn1_w_gate, ffn1_w_up, ffn1_w_down, mix_norm_w, w_in, q_norm_w, k_norm_w, conv_w,
           conv_b, i_bias, f_bias, attn_out_gain, mlstm_out_gain, w_out, ffn2_norm_w, ffn2_w_gate, ffn2_w_up,
           ffn2_w_down):
    b, seq, d = x.shape
    n = b * seq
    aw = ATTN_HEADS * ATTN_HEAD_DIM
    mw = MLSTM_HEADS * MLSTM_HEAD_DIM
    n_main = 3 * aw + 4 * mw
    hm = MLSTM_HEADS
    slopes = jnp.asarray(2.0 ** (-8.0 * np.arange(1, ATTN_HEADS + 1, dtype=np.float32) / ATTN_HEADS), F32)
    xt = x.reshape(n, d)
    for l in range(ffn1_norm_w.shape[0]):
        g1, u1, d1, w_i, w_o, g2, u2, d2 = _to_bf16(ffn1_w_gate[l], ffn1_w_up[l], ffn1_w_down[l], w_in[l], w_out[l],
                                                    ffn2_w_gate[l], ffn2_w_up[l], ffn2_w_down[l])
        xt = _ffn(xt, ffn1_norm_w[l].reshape(1, d), g1, u1, d1)
        gate_bias = jnp.concatenate([i_bias[l], f_bias[l]]).astype(F32)
        qa, ka, va, qm, km, vm, om, gr = _inproj(xt, seq, mix_norm_w[l], w_i, w_in[l][:, n_main:].astype(BF16),
                                                 gate_bias, q_norm_w[l], k_norm_w[l], conv_w[l], conv_b[l])
        shp = lambda t: t.reshape(b, seq, -1)
        attn = _attention(shp(qa), shp(ka), shp(va), attn_out_gain[l], slopes)
        gates = gr.reshape(2, hm, b, seq // MLSTM_CHUNK, MLSTM_CHUNK).transpose(2, 0, 3, 1, 4)
        gates = gates.reshape(b, 2, (seq // MLSTM_CHUNK) * hm, MLSTM_CHUNK)
        mls = _mlstm(shp(qm), shp(km), shp(vm), shp(om), gates, mlstm_out_gain[l])
        xt = _ffn(xt, ffn2_norm_w[l].reshape(1, d), g2, u2, d2,
                  mix=(attn.reshape(n, aw), mls.reshape(n, mw), w_o))
    return xt.reshape(b, seq, d)
```

```python
import functools

import numpy as np
import jax
import jax.numpy as jnp
from jax import lax
from jax.experimental import pallas as pl
from jax.experimental.pallas import tpu as pltpu

F32 = jnp.float32
BF16 = jnp.bfloat16

LANES = 128
SUBLANES = 8
BF16_ROWS = 16
NORM_EPS = 1e-6
ATTN_HEADS = 8
ATTN_HEAD_DIM = 64
MLSTM_HEADS = 4
MLSTM_HEAD_DIM = 128
CONV_WIDTH = 4
DILATED_CONFIGS = ((128, 1), (512, 4), (2048, 16))
ATTN_BLOCK = 128
MLSTM_CHUNK = 128
SCAN_RADIX = 8
NEG_BIG = -1e30
ATTN_GROUP = 4
ATTN_AHEAD = (2, 2, 4)
LOG2E = 1.4426950408889634
FFN_TOKENS = 512
INPROJ_TOKENS = 1024
CONV_ROWS = 64
CAST_STEPS = 16
VMEM_LIMIT = 56 * 1024 * 1024


def _const_spec(shape):
    nd = len(shape)
    return pl.BlockSpec(shape, lambda *_: (0,) * nd, pipeline_mode=pl.Buffered(1))


def _rmsnorm(x, w):
    ms = jnp.mean(x * x, axis=-1, keepdims=True)
    return x * lax.rsqrt(ms + NORM_EPS) * w


def _dot(a, b):
    return jnp.dot(a, b, preferred_element_type=F32)


def _dot_nt(a, b):
    return lax.dot_general(a, b, (((1,), (1,)), ((), ())), preferred_element_type=F32)


def _split3(x):
    h1 = x.astype(BF16)
    r1 = x - h1.astype(F32)
    h2 = r1.astype(BF16)
    h3 = (r1 - h2.astype(F32)).astype(BF16)
    return h1, h2, h3


def _cast_kernel(*refs):
    half = len(refs) // 2
    for src, dst in zip(refs[:half], refs[half:]):
        dst[...] = src[...].astype(dst.dtype)


def _to_bf16(*ws):
    specs = [pl.BlockSpec((w.shape[0] // CAST_STEPS, w.shape[1]), lambda i: (i, 0)) for w in ws]
    return pl.pallas_call(
        _cast_kernel,
        grid=(CAST_STEPS,),
        in_specs=specs,
        out_specs=specs,
        out_shape=[jax.ShapeDtypeStruct(w.shape, BF16) for w in ws],
        compiler_params=pltpu.CompilerParams(dimension_semantics=("parallel",), vmem_limit_bytes=VMEM_LIMIT),
        name="cast_weights",
    )(*ws)


def _swiglu_half(x, nw_ref, wg_ref, wu_ref, wd_ref):
    h = _rmsnorm(x, nw_ref[...]).astype(BF16)
    g = _dot(h, wg_ref[...])
    u = _dot(h, wu_ref[...])
    a = (g * jax.nn.sigmoid(g) * u).astype(BF16)
    return x + 0.5 * _dot(a, wd_ref[...])


def _ffn_kernel(x_ref, nw_ref, wg_ref, wu_ref, wd_ref, o_ref):
    o_ref[...] = _swiglu_half(x_ref[...], nw_ref, wg_ref, wu_ref, wd_ref)


def _outproj_ffn_kernel(x_ref, ya_ref, ym_ref, wo_ref, nw_ref, wg_ref, wu_ref, wd_ref, o_ref):
    aw = ya_ref.shape[1]
    x = x_ref[...] + _dot(ya_ref[...], wo_ref[0:aw, :]) + _dot(ym_ref[...], wo_ref[aw:, :])
    o_ref[...] = _swiglu_half(x, nw_ref, wg_ref, wu_ref, wd_ref)


def _ffn(x, nw, wg, wu, wd, mix=None):
    n, d = x.shape
    f = wg.shape[1]
    tm = FFN_TOKENS
    tok = lambda w: pl.BlockSpec((tm, w), lambda i: (i, 0))
    w_specs = [_const_spec((1, d)), _const_spec((d, f)), _const_spec((d, f)), _const_spec((f, d))]
    if mix is None:
        body, ins, specs = _ffn_kernel, (x,), [tok(d)]
    else:
        ya, ym, wo = mix
        body, ins = _outproj_ffn_kernel, (x, ya, ym, wo)
        specs = [tok(d), tok(ya.shape[1]), tok(ym.shape[1]), _const_spec(wo.shape)]
    return pl.pallas_call(
        body,
        grid=(n // tm,),
        in_specs=specs + w_specs,
        out_specs=tok(d),
        out_shape=jax.ShapeDtypeStruct((n, d), F32),
        compiler_params=pltpu.CompilerParams(dimension_semantics=("parallel",), vmem_limit_bytes=VMEM_LIMIT),
        name="ffn" if mix is None else "outproj_ffn",
    )(*ins, nw, wg, wu, wd)


def _inproj_kernel(tiles_per_seq, x_ref, nw_ref, w_ref, wgr_ref, gbr_ref, qnw_ref, knw_ref, cw_ref, cb_ref,
                   qa_ref, ka_ref, va_ref, qm_ref, km_ref, vm_ref, om_ref, gr_ref, cpad):
    tm = x_ref.shape[0]
    aw = qa_ref.shape[1]
    mw = qm_ref.shape[1]
    h = _rmsnorm(x_ref[...], nw_ref[...]).astype(BF16)
    first = lax.broadcasted_iota(jnp.int32, (1, LANES), 1) < ATTN_HEAD_DIM

    def head_norm(dst, p, w_row, scale):
        for c in range(0, aw, LANES):
            ps = p[:, c:c + LANES]
            sq = ps * ps
            ss0 = jnp.sum(jnp.where(first, sq, 0.0), axis=-1, keepdims=True)
            ss1 = jnp.sum(jnp.where(first, 0.0, sq), axis=-1, keepdims=True)
            ms = jnp.where(first, ss0, ss1) * (1.0 / ATTN_HEAD_DIM)
            dst[:, c:c + LANES] = ps * lax.rsqrt(ms + NORM_EPS) * (w_row[:, c:c + LANES] * scale)

    head_norm(qa_ref, _dot(h, w_ref[:, 0:aw]), qnw_ref[...], ATTN_HEAD_DIM ** -0.5 * LOG2E)
    head_norm(ka_ref, _dot(h, w_ref[:, aw:2 * aw]), knw_ref[...], 1.0)
    va_ref[...] = _dot(h, w_ref[:, 2 * aw:3 * aw])

    base = 3 * aw
    hist = CONV_WIDTH - 1

    nslab = 2 * mw // LANES

    @pl.when(pl.program_id(0) % tiles_per_seq == 0)
    def _():
        cpad[:, 0:SUBLANES, :] = jnp.zeros((nslab, SUBLANES, LANES), F32)

    raw = _dot(h, w_ref[:, base:base + 2 * mw])
    row0 = jnp.minimum(pl.program_id(0), 0) + (SUBLANES - hist)
    for s in range(nslab):
        lanes = slice(s * LANES, (s + 1) * LANES)
        cpad[s, SUBLANES:, :] = raw[:, lanes]
        for r in range(0, tm, CONV_ROWS):
            y = cb_ref[:, lanes]
            for j in range(CONV_WIDTH):
                y = y + cpad[s, pl.ds(row0 + r + j, CONV_ROWS), :] * cw_ref[j:j + 1, lanes]
            y = y * jax.nn.sigmoid(y)
            if s * LANES < mw:
                qm_ref[r:r + CONV_ROWS, lanes] = y.astype(BF16)
            else:
                y = y * (MLSTM_HEAD_DIM ** -0.5)
                km_ref[r:r + CONV_ROWS, s * LANES - mw:(s + 1) * LANES - mw] = y.astype(BF16)
        cpad[s, 0:SUBLANES, :] = cpad[s, tm:tm + SUBLANES, :]

    vm_ref[...] = _dot(h, w_ref[:, base + 2 * mw:base + 3 * mw]).astype(BF16)
    om_ref[...] = jax.nn.sigmoid(_dot(h, w_ref[:, base + 3 * mw:base + 4 * mw])).astype(BF16)
    ng = gr_ref.shape[0]
    gr_ref[...] = _dot_nt(wgr_ref[...], h)[:ng, :] + gbr_ref[...]


def _inproj(x, seq, nw, w_main, w_gate, gate_bias, qnw, knw, conv_w, conv_b):
    n, d = x.shape
    tm = INPROJ_TOKENS
    aw = ATTN_HEADS * ATTN_HEAD_DIM
    mw = MLSTM_HEADS * MLSTM_HEAD_DIM
    ng = 2 * MLSTM_HEADS
    wgr = jnp.zeros((BF16_ROWS, d), BF16).at[:ng, :].set(w_gate.T)
    gbr = gate_bias.reshape(ng, 1)
    qnw_t = jnp.tile(qnw, ATTN_HEADS).reshape(1, aw)
    knw_t = jnp.tile(knw, ATTN_HEADS).reshape(1, aw)
    tok = lambda w: pl.BlockSpec((tm, w), lambda i: (i, 0))
    out_shape = ([jax.ShapeDtypeStruct((n, aw), F32)] * 3 + [jax.ShapeDtypeStruct((n, mw), BF16)] * 4
                 + [jax.ShapeDtypeStruct((ng, n), F32)])
    out_specs = [tok(aw)] * 3 + [tok(mw)] * 4 + [pl.BlockSpec((ng, tm), lambda i: (0, i))]
    return pl.pallas_call(
        functools.partial(_inproj_kernel, seq // tm),
        grid=(n // tm,),
        in_specs=[tok(d), _const_spec((1, d)), _const_spec(w_main.shape), _const_spec(wgr.shape),
                  _const_spec(gbr.shape), _const_spec((1, aw)), _const_spec((1, aw)),
                  _const_spec(conv_w.shape), _const_spec((1, 2 * mw))],
        out_specs=out_specs,
        out_shape=out_shape,
        scratch_shapes=[pltpu.VMEM((2 * mw // LANES, tm + SUBLANES, LANES), F32)],
        compiler_params=pltpu.CompilerParams(dimension_semantics=("arbitrary",), vmem_limit_bytes=VMEM_LIMIT),
        name="in_proj",
    )(x, nw.reshape(1, d), w_main, wgr, gbr, qnw_t, knw_t, conv_w, conv_b.reshape(1, 2 * mw))


def _attn_kernel(slopes_ref, q_ref, k_ref, v_ref, gain_ref, o_ref, qs, kk, vv, x4f, part):
    blk = ATTN_BLOCK
    seq = q_ref.shape[1]
    nblk = seq // blk
    grp = ATTN_GROUP
    assert [d for _, d in DILATED_CONFIGS] == [1, grp, grp * grp] and nblk == grp * grp
    pair = pl.program_id(1)
    lane = lax.broadcasted_iota(jnp.int32, (blk, LANES), 1)
    first = lane < ATTN_HEAD_DIM
    slope = (slopes_ref[2 * pair], slopes_ref[2 * pair + 1])
    refs = (q_ref, k_ref, v_ref)

    def stage(order, j, q, k, v):
        qs[order, 2 * j * blk:(2 * j + 1) * blk, :] = jnp.where(first, q, 0.0).astype(BF16)
        qs[order, (2 * j + 1) * blk:(2 * j + 2) * blk, :] = jnp.where(first, 0.0, q).astype(BF16)
        kk[order, j * blk:(j + 1) * blk, :] = k.astype(BF16)
        vv[order, j * blk:(j + 1) * blk, :] = v.astype(BF16)

    for j in range(nblk):
        stage(0, j, *(ref[0, j * blk:(j + 1) * blk, :] for ref in refs))
    for j in range(nblk):
        r, c = divmod(j, grp)
        x = [ref[0, pl.ds(c * blk * grp + r, blk, stride=grp), :] for ref in refs]
        for i in range(3):
            x4f[i, j * blk:(j + 1) * blk, :] = x[i]
        stage(1, j, *x)
    for j in range(nblk):
        start = (j % grp) * (seq // grp) + j // grp
        stage(2, j, *(x4f[i, pl.ds(start, blk, stride=grp), :] for i in range(3)))

    def make_bias(dil, n_steps):
        qi = lax.broadcasted_iota(jnp.int32, (blk, 2 * blk), 0)
        ki = lax.broadcasted_iota(jnp.int32, (blk, 2 * blk), 1)
        steps = qi + blk - ki
        band = (steps >= 0) & (steps <= n_steps)
        dist = (steps * dil).astype(F32) * LOG2E
        return jnp.concatenate([jnp.where(band, -slope[h] * dist, NEG_BIG) for h in range(2)], axis=0)

    ones = jnp.ones((2 * blk, LANES), BF16)

    def scores(order, off, kcat, bias):
        s = _dot_nt(qs[order, pl.ds(2 * off, 2 * blk), :], kcat) + bias
        m = jnp.max(s, axis=-1, keepdims=True)
        return m, jnp.exp2(s - m).astype(BF16)

    def weighted(m, p, vcat):
        r = _dot(p, jnp.concatenate([vcat, ones[:vcat.shape[0]]], axis=1))
        return (jnp.where(first, m[:blk], m[blk:]),
                jnp.where(first, r[:blk, LANES:], r[blk:, LANES:]),
                jnp.where(first, r[:blk, :LANES], r[blk:, :LANES]))

    def merge(a, b):
        m = jnp.maximum(a[0], b[0])
        wa = jnp.exp2(a[0] - m)
        wb = jnp.exp2(b[0] - m)
        return m, a[1] * wa + b[1] * wb, a[2] * wa + b[2] * wb

    def load_part(stage_i, idx):
        return tuple(part[stage_i, i, idx, :] for i in range(3))

    def store_part(stage_i, idx, val):
        for i in range(3):
            part[stage_i, i, idx, :] = val[i]

    (w1, d1), (w4, d4), (w16, d16) = DILATED_CONFIGS
    biases = (make_bias(d1, w1 // d1), make_bias(d4, w4 // d4), make_bias(d16, w16 // d16))

    def finish(order, j, m, p, vcat):
        g, c = divmod(j, grp)
        cur = pl.ds(j * blk, blk)
        res = weighted(m, p, vcat)
        if order == 2:
            store_part(0, pl.ds(c * (seq // grp) + g, blk, stride=grp), res)
        elif order == 1:
            res = merge(res, load_part(0, cur))
            store_part(1, pl.ds(c * blk * grp + g, blk, stride=grp), res)
        else:
            _, l_tot, a_tot = merge(res, load_part(1, cur))
            o = a_tot / l_tot
            sq = o * o
            ss0 = jnp.sum(jnp.where(first, sq, 0.0), axis=-1, keepdims=True)
            ss1 = jnp.sum(jnp.where(first, 0.0, sq), axis=-1, keepdims=True)
            ms = jnp.where(first, ss0, ss1) * (1.0 / ATTN_HEAD_DIM)
            o_ref[0, cur, :] = (o * lax.rsqrt(ms + NORM_EPS) * gain_ref[...]).astype(o_ref.dtype)

    pending = []
    for order, n_class in ((2, 1), (1, grp), (0, nblk)):
        bias = biases[order]
        k_prev = v_prev = None
        for j in range(nblk):
            cur = pl.ds(j * blk, blk)
            k_cur, v_cur = kk[order, cur, :], vv[order, cur, :]
            if j % n_class == 0:
                kcat, vcat, b_use = k_cur, v_cur, bias[:, blk:]
            else:
                kcat = jnp.concatenate([k_prev, k_cur], axis=0)
                vcat = jnp.concatenate([v_prev, v_cur], axis=0)
                b_use = bias
            k_prev, v_prev = k_cur, v_cur
            pending.append((order, j) + scores(order, j * blk, kcat, b_use) + (vcat,))
            while len(pending) > ATTN_AHEAD[order]:
                finish(*pending.pop(0))
    while pending:
        finish(*pending.pop(0))


def _attention(qa, ka, va, gain, slopes):
    b, seq, aw = qa.shape
    pairs = aw // LANES
    nbr = len(DILATED_CONFIGS)
    blk = pl.BlockSpec((1, seq, LANES), lambda i, p: (i, 0, p))
    return pl.pallas_call(
        _attn_kernel,
        grid=(b, pairs),
        in_specs=[pl.BlockSpec(memory_space=pltpu.SMEM), blk, blk, blk,
                  pl.BlockSpec((1, LANES), lambda i, p: (0, p))],
        out_specs=blk,
        out_shape=jax.ShapeDtypeStruct((b, seq, aw), BF16),
        scratch_shapes=[pltpu.VMEM((nbr, 2 * seq, LANES), BF16),
                        pltpu.VMEM((nbr, seq, LANES), BF16),
                        pltpu.VMEM((nbr, seq, LANES), BF16),
                        pltpu.VMEM((3, seq, LANES), F32),
                        pltpu.VMEM((2, 3, seq, LANES), F32)],
        compiler_params=pltpu.CompilerParams(dimension_semantics=("parallel", "parallel"),
                                             vmem_limit_bytes=VMEM_LIMIT),
        name="dilated_attention",
    )(slopes, qa, ka, va, gain.reshape(1, aw))


def _split2(x):
    h1 = x.astype(BF16).astype(F32)
    return h1, (x - h1).astype(BF16).astype(F32)


def _mlstm_kernel(q_ref, k_ref, v_ref, og_ref, g_ref, gain_ref, o_ref, rowq, colq, lhs_s, guard_s, upd_s):
    chunk = MLSTM_CHUNK
    dh = MLSTM_HEAD_DIM
    seq = q_ref.shape[1]
    heads = q_ref.shape[2] // dh
    nc = seq // chunk
    nr = nc * heads
    assert chunk == LANES and dh == LANES

    ti = lax.broadcasted_iota(jnp.int32, (chunk, chunk), 0)
    si = lax.broadcasted_iota(jnp.int32, (chunk, chunk), 1)
    causal = si <= ti
    eye = (si == ti).astype(BF16)
    eye2 = jnp.concatenate([eye, eye], axis=1)
    tri_t = (ti <= si).astype(BF16)
    lane = lax.broadcasted_iota(jnp.int32, (nr, chunk), 1)
    row = lax.broadcasted_iota(jnp.int32, (nr, chunk), 0)

    i_pre = g_ref[0, 0] * LOG2E
    f_pre = g_ref[0, 1]
    lf = (jnp.minimum(f_pre, 0.0) - jnp.log1p(jnp.exp(-jnp.abs(f_pre)))) * LOG2E
    b = sum(_dot(part, tri_t) for part in _split3(lf))
    u = i_pre - b
    cm = u
    span = 1
    while span < chunk:
        nxt = min(span * SCAN_RADIX, chunk)
        terms = [jnp.where(lane >= sh, pltpu.roll(cm, sh, axis=1), cm) for sh in range(span, nxt, span)]
        cm = functools.reduce(jnp.maximum, terms, cm)
        span = nxt
    b_last = jnp.broadcast_to(b[:, chunk - 1:chunk], (nr, chunk))
    u_max = jnp.broadcast_to(cm[:, chunk - 1:chunk], (nr, chunk))
    m = jnp.zeros((heads, chunk), F32)
    m_start = jnp.zeros((nr, chunk), F32)
    for c in range(nc - 1):
        rows_c = slice(c * heads, (c + 1) * heads)
        m = b_last[rows_c, :] + jnp.maximum(m, u_max[rows_c, :])
        m_start = jnp.where(row // heads == c + 1, jnp.concatenate([m] * nc, axis=0), m_start)
    m_next = b_last + jnp.maximum(m_start, u_max)
    z1, z2 = _split2(jnp.maximum(m_start, cm))
    z = z1 + z2
    r1, r2 = _split2(b + z)
    rowq[0] = u
    rowq[1] = m_start
    rowq[2] = jnp.exp2(b_last + u - m_next)
    rowq[3] = jnp.exp2(b_last + m_start - m_next)
    colq[0] = jnp.concatenate([z1, z2], axis=1)
    colq[1] = jnp.concatenate([r1, r2], axis=1)

    ones = jnp.ones((chunk, dh), BF16)

    for h in range(heads):
        cols = slice(h * dh, (h + 1) * dh)

        def independent(c, carry, h=h, cols=cols):
            rows = pl.ds(pl.multiple_of(c * chunk, chunk), chunk)
            gate_row = pl.ds(c * heads + h, 1)
            qb, kb = q_ref[0, rows, cols], k_ref[0, rows, cols]
            v_ext = jnp.concatenate([v_ref[0, rows, cols], ones], axis=1)
            tiles = []
            for j in range(2):
                v = jnp.broadcast_to(colq[j, gate_row, :], (BF16_ROWS, 2 * chunk)).astype(BF16)
                tiles.extend([v] * (chunk // BF16_ROWS))
            colf = _dot_nt(eye2, jnp.concatenate(tiles, axis=0))
            z_col, mrow_col = colf[:, :chunk], colf[:, chunk:]
            dw = jnp.where(causal, jnp.exp2(rowq[0, gate_row, :] - z_col), 0.0)
            gw = jnp.exp2(rowq[1, gate_row, :] - z_col)
            sc = _dot_nt(qb, kb) * dw
            lhs_s[rows, :] = jnp.concatenate([(gw * qb.astype(F32)).astype(BF16), sc.astype(BF16)], axis=1)
            guard_s[rows, :] = jnp.exp2(-mrow_col)
            wk_t = (kb.astype(F32).T * rowq[2, gate_row, :]).astype(BF16)
            upd_s[rows, :] = _dot(wk_t, v_ext)
            return carry

        lax.fori_loop(0, nc, independent, 0, unroll=16)

        def recurrent(c, state, h=h, cols=cols):
            rows = pl.ds(pl.multiple_of(c * chunk, chunk), chunk)
            v_ext = jnp.concatenate([v_ref[0, rows, cols], ones], axis=1)
            nd = _dot(lhs_s[rows, :], jnp.concatenate([state.astype(BF16), v_ext], axis=0))
            hidden = nd[:, :dh] / jnp.maximum(jnp.abs(nd[:, dh:]), guard_s[rows, :])
            gated = og_ref[0, rows, cols].astype(F32) * hidden
            o_ref[0, rows, cols] = _rmsnorm(gated, gain_ref[:, cols]).astype(o_ref.dtype)
            dec = rowq[3, pl.ds(c * heads + h, 1), :]
            return jnp.concatenate([dec, dec], axis=1) * state + upd_s[rows, :]

        lax.fori_loop(0, nc, recurrent, jnp.zeros((dh, 2 * dh), F32), unroll=16)


def _mlstm(qm, km, vm, om, gates, gain):
    b, seq, mw = qm.shape
    dh = MLSTM_HEAD_DIM
    nr = gates.shape[2]
    blk = pl.BlockSpec((1, seq, mw), lambda i: (i, 0, 0))
    return pl.pallas_call(
        _mlstm_kernel,
        grid=(b,),
        in_specs=[blk, blk, blk, blk,
                  pl.BlockSpec((1, 2, nr, MLSTM_CHUNK), lambda i: (i, 0, 0, 0)),
                  _const_spec((1, mw))],
        out_specs=blk,
        out_shape=jax.ShapeDtypeStruct((b, seq, mw), BF16),
        scratch_shapes=[pltpu.VMEM((4, nr, MLSTM_CHUNK), F32),
                        pltpu.VMEM((2, nr, 2 * MLSTM_CHUNK), F32),
                        pltpu.VMEM((seq, 2 * dh), BF16),
                        pltpu.VMEM((seq, dh), F32),
                        pltpu.VMEM((seq, 2 * dh), F32)],
        compiler_params=pltpu.CompilerParams(dimension_semantics=("parallel",), vmem_limit_bytes=VMEM_LIMIT),
        name="mlstm",
    )(qm, km, vm, om, gates, gain.reshape(1, mw))


def kernel(x, ffn1_norm_w, ffn1_w_gate, ffn1_w_up, ffn1_w_down, mix_norm_w, w_in, q_norm_w, k_norm_w, conv_w,
           conv_b, i_bias, f_bias, attn_out_gain, mlstm_out_gain, w_out, ffn2_norm_w, ffn2_w_gate, ffn2_w_up,
           ffn2_w_down):
    b, seq, d = x.shape
    n = b * seq
    aw = ATTN_HEADS * ATTN_HEAD_DIM
    mw = MLSTM_HEADS * MLSTM_HEAD_DIM
    n_main = 3 * aw + 4 * mw
    hm = MLSTM_HEADS
    slopes = jnp.asarray(2.0 ** (-8.0 * np.arange(1, ATTN_HEADS + 1, dtype=np.float32) / ATTN_HEADS), F32)
    xt = x.reshape(n, d)
    for l in range(ffn1_norm_w.shape[0]):
        g1, u1, d1, w_i, w_o, g2, u2, d2 = _to_bf16(ffn1_w_gate[l], ffn1_w_up[l], ffn1_w_down[l], w_in[l], w_out[l],
                                                    ffn2_w_gate[l], ffn2_w_up[l], ffn2_w_down[l])
        xt = _ffn(xt, ffn1_norm_w[l].reshape(1, d), g1, u1, d1)
        gate_bias = jnp.concatenate([i_bias[l], f_bias[l]]).astype(F32)
        qa, ka, va, qm, km, vm, om, gr = _inproj(xt, seq, mix_norm_w[l], w_i, w_in[l][:, n_main:].astype(BF16),
                                                 gate_bias, q_norm_w[l], k_norm_w[l], conv_w[l], conv_b[l])
        shp = lambda t: t.reshape(b, seq, -1)
        attn = _attention(shp(qa), shp(ka), shp(va), attn_out_gain[l], slopes)
        gates = gr.reshape(2, hm, b, seq // MLSTM_CHUNK, MLSTM_CHUNK).transpose(2, 0, 3, 1, 4)
        gates = gates.reshape(b, 2, (seq // MLSTM_CHUNK) * hm, MLSTM_CHUNK)
        mls = _mlstm(shp(qm), shp(km), shp(vm), shp(om), gates, mlstm_out_gain[l])
        xt = _ffn(xt, ffn2_norm_w[l].reshape(1, d), g2, u2, d2,
                  mix=(attn.reshape(n, aw), mls.reshape(n, mw), w_o))
    return xt.reshape(b, seq, d)
```

```python
import functools

import numpy as np
import jax
import jax.numpy as jnp
from jax import lax
from jax.experimental import pallas as pl
from jax.experimental.pallas import tpu as pltpu

F32 = jnp.float32
BF16 = jnp.bfloat16

LANES = 128
SUBLANES = 8
BF16_ROWS = 16
NORM_EPS = 1e-6
ATTN_HEADS = 8
ATTN_HEAD_DIM = 64
MLSTM_HEADS = 4
MLSTM_HEAD_DIM = 128
CONV_WIDTH = 4
DILATED_CONFIGS = ((128, 1), (512, 4), (2048, 16))
ATTN_BLOCK = 128
MLSTM_CHUNK = 128
SCAN_RADIX = 8
NEG_BIG = -1e30
ATTN_GROUP = 4
ATTN_AHEAD = (2, 2, 4)
LOG2E = 1.4426950408889634
FFN_TOKENS = 1024
INPROJ_TOKENS = 1024
CONV_ROWS = 64
CAST_STEPS = 16
VMEM_LIMIT = 56 * 1024 * 1024


def _const_spec(shape):
    nd = len(shape)
    return pl.BlockSpec(shape, lambda *_: (0,) * nd, pipeline_mode=pl.Buffered(1))


def _rmsnorm(x, w):
    ms = jnp.mean(x * x, axis=-1, keepdims=True)
    return x * lax.rsqrt(ms + NORM_EPS) * w


def _dot(a, b):
    return jnp.dot(a, b, preferred_element_type=F32)


def _dot_nt(a, b):
    return lax.dot_general(a, b, (((1,), (1,)), ((), ())), preferred_element_type=F32)


def _split3(x):
    h1 = x.astype(BF16)
    r1 = x - h1.astype(F32)
    h2 = r1.astype(BF16)
    h3 = (r1 - h2.astype(F32)).astype(BF16)
    return h1, h2, h3


def _cast_kernel(*refs):
    half = len(refs) // 2
    for src, dst in zip(refs[:half], refs[half:]):
        dst[...] = src[...].astype(dst.dtype)


def _to_bf16(*ws):
    specs = [pl.BlockSpec((w.shape[0] // CAST_STEPS, w.shape[1]), lambda i: (i, 0)) for w in ws]
    return pl.pallas_call(
        _cast_kernel,
        grid=(CAST_STEPS,),
        in_specs=specs,
        out_specs=specs,
        out_shape=[jax.ShapeDtypeStruct(w.shape, BF16) for w in ws],
        compiler_params=pltpu.CompilerParams(dimension_semantics=("parallel",), vmem_limit_bytes=VMEM_LIMIT),
        name="cast_weights",
    )(*ws)


def _swiglu_half(x, nw_ref, wg_ref, wu_ref, wd_ref):
    h = _rmsnorm(x, nw_ref[...]).astype(BF16)
    g = _dot(h, wg_ref[...])
    u = _dot(h, wu_ref[...])
    a = (g * jax.nn.sigmoid(g) * u).astype(BF16)
    return x + 0.5 * _dot(a, wd_ref[...])


def _ffn_kernel(x_ref, nw_ref, wg_ref, wu_ref, wd_ref, o_ref):
    o_ref[...] = _swiglu_half(x_ref[...], nw_ref, wg_ref, wu_ref, wd_ref)


def _outproj_ffn_kernel(x_ref, ya_ref, ym_ref, wo_ref, nw_ref, wg_ref, wu_ref, wd_ref, o_ref):
    aw = ya_ref.shape[1]
    x = x_ref[...] + _dot(ya_ref[...], wo_ref[0:aw, :]) + _dot(ym_ref[...], wo_ref[aw:, :])
    o_ref[...] = _swiglu_half(x, nw_ref, wg_ref, wu_ref, wd_ref)


def _ffn(x, nw, wg, wu, wd, mix=None):
    n, d = x.shape
    f = wg.shape[1]
    tm = FFN_TOKENS
    tok = lambda w: pl.BlockSpec((tm, w), lambda i: (i, 0))
    w_specs = [_const_spec((1, d)), _const_spec((d, f)), _const_spec((d, f)), _const_spec((f, d))]
    if mix is None:
        body, ins, specs = _ffn_kernel, (x,), [tok(d)]
    else:
        ya, ym, wo = mix
        body, ins = _outproj_ffn_kernel, (x, ya, ym, wo)
        specs = [tok(d), tok(ya.shape[1]), tok(ym.shape[1]), _const_spec(wo.shape)]
    return pl.pallas_call(
        body,
        grid=(n // tm,),
        in_specs=specs + w_specs,
        out_specs=tok(d),
        out_shape=jax.ShapeDtypeStruct((n, d), F32),
        compiler_params=pltpu.CompilerParams(dimension_semantics=("parallel",), vmem_limit_bytes=VMEM_LIMIT),
        name="ffn" if mix is None else "outproj_ffn",
    )(*ins, nw, wg, wu, wd)


def _inproj_kernel(tiles_per_seq, x_ref, nw_ref, w_ref, wgr_ref, gbr_ref, qnw_ref, knw_ref, cw_ref, cb_ref,
                   qa_ref, ka_ref, va_ref, qm_ref, km_ref, vm_ref, om_ref, gr_ref, cpad):
    tm = x_ref.shape[0]
    aw = qa_ref.shape[1]
    mw = qm_ref.shape[1]
    h = _rmsnorm(x_ref[...], nw_ref[...]).astype(BF16)
    first = lax.broadcasted_iota(jnp.int32, (1, LANES), 1) < ATTN_HEAD_DIM

    def head_norm(dst, p, w_row, scale):
        for c in range(0, aw, LANES):
            ps = p[:, c:c + LANES]
            sq = ps * ps
            ss0 = jnp.sum(jnp.where(first, sq, 0.0), axis=-1, keepdims=True)
            ss1 = jnp.sum(jnp.where(first, 0.0, sq), axis=-1, keepdims=True)
            ms = jnp.where(first, ss0, ss1) * (1.0 / ATTN_HEAD_DIM)
            dst[:, c:c + LANES] = ps * lax.rsqrt(ms + NORM_EPS) * (w_row[:, c:c + LANES] * scale)

    head_norm(qa_ref, _dot(h, w_ref[:, 0:aw]), qnw_ref[...], ATTN_HEAD_DIM ** -0.5 * LOG2E)
    head_norm(ka_ref, _dot(h, w_ref[:, aw:2 * aw]), knw_ref[...], 1.0)
    va_ref[...] = _dot(h, w_ref[:, 2 * aw:3 * aw])

    base = 3 * aw
    hist = CONV_WIDTH - 1

    nslab = 2 * mw // LANES

    @pl.when(pl.program_id(0) % tiles_per_seq == 0)
    def _():
        cpad[:, 0:SUBLANES, :] = jnp.zeros((nslab, SUBLANES, LANES), F32)

    raw = _dot(h, w_ref[:, base:base + 2 * mw])
    row0 = jnp.minimum(pl.program_id(0), 0) + (SUBLANES - hist)
    for s in range(nslab):
        lanes = slice(s * LANES, (s + 1) * LANES)
        cpad[s, SUBLANES:, :] = raw[:, lanes]
        for r in range(0, tm, CONV_ROWS):
            y = cb_ref[:, lanes]
            for j in range(CONV_WIDTH):
                y = y + cpad[s, pl.ds(row0 + r + j, CONV_ROWS), :] * cw_ref[j:j + 1, lanes]
            y = y * jax.nn.sigmoid(y)
            if s * LANES < mw:
                qm_ref[r:r + CONV_ROWS, lanes] = y.astype(BF16)
            else:
                y = y * (MLSTM_HEAD_DIM ** -0.5)
                km_ref[r:r + CONV_ROWS, s * LANES - mw:(s + 1) * LANES - mw] = y.astype(BF16)
        cpad[s, 0:SUBLANES, :] = cpad[s, tm:tm + SUBLANES, :]

    vm_ref[...] = _dot(h, w_ref[:, base + 2 * mw:base + 3 * mw]).astype(BF16)
    om_ref[...] = jax.nn.sigmoid(_dot(h, w_ref[:, base + 3 * mw:base + 4 * mw])).astype(BF16)
    ng = gr_ref.shape[0]
    gr_ref[...] = _dot_nt(wgr_ref[...], h)[:ng, :] + gbr_ref[...]


def _inproj(x, seq, nw, w_main, w_gate, gate_bias, qnw, knw, conv_w, conv_b):
    n, d = x.shape
    tm = INPROJ_TOKENS
    aw = ATTN_HEADS * ATTN_HEAD_DIM
    mw = MLSTM_HEADS * MLSTM_HEAD_DIM
    ng = 2 * MLSTM_HEADS
    wgr = jnp.zeros((BF16_ROWS, d), BF16).at[:ng, :].set(w_gate.T)
    gbr = gate_bias.reshape(ng, 1)
    qnw_t = jnp.tile(qnw, ATTN_HEADS).reshape(1, aw)
    knw_t = jnp.tile(knw, ATTN_HEADS).reshape(1, aw)
    tok = lambda w: pl.BlockSpec((tm, w), lambda i: (i, 0))
    out_shape = ([jax.ShapeDtypeStruct((n, aw), F32)] * 3 + [jax.ShapeDtypeStruct((n, mw), BF16)] * 4
                 + [jax.ShapeDtypeStruct((ng, n), F32)])
    out_specs = [tok(aw)] * 3 + [tok(mw)] * 4 + [pl.BlockSpec((ng, tm), lambda i: (0, i))]
    return pl.pallas_call(
        functools.partial(_inproj_kernel, seq // tm),
        grid=(n // tm,),
        in_specs=[tok(d), _const_spec((1, d)), _const_spec(w_main.shape), _const_spec(wgr.shape),
                  _const_spec(gbr.shape), _const_spec((1, aw)), _const_spec((1, aw)),
                  _const_spec(conv_w.shape), _const_spec((1, 2 * mw))],
        out_specs=out_specs,
        out_shape=out_shape,
        scratch_shapes=[pltpu.VMEM((2 * mw // LANES, tm + SUBLANES, LANES), F32)],
        compiler_params=pltpu.CompilerParams(dimension_semantics=("arbitrary",), vmem_limit_bytes=VMEM_LIMIT),
        name="in_proj",
    )(x, nw.reshape(1, d), w_main, wgr, gbr, qnw_t, knw_t, conv_w, conv_b.reshape(1, 2 * mw))


def _attn_kernel(slopes_ref, q_ref, k_ref, v_ref, gain_ref, o_ref, qs, kk, vv, x4f, part):
    blk = ATTN_BLOCK
    seq = q_ref.shape[1]
    nblk = seq // blk
    grp = ATTN_GROUP
    assert [d for _, d in DILATED_CONFIGS] == [1, grp, grp * grp] and nblk == grp * grp
    pair = pl.program_id(1)
    lane = lax.broadcasted_iota(jnp.int32, (blk, LANES), 1)
    first = lane < ATTN_HEAD_DIM
    slope = (slopes_ref[2 * pair], slopes_ref[2 * pair + 1])
    refs = (q_ref, k_ref, v_ref)

    def stage(order, j, q, k, v):
        qs[order, 2 * j * blk:(2 * j + 1) * blk, :] = jnp.where(first, q, 0.0).astype(BF16)
        qs[order, (2 * j + 1) * blk:(2 * j + 2) * blk, :] = jnp.where(first, 0.0, q).astype(BF16)
        kk[order, j * blk:(j + 1) * blk, :] = k.astype(BF16)
        vv[order, j * blk:(j + 1) * blk, :] = v.astype(BF16)

    for j in range(nblk):
        stage(0, j, *(ref[0, j * blk:(j + 1) * blk, :] for ref in refs))
    for j in range(nblk):
        r, c = divmod(j, grp)
        x = [ref[0, pl.ds(c * blk * grp + r, blk, stride=grp), :] for ref in refs]
        for i in range(3):
            x4f[i, j * blk:(j + 1) * blk, :] = x[i]
        stage(1, j, *x)
    for j in range(nblk):
        start = (j % grp) * (seq // grp) + j // grp
        stage(2, j, *(x4f[i, pl.ds(start, blk, stride=grp), :] for i in range(3)))

    def make_bias(dil, n_steps):
        qi = lax.broadcasted_iota(jnp.int32, (blk, 2 * blk), 0)
        ki = lax.broadcasted_iota(jnp.int32, (blk, 2 * blk), 1)
        steps = qi + blk - ki
        band = (steps >= 0) & (steps <= n_steps)
        dist = (steps * dil).astype(F32) * LOG2E
        return jnp.concatenate([jnp.where(band, -slope[h] * dist, NEG_BIG) for h in range(2)], axis=0)

    ones = jnp.ones((2 * blk, LANES), BF16)

    def scores(order, off, kcat, bias):
        s = _dot_nt(qs[order, pl.ds(2 * off, 2 * blk), :], kcat) + bias
        m = jnp.max(s, axis=-1, keepdims=True)
        return m, jnp.exp2(s - m).astype(BF16)

    def weighted(m, p, vcat):
        r = _dot(p, jnp.concatenate([vcat, ones[:vcat.shape[0]]], axis=1))
        return (jnp.where(first, m[:blk], m[blk:]),
                jnp.where(first, r[:blk, LANES:], r[blk:, LANES:]),
                jnp.where(first, r[:blk, :LANES], r[blk:, :LANES]))

    def merge(a, b):
        m = jnp.maximum(a[0], b[0])
        wa = jnp.exp2(a[0] - m)
        wb = jnp.exp2(b[0] - m)
        return m, a[1] * wa + b[1] * wb, a[2] * wa + b[2] * wb

    def load_part(stage_i, idx):
        return tuple(part[stage_i, i, idx, :] for i in range(3))

    def store_part(stage_i, idx, val):
        for i in range(3):
            part[stage_i, i, idx, :] = val[i]

    (w1, d1), (w4, d4), (w16, d16) = DILATED_CONFIGS
    biases = (make_bias(d1, w1 // d1), make_bias(d4, w4 // d4), make_bias(d16, w16 // d16))

    def finish(order, j, m, p, vcat):
        g, c = divmod(j, grp)
        cur = pl.ds(j * blk, blk)
        res = weighted(m, p, vcat)
        if order == 2:
            store_part(0, pl.ds(c * (seq // grp) + g, blk, stride=grp), res)
        elif order == 1:
            res = merge(res, load_part(0, cur))
            store_part(1, pl.ds(c * blk * grp + g, blk, stride=grp), res)
        else:
            _, l_tot, a_tot = merge(res, load_part(1, cur))
            o = a_tot / l_tot
            sq = o * o
            ss0 = jnp.sum(jnp.where(first, sq, 0.0), axis=-1, keepdims=True)
            ss1 = jnp.sum(jnp.where(first, 0.0, sq), axis=-1, keepdims=True)
            ms = jnp.where(first, ss0, ss1) * (1.0 / ATTN_HEAD_DIM)
            o_ref[0, cur, :] = (o * lax.rsqrt(ms + NORM_EPS) * gain_ref[...]).astype(o_ref.dtype)

    pending = []
    for order, n_class in ((2, 1), (1, grp), (0, nblk)):
        bias = biases[order]
        k_prev = v_prev = None
        for j in range(nblk):
            cur = pl.ds(j * blk, blk)
            k_cur, v_cur = kk[order, cur, :], vv[order, cur, :]
            if j % n_class == 0:
                kcat, vcat, b_use = k_cur, v_cur, bias[:, blk:]
            else:
                kcat = jnp.concatenate([k_prev, k_cur], axis=0)
                vcat = jnp.concatenate([v_prev, v_cur], axis=0)
                b_use = bias
            k_prev, v_prev = k_cur, v_cur
            pending.append((order, j) + scores(order, j * blk, kcat, b_use) + (vcat,))
            while len(pending) > ATTN_AHEAD[order]:
                finish(*pending.pop(0))
    while pending:
        finish(*pending.pop(0))


def _attention(qa, ka, va, gain, slopes):
    b, seq, aw = qa.shape
    pairs = aw // LANES
    nbr = len(DILATED_CONFIGS)
    blk = pl.BlockSpec((1, seq, LANES), lambda i, p: (i, 0, p))
    return pl.pallas_call(
        _attn_kernel,
        grid=(b, pairs),
        in_specs=[pl.BlockSpec(memory_space=pltpu.SMEM), blk, blk, blk,
                  pl.BlockSpec((1, LANES), lambda i, p: (0, p))],
        out_specs=blk,
        out_shape=jax.ShapeDtypeStruct((b, seq, aw), BF16),
        scratch_shapes=[pltpu.VMEM((nbr, 2 * seq, LANES), BF16),
                        pltpu.VMEM((nbr, seq, LANES), BF16),
                        pltpu.VMEM((nbr, seq, LANES), BF16),
                        pltpu.VMEM((3, seq, LANES), F32),
                        pltpu.VMEM((2, 3, seq, LANES), F32)],
        compiler_params=pltpu.CompilerParams(dimension_semantics=("parallel", "parallel"),
                                             vmem_limit_bytes=VMEM_LIMIT),
        name="dilated_attention",
    )(slopes, qa, ka, va, gain.reshape(1, aw))


def _split2(x):
    h1 = x.astype(BF16).astype(F32)
    return h1, (x - h1).astype(BF16).astype(F32)


def _mlstm_kernel(q_ref, k_ref, v_ref, og_ref, g_ref, gain_ref, o_ref, rowq, colq, lhs_s, guard_s, upd_s):
    chunk = MLSTM_CHUNK
    dh = MLSTM_HEAD_DIM
    seq = q_ref.shape[1]
    heads = q_ref.shape[2] // dh
    nc = seq // chunk
    nr = nc * heads
    assert chunk == LANES and dh == LANES

    ti = lax.broadcasted_iota(jnp.int32, (chunk, chunk), 0)
    si = lax.broadcasted_iota(jnp.int32, (chunk, chunk), 1)
    causal = si <= ti
    eye = (si == ti).astype(BF16)
    eye2 = jnp.concatenate([eye, eye], axis=1)
    tri_t = (ti <= si).astype(BF16)
    lane = lax.broadcasted_iota(jnp.int32, (nr, chunk), 1)
    row = lax.broadcasted_iota(jnp.int32, (nr, chunk), 0)

    i_pre = g_ref[0, 0] * LOG2E
    f_pre = g_ref[0, 1]
    lf = (jnp.minimum(f_pre, 0.0) - jnp.log1p(jnp.exp(-jnp.abs(f_pre)))) * LOG2E
    b = sum(_dot(part, tri_t) for part in _split3(lf))
    u = i_pre - b
    cm = u
    span = 1
    while span < chunk:
        nxt = min(span * SCAN_RADIX, chunk)
        terms = [jnp.where(lane >= sh, pltpu.roll(cm, sh, axis=1), cm) for sh in range(span, nxt, span)]
        cm = functools.reduce(jnp.maximum, terms, cm)
        span = nxt
    b_last = jnp.broadcast_to(b[:, chunk - 1:chunk], (nr, chunk))
    u_max = jnp.broadcast_to(cm[:, chunk - 1:chunk], (nr, chunk))
    m = jnp.zeros((heads, chunk), F32)
    m_start = jnp.zeros((nr, chunk), F32)
    for c in range(nc - 1):
        rows_c = slice(c * heads, (c + 1) * heads)
        m = b_last[rows_c, :] + jnp.maximum(m, u_max[rows_c, :])
        m_start = jnp.where(row // heads == c + 1, jnp.concatenate([m] * nc, axis=0), m_start)
    m_next = b_last + jnp.maximum(m_start, u_max)
    z1, z2 = _split2(jnp.maximum(m_start, cm))
    z = z1 + z2
    r1, r2 = _split2(b + z)
    rowq[0] = u
    rowq[1] = m_start
    rowq[2] = jnp.exp2(b_last + u - m_next)
    rowq[3] = jnp.exp2(b_last + m_start - m_next)
    colq[0] = jnp.concatenate([z1, z2], axis=1)
    colq[1] = jnp.concatenate([r1, r2], axis=1)

    ones = jnp.ones((chunk, dh), BF16)

    for h in range(heads):
        cols = slice(h * dh, (h + 1) * dh)

        def independent(c, carry, h=h, cols=cols):
            rows = pl.ds(pl.multiple_of(c * chunk, chunk), chunk)
            gate_row = pl.ds(c * heads + h, 1)
            qb, kb = q_ref[0, rows, cols], k_ref[0, rows, cols]
            v_ext = jnp.concatenate([v_ref[0, rows, cols], ones], axis=1)
            tiles = []
            for j in range(2):
                v = jnp.broadcast_to(colq[j, gate_row, :], (BF16_ROWS, 2 * chunk)).astype(BF16)
                tiles.extend([v] * (chunk // BF16_ROWS))
            colf = _dot_nt(eye2, jnp.concatenate(tiles, axis=0))
            z_col, mrow_col = colf[:, :chunk], colf[:, chunk:]
            dw = jnp.where(causal, jnp.exp2(rowq[0, gate_row, :] - z_col), 0.0)
            gw = jnp.exp2(rowq[1, gate_row, :] - z_col)
            sc = _dot_nt(qb, kb) * dw
            lhs_s[rows, :] = jnp.concatenate([(gw * qb.astype(F32)).astype(BF16), sc.astype(BF16)], axis=1)
            guard_s[rows, :] = jnp.exp2(-mrow_col)
            wk_t = (kb.astype(F32).T * rowq[2, gate_row, :]).astype(BF16)
            upd_s[rows, :] = _dot(wk_t, v_ext)
            return carry

        lax.fori_loop(0, nc, independent, 0, unroll=16)

        def recurrent(c, state, h=h, cols=cols):
            rows = pl.ds(pl.multiple_of(c * chunk, chunk), chunk)
            v_ext = jnp.concatenate([v_ref[0, rows, cols], ones], axis=1)
            nd = _dot(lhs_s[rows, :], jnp.concatenate([state.astype(BF16), v_ext], axis=0))
            hidden = nd[:, :dh] / jnp.maximum(jnp.abs(nd[:, dh:]), guard_s[rows, :])
            gated = og_ref[0, rows, cols].astype(F32) * hidden
            o_ref[0, rows, cols] = _rmsnorm(gated, gain_ref[:, cols]).astype(o_ref.dtype)
            dec = rowq[3, pl.ds(c * heads + h, 1), :]
            return jnp.concatenate([dec, dec], axis=1) * state + upd_s[rows, :]

        lax.fori_loop(0, nc, recurrent, jnp.zeros((dh, 2 * dh), F32), unroll=16)


def _mlstm(qm, km, vm, om, gates, gain):
    b, seq, mw = qm.shape
    dh = MLSTM_HEAD_DIM
    nr = gates.shape[2]
    blk = pl.BlockSpec((1, seq, mw), lambda i: (i, 0, 0))
    return pl.pallas_call(
        _mlstm_kernel,
        grid=(b,),
        in_specs=[blk, blk, blk, blk,
                  pl.BlockSpec((1, 2, nr, MLSTM_CHUNK), lambda i: (i, 0, 0, 0)),
                  _const_spec((1, mw))],
        out_specs=blk,
        out_shape=jax.ShapeDtypeStruct((b, seq, mw), BF16),
        scratch_shapes=[pltpu.VMEM((4, nr, MLSTM_CHUNK), F32),
                        pltpu.VMEM((2, nr, 2 * MLSTM_CHUNK), F32),
                        pltpu.VMEM((seq, 2 * dh), BF16),
                        pltpu.VMEM((seq, dh), F32),
                        pltpu.VMEM((seq, 2 * dh), F32)],
        compiler_params=pltpu.CompilerParams(dimension_semantics=("parallel",), vmem_limit_bytes=VMEM_LIMIT),
        name="mlstm",
    )(qm, km, vm, om, gates, gain.reshape(1, mw))


def kernel(x, ffn1_norm_w, ffn1_w_gate, ffn1_w_up, ffn1_w_down, mix_norm_w, w_in, q_norm_w, k_norm_w, conv_w,
           conv_b, i_bias, f_bias, attn_out_gain, mlstm_out_gain, w_out, ffn2_norm_w, ffn2_w_gate, ffn2_w_up,
           ffn2_w_down):
    b, seq, d = x.shape
    n = b * seq
    aw = ATTN_HEADS * ATTN_HEAD_DIM
    mw = MLSTM_HEADS * MLSTM_HEAD_DIM
    n_main = 3 * aw + 4 * mw
    hm = MLSTM_HEADS
    slopes = jnp.asarray(2.0 ** (-8.0 * np.arange(1, ATTN_HEADS + 1, dtype=np.float32) / ATTN_HEADS), F32)
    xt = x.reshape(n, d)
    for l in range(ffn1_norm_w.shape[0]):
        g1, u1, d1, w_i, w_o, g2, u2, d2 = _to_bf16(ffn1_w_gate[l], ffn1_w_up[l], ffn1_w_down[l], w_in[l], w_out[l],
                                                    ffn2_w_gate[l], ffn2_w_up[l], ffn2_w_down[l])
        xt = _ffn(xt, ffn1_norm_w[l].reshape(1, d), g1, u1, d1)
        gate_bias = jnp.concatenate([i_bias[l], f_bias[l]]).astype(F32)
        qa, ka, va, qm, km, vm, om, gr = _inproj(xt, seq, mix_norm_w[l], w_i, w_in[l][:, n_main:].astype(BF16),
                                                 gate_bias, q_norm_w[l], k_norm_w[l], conv_w[l], conv_b[l])
        shp = lambda t: t.reshape(b, seq, -1)
        attn = _attention(shp(qa), shp(ka), shp(va), attn_out_gain[l], slopes)
        gates = gr.reshape(2, hm, b, seq // MLSTM_CHUNK, MLSTM_CHUNK).transpose(2, 0, 3, 1, 4)
        gates = gates.reshape(b, 2, (seq // MLSTM_CHUNK) * hm, MLSTM_CHUNK)
        mls = _mlstm(shp(qm), shp(km), shp(vm), shp(om), gates, mlstm_out_gain[l])
        xt = _ffn(xt, ffn2_norm_w[l].reshape(1, d), g2, u2, d2,
                  mix=(attn.reshape(n, aw), mls.reshape(n, mw), w_o))
    return xt.reshape(b, seq, d)
```

```python
import functools

import numpy as np
import jax
import jax.numpy as jnp
from jax import lax
from jax.experimental import pallas as pl
from jax.experimental.pallas import tpu as pltpu

F32 = jnp.float32
BF16 = jnp.bfloat16

LANES = 128
SUBLANES = 8
BF16_ROWS = 16
NORM_EPS = 1e-6
ATTN_HEADS = 8
ATTN_HEAD_DIM = 64
MLSTM_HEADS = 4
MLSTM_HEAD_DIM = 128
CONV_WIDTH = 4
DILATED_CONFIGS = ((128, 1), (512, 4), (2048, 16))
ATTN_BLOCK = 128
MLSTM_CHUNK = 128
SCAN_RADIX = 8
NEG_BIG = -1e30
ATTN_GROUP = 4
ATTN_AHEAD = (2, 2, 4)
LOG2E = 1.4426950408889634
FFN_TOKENS = 1024
INPROJ_TOKENS = 1024
CONV_ROWS = 64
CAST_STEPS = 16
VMEM_LIMIT = 56 * 1024 * 1024


def _const_spec(shape):
    nd = len(shape)
    return pl.BlockSpec(shape, lambda *_: (0,) * nd, pipeline_mode=pl.Buffered(1))


def _rmsnorm(x, w):
    ms = jnp.mean(x * x, axis=-1, keepdims=True)
    return x * lax.rsqrt(ms + NORM_EPS) * w


def _dot(a, b):
    return jnp.dot(a, b, preferred_element_type=F32)


def _dot_nt(a, b):
    return lax.dot_general(a, b, (((1,), (1,)), ((), ())), preferred_element_type=F32)


def _split3(x):
    h1 = x.astype(BF16)
    r1 = x - h1.astype(F32)
    h2 = r1.astype(BF16)
    h3 = (r1 - h2.astype(F32)).astype(BF16)
    return h1, h2, h3


def _cast_kernel(*refs):
    half = len(refs) // 2
    for src, dst in zip(refs[:half], refs[half:]):
        dst[...] = src[...].astype(dst.dtype)


def _to_bf16(*ws):
    specs = [pl.BlockSpec((w.shape[0] // CAST_STEPS, w.shape[1]), lambda i: (i, 0)) for w in ws]
    return pl.pallas_call(
        _cast_kernel,
        grid=(CAST_STEPS,),
        in_specs=specs,
        out_specs=specs,
        out_shape=[jax.ShapeDtypeStruct(w.shape, BF16) for w in ws],
        compiler_params=pltpu.CompilerParams(dimension_semantics=("parallel",), vmem_limit_bytes=VMEM_LIMIT),
        name="cast_weights",
    )(*ws)


def _swiglu_half(x, nw_ref, wg_ref, wu_ref, wd_ref):
    h = _rmsnorm(x, nw_ref[...]).astype(BF16)
    g = _dot(h, wg_ref[...])
    u = _dot(h, wu_ref[...])
    a = (g * jax.nn.sigmoid(g) * u).astype(BF16)
    return x + 0.5 * _dot(a, wd_ref[...])


def _ffn_kernel(x_ref, nw_ref, wg_ref, wu_ref, wd_ref, o_ref):
    o_ref[...] = _swiglu_half(x_ref[...], nw_ref, wg_ref, wu_ref, wd_ref)


def _outproj_ffn_kernel(x_ref, ya_ref, ym_ref, wo_ref, nw_ref, wg_ref, wu_ref, wd_ref, o_ref):
    aw = ya_ref.shape[1]
    x = x_ref[...] + _dot(ya_ref[...], wo_ref[0:aw, :]) + _dot(ym_ref[...], wo_ref[aw:, :])
    o_ref[...] = _swiglu_half(x, nw_ref, wg_ref, wu_ref, wd_ref)


def _ffn(x, nw, wg, wu, wd, mix=None):
    n, d = x.shape
    f = wg.shape[1]
    tm = FFN_TOKENS
    tok = lambda w: pl.BlockSpec((tm, w), lambda i: (i, 0))
    w_specs = [_const_spec((1, d)), _const_spec((d, f)), _const_spec((d, f)), _const_spec((f, d))]
    if mix is None:
        body, ins, specs = _ffn_kernel, (x,), [tok(d)]
    else:
        ya, ym, wo = mix
        body, ins = _outproj_ffn_kernel, (x, ya, ym, wo)
        specs = [tok(d), tok(ya.shape[1]), tok(ym.shape[1]), _const_spec(wo.shape)]
    return pl.pallas_call(
        body,
        grid=(n // tm,),
        in_specs=specs + w_specs,
        out_specs=tok(d),
        out_shape=jax.ShapeDtypeStruct((n, d), F32),
        compiler_params=pltpu.CompilerParams(dimension_semantics=("parallel",), vmem_limit_bytes=VMEM_LIMIT),
        name="ffn" if mix is None else "outproj_ffn",
    )(*ins, nw, wg, wu, wd)


def _inproj_kernel(tiles_per_seq, x_ref, nw_ref, w_ref, wgr_ref, gbr_ref, qnw_ref, knw_ref, cw_ref, cb_ref,
                   qa_ref, ka_ref, va_ref, qm_ref, km_ref, vm_ref, om_ref, gr_ref, cpad):
    tm = x_ref.shape[0]
    aw = qa_ref.shape[1]
    mw = qm_ref.shape[1]
    h = _rmsnorm(x_ref[...], nw_ref[...]).astype(BF16)
    first = lax.broadcasted_iota(jnp.int32, (1, LANES), 1) < ATTN_HEAD_DIM

    def head_norm(dst, p, w_row, scale):
        for c in range(0, aw, LANES):
            ps = p[:, c:c + LANES]
            sq = ps * ps
            ss0 = jnp.sum(jnp.where(first, sq, 0.0), axis=-1, keepdims=True)
            ss1 = jnp.sum(jnp.where(first, 0.0, sq), axis=-1, keepdims=True)
            ms = jnp.where(first, ss0, ss1) * (1.0 / ATTN_HEAD_DIM)
            dst[:, c:c + LANES] = ps * lax.rsqrt(ms + NORM_EPS) * (w_row[:, c:c + LANES] * scale)

    head_norm(qa_ref, _dot(h, w_ref[:, 0:aw]), qnw_ref[...], ATTN_HEAD_DIM ** -0.5 * LOG2E)
    head_norm(ka_ref, _dot(h, w_ref[:, aw:2 * aw]), knw_ref[...], 1.0)
    va_ref[...] = _dot(h, w_ref[:, 2 * aw:3 * aw])

    base = 3 * aw
    hist = CONV_WIDTH - 1

    nslab = 2 * mw // LANES

    @pl.when(pl.program_id(0) % tiles_per_seq == 0)
    def _():
        cpad[:, 0:SUBLANES, :] = jnp.zeros((nslab, SUBLANES, LANES), F32)

    raw = _dot(h, w_ref[:, base:base + 2 * mw])
    row0 = jnp.minimum(pl.program_id(0), 0) + (SUBLANES - hist)
    for s in range(nslab):
        lanes = slice(s * LANES, (s + 1) * LANES)
        cpad[s, SUBLANES:, :] = raw[:, lanes]
        for r in range(0, tm, CONV_ROWS):
            y = cb_ref[:, lanes]
            for j in range(CONV_WIDTH):
                y = y + cpad[s, pl.ds(row0 + r + j, CONV_ROWS), :] * cw_ref[j:j + 1, lanes]
            y = y * jax.nn.sigmoid(y)
            if s * LANES < mw:
                qm_ref[r:r + CONV_ROWS, lanes] = y.astype(BF16)
            else:
                y = y * (MLSTM_HEAD_DIM ** -0.5)
                km_ref[r:r + CONV_ROWS, s * LANES - mw:(s + 1) * LANES - mw] = y.astype(BF16)
        cpad[s, 0:SUBLANES, :] = cpad[s, tm:tm + SUBLANES, :]

    vm_ref[...] = _dot(h, w_ref[:, base + 2 * mw:base + 3 * mw]).astype(BF16)
    om_ref[...] = jax.nn.sigmoid(_dot(h, w_ref[:, base + 3 * mw:base + 4 * mw])).astype(BF16)
    ng, hm = gbr_ref.shape[0], gbr_ref.shape[0] // 2
    g = _dot_nt(wgr_ref[...], h)[:ng, :] + gbr_ref[...]
    for c in range(tm // MLSTM_CHUNK):
        for t in range(2):
            gr_ref[0, t, c * hm:(c + 1) * hm, :] = g[t * hm:(t + 1) * hm, c * MLSTM_CHUNK:(c + 1) * MLSTM_CHUNK]


def _inproj(x, seq, nw, w_main, w_gate, gate_bias, qnw, knw, conv_w, conv_b):
    n, d = x.shape
    tm = INPROJ_TOKENS
    aw = ATTN_HEADS * ATTN_HEAD_DIM
    mw = MLSTM_HEADS * MLSTM_HEAD_DIM
    ng = 2 * MLSTM_HEADS
    wgr = jnp.zeros((BF16_ROWS, d), BF16).at[:ng, :].set(w_gate.T)
    gbr = gate_bias.reshape(ng, 1)
    qnw_t = jnp.tile(qnw, ATTN_HEADS).reshape(1, aw)
    knw_t = jnp.tile(knw, ATTN_HEADS).reshape(1, aw)
    tok = lambda w: pl.BlockSpec((tm, w), lambda i: (i, 0))
    tps = seq // tm
    rows = (tm // MLSTM_CHUNK) * MLSTM_HEADS
    out_shape = ([jax.ShapeDtypeStruct((n, aw), F32)] * 3 + [jax.ShapeDtypeStruct((n, mw), BF16)] * 4
                 + [jax.ShapeDtypeStruct((n // seq, 2, tps * rows, MLSTM_CHUNK), F32)])
    out_specs = [tok(aw)] * 3 + [tok(mw)] * 4 + [pl.BlockSpec((1, 2, rows, MLSTM_CHUNK),
                                                               lambda i: (i // tps, 0, i % tps, 0))]
    return pl.pallas_call(
        functools.partial(_inproj_kernel, tps),
        grid=(n // tm,),
        in_specs=[tok(d), _const_spec((1, d)), _const_spec(w_main.shape), _const_spec(wgr.shape),
                  _const_spec(gbr.shape), _const_spec((1, aw)), _const_spec((1, aw)),
                  _const_spec(conv_w.shape), _const_spec((1, 2 * mw))],
        out_specs=out_specs,
        out_shape=out_shape,
        scratch_shapes=[pltpu.VMEM((2 * mw // LANES, tm + SUBLANES, LANES), F32)],
        compiler_params=pltpu.CompilerParams(dimension_semantics=("arbitrary",), vmem_limit_bytes=VMEM_LIMIT),
        name="in_proj",
    )(x, nw.reshape(1, d), w_main, wgr, gbr, qnw_t, knw_t, conv_w, conv_b.reshape(1, 2 * mw))


def _attn_kernel(slopes_ref, q_ref, k_ref, v_ref, gain_ref, o_ref, qs, kk, vv, x4f, part):
    blk = ATTN_BLOCK
    seq = q_ref.shape[1]
    nblk = seq // blk
    grp = ATTN_GROUP
    assert [d for _, d in DILATED_CONFIGS] == [1, grp, grp * grp] and nblk == grp * grp
    pair = pl.program_id(1)
    lane = lax.broadcasted_iota(jnp.int32, (blk, LANES), 1)
    first = lane < ATTN_HEAD_DIM
    slope = (slopes_ref[2 * pair], slopes_ref[2 * pair + 1])
    refs = (q_ref, k_ref, v_ref)

    def stage(order, j, q, k, v):
        qs[order, 2 * j * blk:(2 * j + 1) * blk, :] = jnp.where(first, q, 0.0).astype(BF16)
        qs[order, (2 * j + 1) * blk:(2 * j + 2) * blk, :] = jnp.where(first, 0.0, q).astype(BF16)
        kk[order, j * blk:(j + 1) * blk, :] = k.astype(BF16)
        vv[order, j * blk:(j + 1) * blk, :] = v.astype(BF16)

    for j in range(nblk):
        stage(0, j, *(ref[0, j * blk:(j + 1) * blk, :] for ref in refs))
    for j in range(nblk):
        r, c = divmod(j, grp)
        x = [ref[0, pl.ds(c * blk * grp + r, blk, stride=grp), :] for ref in refs]
        for i in range(3):
            x4f[i, j * blk:(j + 1) * blk, :] = x[i]
        stage(1, j, *x)
    for j in range(nblk):
        start = (j % grp) * (seq // grp) + j // grp
        stage(2, j, *(x4f[i, pl.ds(start, blk, stride=grp), :] for i in range(3)))

    def make_bias(dil, n_steps):
        qi = lax.broadcasted_iota(jnp.int32, (blk, 2 * blk), 0)
        ki = lax.broadcasted_iota(jnp.int32, (blk, 2 * blk), 1)
        steps = qi + blk - ki
        band = (steps >= 0) & (steps <= n_steps)
        dist = (steps * dil).astype(F32) * LOG2E
        return jnp.concatenate([jnp.where(band, -slope[h] * dist, NEG_BIG) for h in range(2)], axis=0)

    ones = jnp.ones((2 * blk, LANES), BF16)

    def scores(order, off, kcat, bias):
        s = _dot_nt(qs[order, pl.ds(2 * off, 2 * blk), :], kcat) + bias
        m = jnp.max(s, axis=-1, keepdims=True)
        return m, jnp.exp2(s - m).astype(BF16)

    def weighted(m, p, vcat):
        r = _dot(p, jnp.concatenate([vcat, ones[:vcat.shape[0]]], axis=1))
        return (jnp.where(first, m[:blk], m[blk:]),
                jnp.where(first, r[:blk, LANES:], r[blk:, LANES:]),
                jnp.where(first, r[:blk, :LANES], r[blk:, :LANES]))

    def merge(a, b):
        m = jnp.maximum(a[0], b[0])
        wa = jnp.exp2(a[0] - m)
        wb = jnp.exp2(b[0] - m)
        return m, a[1] * wa + b[1] * wb, a[2] * wa + b[2] * wb

    def load_part(stage_i, idx):
        return tuple(part[stage_i, i, idx, :] for i in range(3))

    def store_part(stage_i, idx, val):
        for i in range(3):
            part[stage_i, i, idx, :] = val[i]

    (w1, d1), (w4, d4), (w16, d16) = DILATED_CONFIGS
    biases = (make_bias(d1, w1 // d1), make_bias(d4, w4 // d4), make_bias(d16, w16 // d16))

    def finish(order, j, m, p, vcat):
        g, c = divmod(j, grp)
        cur = pl.ds(j * blk, blk)
        res = weighted(m, p, vcat)
        if order == 2:
            store_part(0, pl.ds(c * (seq // grp) + g, blk, stride=grp), res)
        elif order == 1:
            res = merge(res, load_part(0, cur))
            store_part(1, pl.ds(c * blk * grp + g, blk, stride=grp), res)
        else:
            _, l_tot, a_tot = merge(res, load_part(1, cur))
            o = a_tot / l_tot
            sq = o * o
            ss0 = jnp.sum(jnp.where(first, sq, 0.0), axis=-1, keepdims=True)
            ss1 = jnp.sum(jnp.where(first, 0.0, sq), axis=-1, keepdims=True)
            ms = jnp.where(first, ss0, ss1) * (1.0 / ATTN_HEAD_DIM)
            o_ref[0, cur, :] = (o * lax.rsqrt(ms + NORM_EPS) * gain_ref[...]).astype(o_ref.dtype)

    pending = []
    for order, n_class in ((2, 1), (1, grp), (0, nblk)):
        bias = biases[order]
        k_prev = v_prev = None
        for j in range(nblk):
            cur = pl.ds(j * blk, blk)
            k_cur, v_cur = kk[order, cur, :], vv[order, cur, :]
            if j % n_class == 0:
                kcat, vcat, b_use = k_cur, v_cur, bias[:, blk:]
            else:
                kcat = jnp.concatenate([k_prev, k_cur], axis=0)
                vcat = jnp.concatenate([v_prev, v_cur], axis=0)
                b_use = bias
            k_prev, v_prev = k_cur, v_cur
            pending.append((order, j) + scores(order, j * blk, kcat, b_use) + (vcat,))
            while len(pending) > ATTN_AHEAD[order]:
                finish(*pending.pop(0))
    while pending:
        finish(*pending.pop(0))


def _attention(qa, ka, va, gain, slopes):
    b, seq, aw = qa.shape
    pairs = aw // LANES
    nbr = len(DILATED_CONFIGS)
    blk = pl.BlockSpec((1, seq, LANES), lambda i, p: (i, 0, p))
    return pl.pallas_call(
        _attn_kernel,
        grid=(b, pairs),
        in_specs=[pl.BlockSpec(memory_space=pltpu.SMEM), blk, blk, blk,
                  pl.BlockSpec((1, LANES), lambda i, p: (0, p))],
        out_specs=blk,
        out_shape=jax.ShapeDtypeStruct((b, seq, aw), BF16),
        scratch_shapes=[pltpu.VMEM((nbr, 2 * seq, LANES), BF16),
                        pltpu.VMEM((nbr, seq, LANES), BF16),
                        pltpu.VMEM((nbr, seq, LANES), BF16),
                        pltpu.VMEM((3, seq, LANES), F32),
                        pltpu.VMEM((2, 3, seq, LANES), F32)],
        compiler_params=pltpu.CompilerParams(dimension_semantics=("parallel", "parallel"),
                                             vmem_limit_bytes=VMEM_LIMIT),
        name="dilated_attention",
    )(slopes, qa, ka, va, gain.reshape(1, aw))


def _split2(x):
    h1 = x.astype(BF16).astype(F32)
    return h1, (x - h1).astype(BF16).astype(F32)


def _mlstm_kernel(q_ref, k_ref, v_ref, og_ref, g_ref, gain_ref, o_ref, rowq, colq, lhs_s, guard_s, upd_s):
    chunk = MLSTM_CHUNK
    dh = MLSTM_HEAD_DIM
    seq = q_ref.shape[1]
    heads = q_ref.shape[2] // dh
    nc = seq // chunk
    nr = nc * heads
    assert chunk == LANES and dh == LANES

    ti = lax.broadcasted_iota(jnp.int32, (chunk, chunk), 0)
    si = lax.broadcasted_iota(jnp.int32, (chunk, chunk), 1)
    causal = si <= ti
    eye = (si == ti).astype(BF16)
    eye2 = jnp.concatenate([eye, eye], axis=1)
    tri_t = (ti <= si).astype(BF16)
    lane = lax.broadcasted_iota(jnp.int32, (nr, chunk), 1)
    row = lax.broadcasted_iota(jnp.int32, (nr, chunk), 0)

    i_pre = g_ref[0, 0] * LOG2E
    f_pre = g_ref[0, 1]
    lf = (jnp.minimum(f_pre, 0.0) - jnp.log1p(jnp.exp(-jnp.abs(f_pre)))) * LOG2E
    b = sum(_dot(part, tri_t) for part in _split3(lf))
    u = i_pre - b
    cm = u
    span = 1
    while span < chunk:
        nxt = min(span * SCAN_RADIX, chunk)
        terms = [jnp.where(lane >= sh, pltpu.roll(cm, sh, axis=1), cm) for sh in range(span, nxt, span)]
        cm = functools.reduce(jnp.maximum, terms, cm)
        span = nxt
    b_last = jnp.broadcast_to(b[:, chunk - 1:chunk], (nr, chunk))
    u_max = jnp.broadcast_to(cm[:, chunk - 1:chunk], (nr, chunk))
    m = jnp.zeros((heads, chunk), F32)
    m_start = jnp.zeros((nr, chunk), F32)
    for c in range(nc - 1):
        rows_c = slice(c * heads, (c + 1) * heads)
        m = b_last[rows_c, :] + jnp.maximum(m, u_max[rows_c, :])
        m_start = jnp.where(row // heads == c + 1, jnp.concatenate([m] * nc, axis=0), m_start)
    m_next = b_last + jnp.maximum(m_start, u_max)
    z1, z2 = _split2(jnp.maximum(m_start, cm))
    z = z1 + z2
    r1, r2 = _split2(b + z)
    rowq[0] = u
    rowq[1] = m_start
    rowq[2] = jnp.exp2(b_last + u - m_next)
    rowq[3] = jnp.exp2(b_last + m_start - m_next)
    colq[0] = jnp.concatenate([z1, z2], axis=1)
    colq[1] = jnp.concatenate([r1, r2], axis=1)

    ones = jnp.ones((chunk, dh), BF16)

    for h in range(heads):
        cols = slice(h * dh, (h + 1) * dh)

        def independent(c, carry, h=h, cols=cols):
            rows = pl.ds(pl.multiple_of(c * chunk, chunk), chunk)
            gate_row = pl.ds(c * heads + h, 1)
            qb, kb = q_ref[0, rows, cols], k_ref[0, rows, cols]
            v_ext = jnp.concatenate([v_ref[0, rows, cols], ones], axis=1)
            tiles = []
            for j in range(2):
                v = jnp.broadcast_to(colq[j, gate_row, :], (BF16_ROWS, 2 * chunk)).astype(BF16)
                tiles.extend([v] * (chunk // BF16_ROWS))
            colf = _dot_nt(eye2, jnp.concatenate(tiles, axis=0))
            z_col, mrow_col = colf[:, :chunk], colf[:, chunk:]
            dw = jnp.where(causal, jnp.exp2(rowq[0, gate_row, :] - z_col), 0.0)
            gw = jnp.exp2(rowq[1, gate_row, :] - z_col)
            sc = _dot_nt(qb, kb) * dw
            lhs_s[rows, :] = jnp.concatenate([(gw * qb.astype(F32)).astype(BF16), sc.astype(BF16)], axis=1)
            guard_s[rows, :] = jnp.exp2(-mrow_col)
            wk_t = (kb.astype(F32).T * rowq[2, gate_row, :]).astype(BF16)
            upd_s[rows, :] = _dot(wk_t, v_ext)
            return carry

        lax.fori_loop(0, nc, independent, 0, unroll=16)

        def recurrent(c, state, h=h, cols=cols):
            rows = pl.ds(pl.multiple_of(c * chunk, chunk), chunk)
            v_ext = jnp.concatenate([v_ref[0, rows, cols], ones], axis=1)
            nd = _dot(lhs_s[rows, :], jnp.concatenate([state.astype(BF16), v_ext], axis=0))
            hidden = nd[:, :dh] / jnp.maximum(jnp.abs(nd[:, dh:]), guard_s[rows, :])
            gated = og_ref[0, rows, cols].astype(F32) * hidden
            o_ref[0, rows, cols] = _rmsnorm(gated, gain_ref[:, cols]).astype(o_ref.dtype)
            dec = rowq[3, pl.ds(c * heads + h, 1), :]
            return jnp.concatenate([dec, dec], axis=1) * state + upd_s[rows, :]

        lax.fori_loop(0, nc, recurrent, jnp.zeros((dh, 2 * dh), F32), unroll=16)


def _mlstm(qm, km, vm, om, gates, gain):
    b, seq, mw = qm.shape
    dh = MLSTM_HEAD_DIM
    nr = gates.shape[2]
    blk = pl.BlockSpec((1, seq, mw), lambda i: (i, 0, 0))
    return pl.pallas_call(
        _mlstm_kernel,
        grid=(b,),
        in_specs=[blk, blk, blk, blk,
                  pl.BlockSpec((1, 2, nr, MLSTM_CHUNK), lambda i: (i, 0, 0, 0)),
                  _const_spec((1, mw))],
        out_specs=blk,
        out_shape=jax.ShapeDtypeStruct((b, seq, mw), BF16),
        scratch_shapes=[pltpu.VMEM((4, nr, MLSTM_CHUNK), F32),
                        pltpu.VMEM((2, nr, 2 * MLSTM_CHUNK), F32),
                        pltpu.VMEM((seq, 2 * dh), BF16),
                        pltpu.VMEM((seq, dh), F32),
                        pltpu.VMEM((seq, 2 * dh), F32)],
        compiler_params=pltpu.CompilerParams(dimension_semantics=("parallel",), vmem_limit_bytes=VMEM_LIMIT),
        name="mlstm",
    )(qm, km, vm, om, gates, gain.reshape(1, mw))


def kernel(x, ffn1_norm_w, ffn1_w_gate, ffn1_w_up, ffn1_w_down, mix_norm_w, w_in, q_norm_w, k_norm_w, conv_w,
           conv_b, i_bias, f_bias, attn_out_gain, mlstm_out_gain, w_out, ffn2_norm_w, ffn2_w_gate, ffn2_w_up,
           ffn2_w_down):
    b, seq, d = x.shape
    n = b * seq
    aw = ATTN_HEADS * ATTN_HEAD_DIM
    mw = MLSTM_HEADS * MLSTM_HEAD_DIM
    n_main = 3 * aw + 4 * mw
    hm = MLSTM_HEADS
    slopes = jnp.asarray(2.0 ** (-8.0 * np.arange(1, ATTN_HEADS + 1, dtype=np.float32) / ATTN_HEADS), F32)
    xt = x.reshape(n, d)
    for l in range(ffn1_norm_w.shape[0]):
        g1, u1, d1, w_i, w_o, g2, u2, d2 = _to_bf16(ffn1_w_gate[l], ffn1_w_up[l], ffn1_w_down[l], w_in[l], w_out[l],
                                                    ffn2_w_gate[l], ffn2_w_up[l], ffn2_w_down[l])
        xt = _ffn(xt, ffn1_norm_w[l].reshape(1, d), g1, u1, d1)
        gate_bias = jnp.concatenate([i_bias[l], f_bias[l]]).astype(F32)
        qa, ka, va, qm, km, vm, om, gates = _inproj(xt, seq, mix_norm_w[l], w_i, w_in[l][:, n_main:].astype(BF16),
                                                 gate_bias, q_norm_w[l], k_norm_w[l], conv_w[l], conv_b[l])
        shp = lambda t: t.reshape(b, seq, -1)
        attn = _attention(shp(qa), shp(ka), shp(va), attn_out_gain[l], slopes)
        mls = _mlstm(shp(qm), shp(km), shp(vm), shp(om), gates, mlstm_out_gain[l])
        xt = _ffn(xt, ffn2_norm_w[l].reshape(1, d), g2, u2, d2,
                  mix=(attn.reshape(n, aw), mls.reshape(n, mw), w_o))
    return xt.reshape(b, seq, d)
```

```python
import functools

import numpy as np
import jax
import jax.numpy as jnp
from jax import lax
from jax.experimental import pallas as pl
from jax.experimental.pallas import tpu as pltpu

F32 = jnp.float32
BF16 = jnp.bfloat16

LANES = 128
SUBLANES = 8
BF16_ROWS = 16
NORM_EPS = 1e-6
ATTN_HEADS = 8
ATTN_HEAD_DIM = 64
MLSTM_HEADS = 4
MLSTM_HEAD_DIM = 128
CONV_WIDTH = 4
DILATED_CONFIGS = ((128, 1), (512, 4), (2048, 16))
ATTN_BLOCK = 128
MLSTM_CHUNK = 128
SCAN_RADIX = 8
NEG_BIG = -1e30
ATTN_GROUP = 4
ATTN_AHEAD = (2, 2, 4)
LOG2E = 1.4426950408889634
FFN_TOKENS = 1024
INPROJ_TOKENS = 1024
CONV_ROWS = 64
CAST_STEPS = 16
VMEM_LIMIT = 56 * 1024 * 1024


def _const_spec(shape):
    nd = len(shape)
    return pl.BlockSpec(shape, lambda *_: (0,) * nd, pipeline_mode=pl.Buffered(1))


def _rmsnorm(x, w):
    ms = jnp.mean(x * x, axis=-1, keepdims=True)
    return x * lax.rsqrt(ms + NORM_EPS) * w


def _dot(a, b):
    return jnp.dot(a, b, preferred_element_type=F32)


def _dot_nt(a, b):
    return lax.dot_general(a, b, (((1,), (1,)), ((), ())), preferred_element_type=F32)


def _split3(x):
    h1 = x.astype(BF16)
    r1 = x - h1.astype(F32)
    h2 = r1.astype(BF16)
    h3 = (r1 - h2.astype(F32)).astype(BF16)
    return h1, h2, h3


def _cast_kernel(*refs):
    half = len(refs) // 2
    for src, dst in zip(refs[:half], refs[half:]):
        dst[...] = src[...].astype(dst.dtype)


def _to_bf16(*ws):
    specs = [pl.BlockSpec((w.shape[0] // CAST_STEPS, w.shape[1]), lambda i: (i, 0)) for w in ws]
    return pl.pallas_call(
        _cast_kernel,
        grid=(CAST_STEPS,),
        in_specs=specs,
        out_specs=specs,
        out_shape=[jax.ShapeDtypeStruct(w.shape, BF16) for w in ws],
        compiler_params=pltpu.CompilerParams(dimension_semantics=("parallel",), vmem_limit_bytes=VMEM_LIMIT),
        name="cast_weights",
    )(*ws)


def _swiglu_half(x, nw_ref, wg_ref, wu_ref, wd_ref):
    h = _rmsnorm(x, nw_ref[...]).astype(BF16)
    g = _dot(h, wg_ref[...])
    u = _dot(h, wu_ref[...])
    a = (g * jax.nn.sigmoid(g) * u).astype(BF16)
    return x + 0.5 * _dot(a, wd_ref[...])


def _ffn_kernel(x_ref, nw_ref, wg_ref, wu_ref, wd_ref, o_ref):
    o_ref[...] = _swiglu_half(x_ref[...], nw_ref, wg_ref, wu_ref, wd_ref)


def _outproj_ffn_kernel(x_ref, ya_ref, ym_ref, wo_ref, nw_ref, wg_ref, wu_ref, wd_ref, o_ref):
    aw = ya_ref.shape[1]
    x = x_ref[...] + _dot(ya_ref[...], wo_ref[0:aw, :]) + _dot(ym_ref[...], wo_ref[aw:, :])
    o_ref[...] = _swiglu_half(x, nw_ref, wg_ref, wu_ref, wd_ref)


def _ffn(x, nw, wg, wu, wd, mix=None):
    n, d = x.shape
    f = wg.shape[1]
    tm = FFN_TOKENS
    tok = lambda w: pl.BlockSpec((tm, w), lambda i: (i, 0))
    w_specs = [_const_spec((1, d)), _const_spec((d, f)), _const_spec((d, f)), _const_spec((f, d))]
    if mix is None:
        body, ins, specs = _ffn_kernel, (x,), [tok(d)]
    else:
        ya, ym, wo = mix
        body, ins = _outproj_ffn_kernel, (x, ya, ym, wo)
        specs = [tok(d), tok(ya.shape[1]), tok(ym.shape[1]), _const_spec(wo.shape)]
    return pl.pallas_call(
        body,
        grid=(n // tm,),
        in_specs=specs + w_specs,
        out_specs=tok(d),
        out_shape=jax.ShapeDtypeStruct((n, d), F32),
        compiler_params=pltpu.CompilerParams(dimension_semantics=("parallel",), vmem_limit_bytes=VMEM_LIMIT),
        name="ffn" if mix is None else "outproj_ffn",
    )(*ins, nw, wg, wu, wd)


def _inproj_kernel(tiles_per_seq, x_ref, nw_ref, w_ref, wgr_ref, gbr_ref, qnw_ref, knw_ref, cw_ref, cb_ref,
                   qa_ref, ka_ref, va_ref, qm_ref, km_ref, vm_ref, om_ref, gr_ref, cpad):
    tm = x_ref.shape[0]
    aw = qa_ref.shape[1]
    mw = qm_ref.shape[1]
    h = _rmsnorm(x_ref[...], nw_ref[...]).astype(BF16)
    first = lax.broadcasted_iota(jnp.int32, (1, LANES), 1) < ATTN_HEAD_DIM

    def head_norm(dst, p, w_row, scale):
        for c in range(0, aw, LANES):
            ps = p[:, c:c + LANES]
            sq = ps * ps
            ss0 = jnp.sum(jnp.where(first, sq, 0.0), axis=-1, keepdims=True)
            ss1 = jnp.sum(jnp.where(first, 0.0, sq), axis=-1, keepdims=True)
            ms = jnp.where(first, ss0, ss1) * (1.0 / ATTN_HEAD_DIM)
            dst[:, c:c + LANES] = ps * lax.rsqrt(ms + NORM_EPS) * (w_row[:, c:c + LANES] * scale)

    head_norm(qa_ref, _dot(h, w_ref[:, 0:aw]), qnw_ref[...], ATTN_HEAD_DIM ** -0.5 * LOG2E)
    head_norm(ka_ref, _dot(h, w_ref[:, aw:2 * aw]), knw_ref[...], 1.0)
    va_ref[...] = _dot(h, w_ref[:, 2 * aw:3 * aw])

    base = 3 * aw
    hist = CONV_WIDTH - 1

    nslab = 2 * mw // LANES

    @pl.when(pl.program_id(0) % tiles_per_seq == 0)
    def _():
        cpad[:, 0:SUBLANES, :] = jnp.zeros((nslab, SUBLANES, LANES), F32)

    raw = _dot(h, w_ref[:, base:base + 2 * mw])
    row0 = jnp.minimum(pl.program_id(0), 0) + (SUBLANES - hist)
    for s in range(nslab):
        lanes = slice(s * LANES, (s + 1) * LANES)
        cpad[s, SUBLANES:, :] = raw[:, lanes]
        for r in range(0, tm, CONV_ROWS):
            y = cb_ref[:, lanes]
            for j in range(CONV_WIDTH):
                y = y + cpad[s, pl.ds(row0 + r + j, CONV_ROWS), :] * cw_ref[j:j + 1, lanes]
            y = y * jax.nn.sigmoid(y)
            if s * LANES < mw:
                qm_ref[r:r + CONV_ROWS, lanes] = y.astype(BF16)
            else:
                y = y * (MLSTM_HEAD_DIM ** -0.5)
                km_ref[r:r + CONV_ROWS, s * LANES - mw:(s + 1) * LANES - mw] = y.astype(BF16)
        cpad[s, 0:SUBLANES, :] = cpad[s, tm:tm + SUBLANES, :]

    vm_ref[...] = _dot(h, w_ref[:, base + 2 * mw:base + 3 * mw]).astype(BF16)
    om_ref[...] = jax.nn.sigmoid(_dot(h, w_ref[:, base + 3 * mw:base + 4 * mw])).astype(BF16)
    ng, hm = gbr_ref.shape[0], gbr_ref.shape[0] // 2
    g = _dot_nt(wgr_ref[...], h)[:ng, :] + gbr_ref[...]
    for c in range(tm // MLSTM_CHUNK):
        for t in range(2):
            gr_ref[0, t, c * hm:(c + 1) * hm, :] = g[t * hm:(t + 1) * hm, c * MLSTM_CHUNK:(c + 1) * MLSTM_CHUNK]


def _inproj(x, seq, nw, w_main, w_gate, gate_bias, qnw, knw, conv_w, conv_b):
    n, d = x.shape
    tm = INPROJ_TOKENS
    aw = ATTN_HEADS * ATTN_HEAD_DIM
    mw = MLSTM_HEADS * MLSTM_HEAD_DIM
    ng = 2 * MLSTM_HEADS
    wgr = jnp.zeros((BF16_ROWS, d), BF16).at[:ng, :].set(w_gate.T)
    gbr = gate_bias.reshape(ng, 1)
    qnw_t = jnp.tile(qnw, ATTN_HEADS).reshape(1, aw)
    knw_t = jnp.tile(knw, ATTN_HEADS).reshape(1, aw)
    tok = lambda w: pl.BlockSpec((tm, w), lambda i: (i, 0))
    tps = seq // tm
    rows = (tm // MLSTM_CHUNK) * MLSTM_HEADS
    out_shape = ([jax.ShapeDtypeStruct((n, aw), F32)] * 3 + [jax.ShapeDtypeStruct((n, mw), BF16)] * 4
                 + [jax.ShapeDtypeStruct((n // seq, 2, tps * rows, MLSTM_CHUNK), F32)])
    out_specs = [tok(aw)] * 3 + [tok(mw)] * 4 + [pl.BlockSpec((1, 2, rows, MLSTM_CHUNK),
                                                               lambda i: (i // tps, 0, i % tps, 0))]
    return pl.pallas_call(
        functools.partial(_inproj_kernel, tps),
        grid=(n // tm,),
        in_specs=[tok(d), _const_spec((1, d)), _const_spec(w_main.shape), _const_spec(wgr.shape),
                  _const_spec(gbr.shape), _const_spec((1, aw)), _const_spec((1, aw)),
                  _const_spec(conv_w.shape), _const_spec((1, 2 * mw))],
        out_specs=out_specs,
        out_shape=out_shape,
        scratch_shapes=[pltpu.VMEM((2 * mw // LANES, tm + SUBLANES, LANES), F32)],
        compiler_params=pltpu.CompilerParams(dimension_semantics=("arbitrary",), vmem_limit_bytes=VMEM_LIMIT),
        name="in_proj",
    )(x, nw.reshape(1, d), w_main, wgr, gbr, qnw_t, knw_t, conv_w, conv_b.reshape(1, 2 * mw))


def _attn_kernel(slopes_ref, q_ref, k_ref, v_ref, gain_ref, o_ref, qs, kk, vv, x4f, part):
    blk = ATTN_BLOCK
    seq = q_ref.shape[1]
    nblk = seq // blk
    grp = ATTN_GROUP
    assert [d for _, d in DILATED_CONFIGS] == [1, grp, grp * grp] and nblk == grp * grp
    pair = pl.program_id(1)
    lane = lax.broadcasted_iota(jnp.int32, (blk, LANES), 1)
    first = lane < ATTN_HEAD_DIM
    slope = (slopes_ref[2 * pair], slopes_ref[2 * pair + 1])
    refs = (q_ref, k_ref, v_ref)

    def stage(order, j, q, k, v):
        qs[order, 2 * j * blk:(2 * j + 1) * blk, :] = jnp.where(first, q, 0.0).astype(BF16)
        qs[order, (2 * j + 1) * blk:(2 * j + 2) * blk, :] = jnp.where(first, 0.0, q).astype(BF16)
        kk[order, j * blk:(j + 1) * blk, :] = k.astype(BF16)
        vv[order, j * blk:(j + 1) * blk, :] = v.astype(BF16)

    for j in range(nblk):
        stage(0, j, *(ref[0, j * blk:(j + 1) * blk, :] for ref in refs))
    for j in range(nblk):
        r, c = divmod(j, grp)
        x = [ref[0, pl.ds(c * blk * grp + r, blk, stride=grp), :] for ref in refs]
        for i in range(3):
            x4f[i, j * blk:(j + 1) * blk, :] = x[i]
        stage(1, j, *x)
    for j in range(nblk):
        start = (j % grp) * (seq // grp) + j // grp
        stage(2, j, *(x4f[i, pl.ds(start, blk, stride=grp), :] for i in range(3)))

    def make_bias(dil, n_steps):
        qi = lax.broadcasted_iota(jnp.int32, (blk, 2 * blk), 0)
        ki = lax.broadcasted_iota(jnp.int32, (blk, 2 * blk), 1)
        steps = qi + blk - ki
        band = (steps >= 0) & (steps <= n_steps)
        dist = (steps * dil).astype(F32) * LOG2E
        return jnp.concatenate([jnp.where(band, -slope[h] * dist, NEG_BIG) for h in range(2)], axis=0)

    ones = jnp.ones((2 * blk, LANES), BF16)

    def scores(order, off, kcat, bias):
        s = _dot_nt(qs[order, pl.ds(2 * off, 2 * blk), :], kcat) + bias
        m = jnp.max(s, axis=-1, keepdims=True)
        return m, jnp.exp2(s - m).astype(BF16)

    def weighted(m, p, vcat):
        r = _dot(p, jnp.concatenate([vcat, ones[:vcat.shape[0]]], axis=1))
        return (jnp.where(first, m[:blk], m[blk:]),
                jnp.where(first, r[:blk, LANES:], r[blk:, LANES:]),
                jnp.where(first, r[:blk, :LANES], r[blk:, :LANES]))

    def merge(a, b):
        m = jnp.maximum(a[0], b[0])
        wa = jnp.exp2(a[0] - m)
        wb = jnp.exp2(b[0] - m)
        return m, a[1] * wa + b[1] * wb, a[2] * wa + b[2] * wb

    def load_part(stage_i, idx):
        return tuple(part[stage_i, i, idx, :] for i in range(3))

    def store_part(stage_i, idx, val):
        for i in range(3):
            part[stage_i, i, idx, :] = val[i]

    (w1, d1), (w4, d4), (w16, d16) = DILATED_CONFIGS
    biases = (make_bias(d1, w1 // d1), make_bias(d4, w4 // d4), make_bias(d16, w16 // d16))

    def finish(order, j, m, p, vcat):
        g, c = divmod(j, grp)
        cur = pl.ds(j * blk, blk)
        res = weighted(m, p, vcat)
        if order == 2:
            store_part(0, pl.ds(c * (seq // grp) + g, blk, stride=grp), res)
        elif order == 1:
            res = merge(res, load_part(0, cur))
            store_part(1, pl.ds(c * blk * grp + g, blk, stride=grp), res)
        else:
            _, l_tot, a_tot = merge(res, load_part(1, cur))
            o = a_tot / l_tot
            sq = o * o
            ss0 = jnp.sum(jnp.where(first, sq, 0.0), axis=-1, keepdims=True)
            ss1 = jnp.sum(jnp.where(first, 0.0, sq), axis=-1, keepdims=True)
            ms = jnp.where(first, ss0, ss1) * (1.0 / ATTN_HEAD_DIM)
            o_ref[0, cur, :] = (o * lax.rsqrt(ms + NORM_EPS) * gain_ref[...]).astype(o_ref.dtype)

    pending = []
    for order, n_class in ((2, 1), (1, grp), (0, nblk)):
        bias = biases[order]
        k_prev = v_prev = None
        for j in range(nblk):
            cur = pl.ds(j * blk, blk)
            k_cur, v_cur = kk[order, cur, :], vv[order, cur, :]
            if j % n_class == 0:
                kcat, vcat, b_use = k_cur, v_cur, bias[:, blk:]
            else:
                kcat = jnp.concatenate([k_prev, k_cur], axis=0)
                vcat = jnp.concatenate([v_prev, v_cur], axis=0)
                b_use = bias
            k_prev, v_prev = k_cur, v_cur
            pending.append((order, j) + scores(order, j * blk, kcat, b_use) + (vcat,))
            while len(pending) > ATTN_AHEAD[order]:
                finish(*pending.pop(0))
    while pending:
        finish(*pending.pop(0))


def _attention(qa, ka, va, gain, slopes):
    b, seq, aw = qa.shape
    pairs = aw // LANES
    nbr = len(DILATED_CONFIGS)
    blk = pl.BlockSpec((1, seq, LANES), lambda i, p: (i, 0, p))
    return pl.pallas_call(
        _attn_kernel,
        grid=(b, pairs),
        in_specs=[pl.BlockSpec(memory_space=pltpu.SMEM), blk, blk, blk,
                  pl.BlockSpec((1, LANES), lambda i, p: (0, p))],
        out_specs=blk,
        out_shape=jax.ShapeDtypeStruct((b, seq, aw), BF16),
        scratch_shapes=[pltpu.VMEM((nbr, 2 * seq, LANES), BF16),
                        pltpu.VMEM((nbr, seq, LANES), BF16),
                        pltpu.VMEM((nbr, seq, LANES), BF16),
                        pltpu.VMEM((3, seq, LANES), F32),
                        pltpu.VMEM((2, 3, seq, LANES), F32)],
        compiler_params=pltpu.CompilerParams(dimension_semantics=("parallel", "parallel"),
                                             vmem_limit_bytes=VMEM_LIMIT),
        name="dilated_attention",
    )(slopes, qa, ka, va, gain.reshape(1, aw))


def _split2(x):
    h1 = x.astype(BF16).astype(F32)
    return h1, (x - h1).astype(BF16).astype(F32)


def _mlstm_kernel(q_ref, k_ref, v_ref, og_ref, g_ref, gain_ref, o_ref, rowq, colq, lhs_s, guard_s, upd_s):
    chunk = MLSTM_CHUNK
    dh = MLSTM_HEAD_DIM
    seq = q_ref.shape[1]
    heads = q_ref.shape[2] // dh
    nc = seq // chunk
    nr = nc * heads
    assert chunk == LANES and dh == LANES

    ti = lax.broadcasted_iota(jnp.int32, (chunk, chunk), 0)
    si = lax.broadcasted_iota(jnp.int32, (chunk, chunk), 1)
    causal = si <= ti
    eye = (si == ti).astype(BF16)
    eye2 = jnp.concatenate([eye, eye], axis=1)
    tri_t = (ti <= si).astype(BF16)
    lane = lax.broadcasted_iota(jnp.int32, (nr, chunk), 1)
    row = lax.broadcasted_iota(jnp.int32, (nr, chunk), 0)

    i_pre = g_ref[0, 0] * LOG2E
    f_pre = g_ref[0, 1]
    lf = (jnp.minimum(f_pre, 0.0) - jnp.log1p(jnp.exp(-jnp.abs(f_pre)))) * LOG2E
    b = sum(_dot(part, tri_t) for part in _split3(lf))
    u = i_pre - b
    cm = u
    span = 1
    while span < chunk:
        nxt = min(span * SCAN_RADIX, chunk)
        terms = [jnp.where(lane >= sh, pltpu.roll(cm, sh, axis=1), cm) for sh in range(span, nxt, span)]
        cm = functools.reduce(jnp.maximum, terms, cm)
        span = nxt
    b_last = jnp.broadcast_to(b[:, chunk - 1:chunk], (nr, chunk))
    u_max = jnp.broadcast_to(cm[:, chunk - 1:chunk], (nr, chunk))
    m = jnp.zeros((heads, chunk), F32)
    m_start = jnp.zeros((nr, chunk), F32)
    for c in range(nc - 1):
        rows_c = slice(c * heads, (c + 1) * heads)
        m = b_last[rows_c, :] + jnp.maximum(m, u_max[rows_c, :])
        m_start = jnp.where(row // heads == c + 1, jnp.concatenate([m] * nc, axis=0), m_start)
    m_next = b_last + jnp.maximum(m_start, u_max)
    z1, z2 = _split2(jnp.maximum(m_start, cm))
    z = z1 + z2
    r1, r2 = _split2(b + z)
    rowq[0] = u
    rowq[1] = m_start
    rowq[2] = jnp.exp2(b_last + u - m_next)
    rowq[3] = jnp.exp2(b_last + m_start - m_next)
    colq[0] = jnp.concatenate([z1, z2], axis=1)
    colq[1] = jnp.concatenate([r1, r2], axis=1)

    ones = jnp.ones((chunk, dh), BF16)

    for h in range(heads):
        cols = slice(h * dh, (h + 1) * dh)

        def independent(c, carry, h=h, cols=cols):
            rows = pl.ds(pl.multiple_of(c * chunk, chunk), chunk)
            gate_row = pl.ds(c * heads + h, 1)
            qb, kb = q_ref[0, rows, cols], k_ref[0, rows, cols]
            v_ext = jnp.concatenate([v_ref[0, rows, cols], ones], axis=1)
            tiles = []
            for j in range(2):
                v = jnp.broadcast_to(colq[j, gate_row, :], (BF16_ROWS, 2 * chunk)).astype(BF16)
                tiles.extend([v] * (chunk // BF16_ROWS))
            colf = _dot_nt(eye2, jnp.concatenate(tiles, axis=0))
            z_col, mrow_col = colf[:, :chunk], colf[:, chunk:]
            dw = jnp.where(causal, jnp.exp2(rowq[0, gate_row, :] - z_col), 0.0)
            gw = jnp.exp2(rowq[1, gate_row, :] - z_col)
            sc = _dot_nt(qb, kb) * dw
            lhs_s[rows, :] = jnp.concatenate([(gw * qb.astype(F32)).astype(BF16), sc.astype(BF16)], axis=1)
            guard_s[rows, :] = jnp.exp2(-mrow_col)
            wk_t = (kb.astype(F32).T * rowq[2, gate_row, :]).astype(BF16)
            upd_s[rows, :] = _dot(wk_t, v_ext)
            return carry

        lax.fori_loop(0, nc, independent, 0, unroll=16)

        def recurrent(c, state, h=h, cols=cols):
            rows = pl.ds(pl.multiple_of(c * chunk, chunk), chunk)
            v_ext = jnp.concatenate([v_ref[0, rows, cols], ones], axis=1)
            nd = _dot(lhs_s[rows, :], jnp.concatenate([state.astype(BF16), v_ext], axis=0))
            hidden = nd[:, :dh] / jnp.maximum(jnp.abs(nd[:, dh:]), guard_s[rows, :])
            gated = og_ref[0, rows, cols].astype(F32) * hidden
            o_ref[0, rows, cols] = _rmsnorm(gated, gain_ref[:, cols]).astype(o_ref.dtype)
            dec = rowq[3, pl.ds(c * heads + h, 1), :]
            return jnp.concatenate([dec, dec], axis=1) * state + upd_s[rows, :]

        lax.fori_loop(0, nc, recurrent, jnp.zeros((dh, 2 * dh), F32), unroll=16)


def _mlstm(qm, km, vm, om, gates, gain):
    b, seq, mw = qm.shape
    dh = MLSTM_HEAD_DIM
    nr = gates.shape[2]
    blk = pl.BlockSpec((1, seq, mw), lambda i: (i, 0, 0))
    return pl.pallas_call(
        _mlstm_kernel,
        grid=(b,),
        in_specs=[blk, blk, blk, blk,
                  pl.BlockSpec((1, 2, nr, MLSTM_CHUNK), lambda i: (i, 0, 0, 0)),
                  _const_spec((1, mw))],
        out_specs=blk,
        out_shape=jax.ShapeDtypeStruct((b, seq, mw), BF16),
        scratch_shapes=[pltpu.VMEM((4, nr, MLSTM_CHUNK), F32),
                        pltpu.VMEM((2, nr, 2 * MLSTM_CHUNK), F32),
                        pltpu.VMEM((seq, 2 * dh), BF16),
                        pltpu.VMEM((seq, dh), F32),
                        pltpu.VMEM((seq, 2 * dh), F32)],
        compiler_params=pltpu.CompilerParams(dimension_semantics=("parallel",), vmem_limit_bytes=VMEM_LIMIT),
        name="mlstm",
    )(qm, km, vm, om, gates, gain.reshape(1, mw))


def kernel(x, ffn1_norm_w, ffn1_w_gate, ffn1_w_up, ffn1_w_down, mix_norm_w, w_in, q_norm_w, k_norm_w, conv_w,
           conv_b, i_bias, f_bias, attn_out_gain, mlstm_out_gain, w_out, ffn2_norm_w, ffn2_w_gate, ffn2_w_up,
           ffn2_w_down):
    b, seq, d = x.shape
    n = b * seq
    aw = ATTN_HEADS * ATTN_HEAD_DIM
    mw = MLSTM_HEADS * MLSTM_HEAD_DIM
    n_main = 3 * aw + 4 * mw
    slopes = jnp.asarray(2.0 ** (-8.0 * np.arange(1, ATTN_HEADS + 1, dtype=np.float32) / ATTN_HEADS), F32)
    xt = x.reshape(n, d)
    for l in range(ffn1_norm_w.shape[0]):
        g1, u1, d1, w_i, w_o, g2, u2, d2 = _to_bf16(ffn1_w_gate[l], ffn1_w_up[l], ffn1_w_down[l], w_in[l], w_out[l],
                                                    ffn2_w_gate[l], ffn2_w_up[l], ffn2_w_down[l])
        xt = _ffn(xt, ffn1_norm_w[l].reshape(1, d), g1, u1, d1)
        gate_bias = jnp.concatenate([i_bias[l], f_bias[l]]).astype(F32)
        qa, ka, va, qm, km, vm, om, gates = _inproj(xt, seq, mix_norm_w[l], w_i, w_i[:, n_main:],
                                                    gate_bias, q_norm_w[l], k_norm_w[l], conv_w[l], conv_b[l])
        shp = lambda t: t.reshape(b, seq, -1)
        attn = _attention(shp(qa), shp(ka), shp(va), attn_out_gain[l], slopes)
        mls = _mlstm(shp(qm), shp(km), shp(vm), shp(om), gates, mlstm_out_gain[l])
        xt = _ffn(xt, ffn2_norm_w[l].reshape(1, d), g2, u2, d2,
                  mix=(attn.reshape(n, aw), mls.reshape(n, mw), w_o))
    return xt.reshape(b, seq, d)
```
